```python
import math
import jax, jax.numpy as jnp
from jax import lax
import numpy as np

D_MODEL = 1024
BATCH = 8
SEQ = 4096
DEPTH = 1

CHUNK = 64
S5_WIDTH = 512
S5_GROUP = 16
S5_GROUPS = S5_WIDTH // S5_GROUP
S5_STATE = 64
DA_HEADS = 8
DA_HEAD_DIM = 64
DA_WIDTH = DA_HEADS * 2 * DA_HEAD_DIM
Q_BLOCK = 128
N_BRANCH = 2
IN_WIDTH = S5_WIDTH + 3 * DA_WIDTH + N_BRANCH * D_MODEL
N_GROUPS = 4
EXP_PER_GROUP = 8
N_EXPERTS = N_GROUPS * EXP_PER_GROUP
D_EXPERT = 256
TOP_K = 2
N_MOD = 6
EPS = 1e-6

kernel_name = "hybrid_s5_diffattn_hmoe_block"


def rms_norm(x, g):
    xf = x.astype(jnp.float32)
    y = xf * lax.rsqrt(jnp.mean(xf * xf, axis=-1, keepdims=True) + EPS)
    return (y * g.astype(jnp.float32)).astype(x.dtype)


def _complex_affine_combine(e1, e2):
    a1r, a1i, b1r, b1i = e1
    a2r, a2i, b2r, b2i = e2
    ar = a2r * a1r - a2i * a1i
    ai = a2r * a1i + a2i * a1r
    br = a2r * b1r - a2i * b1i + b2r
    bi = a2r * b1i + a2i * b1r + b2i
    return (ar, ai, br, bi)


def s5_branch(u, a_re, a_im, b_re, b_im, c_re, c_im, d_skip, log_dt, w_glu, b_glu):
    nb, L, _ = u.shape
    uf = u.astype(jnp.float32)
    ug = uf.reshape(nb, L, S5_GROUPS, S5_GROUP)
    dt = jnp.exp(log_dt.astype(jnp.float32))[:, None]
    lr = a_re.astype(jnp.float32)
    li = a_im.astype(jnp.float32)
    mag = jnp.exp(lr * dt)
    abar_re = mag * jnp.cos(li * dt)
    abar_im = mag * jnp.sin(li * dt)
    den = lr * lr + li * li
    nr = abar_re - 1.0
    f_re = (nr * lr + abar_im * li) / den
    f_im = (abar_im * lr - nr * li) / den
    bb_re = f_re[..., None] * b_re - f_im[..., None] * b_im
    bb_im = f_re[..., None] * b_im + f_im[..., None] * b_re
    bu_re = jnp.einsum('blgc,gpc->blgp', ug, bb_re)
    bu_im = jnp.einsum('blgc,gpc->blgp', ug, bb_im)
    a_seq_re = jnp.broadcast_to(abar_re[None, None], (1, L, S5_GROUPS, S5_STATE))
    a_seq_im = jnp.broadcast_to(abar_im[None, None], (1, L, S5_GROUPS, S5_STATE))
    _, _, x_re, x_im = lax.associative_scan(
        _complex_affine_combine, (a_seq_re, a_seq_im, bu_re, bu_im), axis=1)
    y = (jnp.einsum('blgp,gcp->blgc', x_re, c_re)
         - jnp.einsum('blgp,gcp->blgc', x_im, c_im))
    y = y.reshape(nb, L, S5_WIDTH) + d_skip * uf
    y = jax.nn.gelu(y)
    y = y * jax.nn.sigmoid(y @ w_glu + b_glu)
    return y.astype(u.dtype)


def diff_attention(q, k, v, lq1, lk1, lq2, lk2, g_subln, lambda_init):
    nb, L, _ = q.shape
    q = q.reshape(nb, L, DA_HEADS, 2, DA_HEAD_DIM)
    k = k.reshape(nb, L, DA_HEADS, 2, DA_HEAD_DIM)
    v = v.reshape(nb, L, DA_HEADS, 2 * DA_HEAD_DIM)
    lam = (jnp.exp(jnp.sum(lq1.astype(jnp.float32) * lk1.astype(jnp.float32)))
           - jnp.exp(jnp.sum(lq2.astype(jnp.float32) * lk2.astype(jnp.float32)))
           + lambda_init)
    n_blk = L // Q_BLOCK
    qb = q.reshape(nb, n_blk, Q_BLOCK, DA_HEADS, 2, DA_HEAD_DIM).swapaxes(0, 1)
    k_chunk = jnp.arange(L) // CHUNK
    scale = DA_HEAD_DIM ** -0.5

    def one_block(args):
        q_blk, i = args
        s = jnp.einsum('bqhmd,bkhmd->bhmqk', q_blk, k).astype(jnp.float32) * scale
        q_chunk = (i * Q_BLOCK + jnp.arange(Q_BLOCK)) // CHUNK
        mask = k_chunk[None, :] <= q_chunk[:, None]
        s = jnp.where(mask, s, -jnp.inf)
        p = jax.nn.softmax(s, axis=-1)
        attn = p[:, :, 0] - lam * p[:, :, 1]
        return jnp.einsum('bhqk,bkhe->bqhe', attn.astype(v.dtype), v)

    o = lax.map(one_block, (qb, jnp.arange(n_blk)))
    o = o.swapaxes(0, 1).reshape(nb, L, DA_HEADS, 2 * DA_HEAD_DIM)
    o = rms_norm(o, g_subln) * (1.0 - lambda_init)
    return o.reshape(nb, L, DA_WIDTH)


def hier_moe(u, w_rg, b_rg, w_re, b_re, w_g, w_u, w_d):
    nb, L, D = u.shape
    t = u.reshape(nb * L, D)
    grp_prob = jax.nn.softmax((t @ w_rg + b_rg).astype(jnp.float32), axis=-1)
    gp, gi = lax.top_k(grp_prob, 1)
    exp_logits = (t @ w_re + b_re).astype(jnp.float32).reshape(-1, N_GROUPS, EXP_PER_GROUP)
    in_grp = jnp.take_along_axis(exp_logits, gi[:, :, None], axis=1)[:, 0]
    top_v, top_i = lax.top_k(in_grp, TOP_K)
    w = jax.nn.softmax(top_v, axis=-1) * gp
    eid = gi * EXP_PER_GROUP + top_i
    combine = jnp.einsum('tk,tke->te', w,
                         jax.nn.one_hot(eid, N_EXPERTS, dtype=jnp.float32)).astype(u.dtype)

    def expert_step(acc, params):
        wg, wu, wd, cw = params
        hdn = jax.nn.silu(t @ wg) * (t @ wu)
        return acc + cw[:, None] * (hdn @ wd), None

    y, _ = lax.scan(expert_step, jnp.zeros_like(t), (w_g, w_u, w_d, combine.T))
    return y.reshape(nb, L, D)


def setup_inputs(seed: int = 0) -> dict:
    key = jax.random.key(seed)
    ks = iter(jax.random.split(key, 40))
    f32 = jnp.float32

    def nrm(shape, scale):
        return jax.random.normal(next(ks), shape, f32) * scale

    def gain(shape):
        return 1.0 + nrm(shape, 0.01)

    n_idx = jnp.arange(S5_STATE, dtype=f32)
    return {
        "x": nrm((BATCH, SEQ, D_MODEL), 1.0),
        "c": nrm((BATCH, D_MODEL), 1.0),
        "w_ada": nrm((DEPTH, D_MODEL, N_MOD * D_MODEL), 0.5 * D_MODEL ** -0.5),
        "b_ada": nrm((DEPTH, N_MOD * D_MODEL), 0.01),
        "g_norm_mix": gain((DEPTH, D_MODEL)),
        "w_in": nrm((DEPTH, D_MODEL, IN_WIDTH), D_MODEL ** -0.5),
        "b_in": nrm((DEPTH, IN_WIDTH), 0.01),
        "s5_a_re": -0.5 + nrm((DEPTH, S5_GROUPS, S5_STATE), 0.01),
        "s5_a_im": math.pi * n_idx + nrm((DEPTH, S5_GROUPS, S5_STATE), 0.01),
        "s5_b_re": nrm((DEPTH, S5_GROUPS, S5_STATE, S5_GROUP), (2 * S5_GROUP) ** -0.5),
        "s5_b_im": nrm((DEPTH, S5_GROUPS, S5_STATE, S5_GROUP), (2 * S5_GROUP) ** -0.5),
        "s5_c_re": nrm((DEPTH, S5_GROUPS, S5_GROUP, S5_STATE), S5_STATE ** -0.5),
        "s5_c_im": nrm((DEPTH, S5_GROUPS, S5_GROUP, S5_STATE), S5_STATE ** -0.5),
        "s5_d": nrm((DEPTH, S5_WIDTH), 1.0),
        "s5_log_dt": jax.random.uniform(next(ks), (DEPTH, S5_GROUPS), f32,
                                        minval=math.log(1e-3), maxval=math.log(1e-1)),
        "w_glu": nrm((DEPTH, S5_WIDTH, S5_WIDTH), S5_WIDTH ** -0.5),
        "b_glu": nrm((DEPTH, S5_WIDTH), 0.01),
        "lambda_q1": nrm((DEPTH, DA_HEAD_DIM), 0.1),
        "lambda_k1": nrm((DEPTH, DA_HEAD_DIM), 0.1),
        "lambda_q2": nrm((DEPTH, DA_HEAD_DIM), 0.1),
        "lambda_k2": nrm((DEPTH, DA_HEAD_DIM), 0.1),
        "g_subln": gain((DEPTH, 2 * DA_HEAD_DIM)),
        "w_br_ssm": nrm((DEPTH, S5_WIDTH, D_MODEL), S5_WIDTH ** -0.5),
        "w_br_attn": nrm((DEPTH, DA_WIDTH, D_MODEL), DA_WIDTH ** -0.5),
        "w_out": nrm((DEPTH, D_MODEL, D_MODEL), D_MODEL ** -0.5),
        "g_norm_ffn": gain((DEPTH, D_MODEL)),
        "w_router_grp": nrm((DEPTH, D_MODEL, N_GROUPS), D_MODEL ** -0.5),
        "b_router_grp": nrm((DEPTH, N_GROUPS), 0.01),
        "w_router_exp": nrm((DEPTH, D_MODEL, N_EXPERTS), D_MODEL ** -0.5),
        "b_router_exp": nrm((DEPTH, N_EXPERTS), 0.01),
        "w_exp_gate": nrm((DEPTH, N_EXPERTS, D_MODEL, D_EXPERT), D_MODEL ** -0.5),
        "w_exp_up": nrm((DEPTH, N_EXPERTS, D_MODEL, D_EXPERT), D_MODEL ** -0.5),
        "w_exp_down": nrm((DEPTH, N_EXPERTS, D_EXPERT, D_MODEL), D_EXPERT ** -0.5),
        "g_final": gain((D_MODEL,)),
    }


def reference(x, c, w_ada, b_ada, g_norm_mix, w_in, b_in,
              s5_a_re, s5_a_im, s5_b_re, s5_b_im, s5_c_re, s5_c_im, s5_d, s5_log_dt,
              w_glu, b_glu, lambda_q1, lambda_k1, lambda_q2, lambda_k2, g_subln,
              w_br_ssm, w_br_attn, w_out, g_norm_ffn,
              w_router_grp, b_router_grp, w_router_exp, b_router_exp,
              w_exp_gate, w_exp_up, w_exp_down, g_final):
    h = x
    cs = jax.nn.silu(c)
    splits = [S5_WIDTH, S5_WIDTH + DA_WIDTH, S5_WIDTH + 2 * DA_WIDTH, S5_WIDTH + 3 * DA_WIDTH]
    for l in range(DEPTH):
        lambda_init = 0.8 - 0.6 * math.exp(-0.3 * l)
        mod = (cs @ w_ada[l] + b_ada[l])[:, None, :]
        sh_m, sc_m, gt_m, sh_f, sc_f, gt_f = jnp.split(mod, N_MOD, axis=-1)

        u = rms_norm(h, g_norm_mix[l]) * (1.0 + sc_m) + sh_m
        proj = u @ w_in[l] + b_in[l]
        s5_in, q, k, v, gates = jnp.split(proj, splits, axis=-1)
        y_ssm = s5_branch(s5_in, s5_a_re[l], s5_a_im[l], s5_b_re[l], s5_b_im[l],
                          s5_c_re[l], s5_c_im[l], s5_d[l], s5_log_dt[l],
                          w_glu[l], b_glu[l]) @ w_br_ssm[l]
        y_att = diff_attention(q, k, v, lambda_q1[l], lambda_k1[l], lambda_q2[l],
                               lambda_k2[l], g_subln[l], lambda_init) @ w_br_attn[l]
        g_ssm, g_att = jnp.split(jax.nn.sigmoid(gates), N_BRANCH, axis=-1)
        mix = (g_ssm * y_ssm + g_att * y_att) @ w_out[l]
        h = h + gt_m * mix

        u = rms_norm(h, g_norm_ffn[l]) * (1.0 + sc_f) + sh_f
        ffn = hier_moe(u, w_router_grp[l], b_router_grp[l], w_router_exp[l], b_router_exp[l],
                       w_exp_gate[l], w_exp_up[l], w_exp_down[l])
        h = h + gt_f * ffn
    return rms_norm(h, g_final)
```

```python
import functools
import math

import jax
import jax.numpy as jnp
from jax import lax
from jax.experimental import pallas as pl
from jax.experimental.pallas import tpu as pltpu

EPS = 1e-6
CHUNK = 64
S5_GROUP = 16
S5_STATE = 64
S5_STEP = 16
DA_HEADS = 8
DA_HEAD_DIM = 64
N_GROUPS = 4
EXP_PER_GROUP = 8
N_EXPERTS = N_GROUPS * EXP_PER_GROUP
D_EXPERT = 256
LANES = 128
VMEM_LIMIT = 56 * 1024 * 1024

F32 = jnp.float32
BF16 = jnp.bfloat16
HIGHEST = lax.Precision.HIGHEST


def _cparams(*sem):
    return pltpu.CompilerParams(dimension_semantics=sem, vmem_limit_bytes=VMEM_LIMIT)


def _dot(a, b):
    return jnp.dot(a, b, preferred_element_type=F32)


def _rms(x, g):
    return x * lax.rsqrt(jnp.mean(x * x, axis=-1, keepdims=True) + EPS) * g


def _ada_kernel(c_ref, w_ref, b_ref, o_ref):
    c = c_ref[...]
    cs = c * jax.nn.sigmoid(c)
    o_ref[...] = jnp.dot(cs, w_ref[...], preferred_element_type=F32, precision=HIGHEST) + b_ref[...]


def _ada_mod(c, w, b):
    nb, d = c.shape
    n = w.shape[1]
    tn = n // 4
    return pl.pallas_call(
        _ada_kernel,
        out_shape=jax.ShapeDtypeStruct((nb, n), F32),
        grid=(n // tn,),
        in_specs=[pl.BlockSpec((nb, d), lambda j: (0, 0)),
                  pl.BlockSpec((d, tn), lambda j: (0, j)),
                  pl.BlockSpec((1, tn), lambda j: (0, j))],
        out_specs=pl.BlockSpec((nb, tn), lambda j: (0, j)),
        compiler_params=_cparams("parallel"),
        name="ada_mod",
    )(c, w, b)


def _inproj_kernel(x_ref, mod_ref, g_ref, w_ref, b_ref, s5_ref, q_ref, k_ref, v_ref, gt_ref, *, widths):
    x = x_ref[...]
    mod = mod_ref[0]
    u = _rms(x, g_ref[...]) * (1.0 + mod[1:2, :]) + mod[0:1, :]
    ub = u.astype(BF16)
    s5w, qw, kw, vw, gw = widths
    o = 0
    s5_ref[...] = (_dot(ub, w_ref[:, o:o + s5w]) + b_ref[:, o:o + s5w]).astype(BF16)
    o += s5w
    q_ref[...] = ((_dot(ub, w_ref[:, o:o + qw]) + b_ref[:, o:o + qw]) * (DA_HEAD_DIM ** -0.5)).astype(BF16)
    o += qw
    k_ref[...] = (_dot(ub, w_ref[:, o:o + kw]) + b_ref[:, o:o + kw]).astype(BF16)
    o += kw
    v_ref[...] = (_dot(ub, w_ref[:, o:o + vw]) + b_ref[:, o:o + vw]).astype(BF16)
    o += vw
    gt_ref[...] = jax.nn.sigmoid(_dot(ub, w_ref[:, o:o + gw]) + b_ref[:, o:o + gw]).astype(BF16)


def _in_proj(x2, mod3, g, w, b, seq, widths, tm):
    t, d = x2.shape
    n = w.shape[1]
    tpb = seq // tm
    row = lambda i: (i, 0)
    const = lambda i: (0, 0)
    outs = [jax.ShapeDtypeStruct((t, wd), BF16) for wd in widths]
    return pl.pallas_call(
        functools.partial(_inproj_kernel, widths=widths),
        out_shape=outs,
        grid=(t // tm,),
        in_specs=[pl.BlockSpec((tm, d), row),
                  pl.BlockSpec((1, 6, d), lambda i: (i // tpb, 0, 0)),
                  pl.BlockSpec((1, d), const),
                  pl.BlockSpec((d, n), const),
                  pl.BlockSpec((1, n), const)],
        out_specs=[pl.BlockSpec((tm, wd), row) for wd in widths],
        compiler_params=_cparams("parallel"),
        name="in_proj",
    )(x2, mod3, g, w, b)


def _s5_prep_kernel(ar_ref, ai_ref, dt_ref, br_ref, bi_ref, crt_ref, cit_ref, cr_ref, ci_ref,
                    er_ref, ei_ref, vr_ref, vi_ref, k_ref, a16r_ref, a16i_ref):
    lr = ar_ref[0]
    li = ai_ref[0]
    dt = jnp.exp(dt_ref[0])
    mag = jnp.exp(lr * dt)
    abr = mag * jnp.cos(li * dt)
    abi = mag * jnp.sin(li * dt)
    den = lr * lr + li * li
    nr = abr - 1.0
    f_re = (nr * lr + abi * li) / den
    f_im = (abi * lr - nr * li) / den
    b_re = br_ref[0]
    b_im = bi_ref[0]
    bb_re = f_re * b_re - f_im * b_im
    bb_im = f_re * b_im + f_im * b_re
    crt = crt_ref[0]
    cit = cit_ref[0]
    cr = cr_ref[0]
    ci = ci_ref[0]
    pr = jnp.ones_like(lr)
    pi = jnp.zeros_like(lr)
    for j in range(S5_STEP):
        e_re = pr * bb_re - pi * bb_im
        e_im = pr * bb_im + pi * bb_re
        er_ref[0, j] = e_re
        ei_ref[0, j] = e_im
        k_ref[0, j] = (jnp.dot(cr, e_re, preferred_element_type=F32, precision=HIGHEST)
                       - jnp.dot(ci, e_im, preferred_element_type=F32, precision=HIGHEST))
        pr, pi = pr * abr - pi * abi, pr * abi + pi * abr
        vr_ref[0, j] = pr * crt - pi * cit
        vi_ref[0, j] = pr * cit + pi * crt
    a16r_ref[0] = pr
    a16i_ref[0] = pi


def _s5_prep(a_re, a_im, log_dt, b_re, b_im, c_re, c_im):
    ng, p = a_re.shape
    gc = b_re.shape[-1]
    col = lambda a: a.reshape(ng, p, 1)
    g3 = lambda g: (g, 0, 0)
    g4 = lambda g: (g, 0, 0, 0)
    spec_p1 = pl.BlockSpec((1, p, 1), g3)
    spec_pc = pl.BlockSpec((1, p, gc), g3)
    spec_cp = pl.BlockSpec((1, gc, p), g3)
    spec_jpc = pl.BlockSpec((1, S5_STEP, p, gc), g4)
    shp_jpc = jax.ShapeDtypeStruct((ng, S5_STEP, p, gc), F32)
    return pl.pallas_call(
        _s5_prep_kernel,
        out_shape=[shp_jpc, shp_jpc, shp_jpc, shp_jpc,
                   jax.ShapeDtypeStruct((ng, S5_STEP, gc, gc), F32),
                   jax.ShapeDtypeStruct((ng, p, 1), F32),
                   jax.ShapeDtypeStruct((ng, p, 1), F32)],
        grid=(ng,),
        in_specs=[spec_p1, spec_p1, pl.BlockSpec((1, 1, 1), g3), spec_pc, spec_pc,
                  spec_pc, spec_pc, spec_cp, spec_cp],
        out_specs=[spec_jpc, spec_jpc, spec_jpc, spec_jpc,
                   pl.BlockSpec((1, S5_STEP, gc, gc), g4), spec_p1, spec_p1],
        compiler_params=_cparams("parallel"),
        name="s5_prep",
    )(col(a_re), col(a_im), log_dt.reshape(ng, 1, 1), b_re, b_im,
      jnp.swapaxes(c_re, 1, 2), jnp.swapaxes(c_im, 1, 2), c_re, c_im)


def _pair_diag(a):
    g, r, c = a.shape
    a = a.reshape(g // 2, 2, r, c)
    z = jnp.zeros((g // 2, r, c), a.dtype)
    top = jnp.concatenate([a[:, 0], z], axis=2)
    bot = jnp.concatenate([z, a[:, 1]], axis=2)
    return jnp.concatenate([top, bot], axis=1)


def _s5_operators(e_re, e_im, v_re, v_im, kk, a16r, a16i):
    ng, n, p, gc = e_re.shape
    w = n * gc
    s_idx = jnp.arange(n)
    lag = s_idx[None, :] - s_idx[:, None]
    toe = jnp.take(kk, jnp.clip(lag, 0, n - 1), axis=1)
    toe = jnp.where((lag >= 0)[None, :, :, None, None], toe, 0.0)
    m = toe.transpose(0, 1, 4, 2, 3).reshape(ng, w, w)
    to_w = lambda e: e[:, ::-1].transpose(0, 1, 3, 2).reshape(ng, w, p)
    to_v = lambda v: v.transpose(0, 2, 1, 3).reshape(ng, p, w)
    pair_row = lambda a: a.reshape(ng // 2, 1, 2 * p)
    return (_pair_diag(m).astype(BF16), _pair_diag(to_w(e_re)).astype(BF16),
            _pair_diag(to_w(e_im)).astype(BF16), _pair_diag(to_v(v_re)).astype(BF16),
            _pair_diag(-to_v(v_im)).astype(BF16), pair_row(a16r), pair_row(a16i))


def _s5_core_kernel(u_ref, m_ref, wr_ref, wi_ref, vr_ref, vi_ref, ar_ref, ai_ref, d_ref, o_ref,
                    sr_ref, si_ref, xr_ref, xi_ref, *, nb, nk):
    u = u_ref[...]
    sr_ref[...] = _dot(u, wr_ref[0])
    si_ref[...] = _dot(u, wi_ref[0])
    a_r = jnp.broadcast_to(ar_ref[0], (nb, 2 * S5_STATE))
    a_i = jnp.broadcast_to(ai_ref[0], (nb, 2 * S5_STATE))

    def step(k, carry):
        x_r, x_i = carry
        rows = pl.ds(pl.multiple_of(k * nb, nb), nb)
        xr_ref[rows, :] = x_r
        xi_ref[rows, :] = x_i
        n_r = a_r * x_r - a_i * x_i + sr_ref[rows, :]
        n_i = a_r * x_i + a_i * x_r + si_ref[rows, :]
        return n_r, n_i

    zero = jnp.zeros((nb, 2 * S5_STATE), F32)
    lax.fori_loop(0, nk, step, (zero, zero), unroll=8)
    y = (_dot(u, m_ref[0]) + _dot(xr_ref[...].astype(BF16), vr_ref[0])
         + _dot(xi_ref[...].astype(BF16), vi_ref[0]))
    y = y + d_ref[...] * u.astype(F32)
    o_ref[...] = jax.nn.gelu(y).astype(BF16)


def _s5_core(u_lay, ops, d_lay, nb, nk):
    m, wr, wi, vr, vi, ar, ai = ops
    npair, w2, _ = m.shape
    p2 = wr.shape[2]
    r = nb * nk
    g3 = lambda g: (g, 0, 0)
    return pl.pallas_call(
        functools.partial(_s5_core_kernel, nb=nb, nk=nk),
        out_shape=jax.ShapeDtypeStruct(u_lay.shape, BF16),
        grid=(npair,),
        in_specs=[pl.BlockSpec((r, w2), lambda g: (0, g)),
                  pl.BlockSpec((1, w2, w2), g3),
                  pl.BlockSpec((1, w2, p2), g3), pl.BlockSpec((1, w2, p2), g3),
                  pl.BlockSpec((1, p2, w2), g3), pl.BlockSpec((1, p2, w2), g3),
                  pl.BlockSpec((1, 1, p2), g3), pl.BlockSpec((1, 1, p2), g3),
                  pl.BlockSpec((1, w2), lambda g: (0, g))],
        out_specs=pl.BlockSpec((r, w2), lambda g: (0, g)),
        scratch_shapes=[pltpu.VMEM((r, p2), F32)] * 4,
        compiler_params=_cparams("parallel"),
        name="s5_core",
    )(u_lay, m, wr, wi, vr, vi, ar, ai, d_lay)


def _attn_kernel(q_ref, k_ref, v_ref, lq1_ref, lk1_ref, lq2_ref, lk2_ref, g_ref, o_ref,
                 m_ref, l_ref, acc_ref, *, bq, lambda_init):
    qi = pl.program_id(2)
    q = q_ref[0]
    lane = lax.broadcasted_iota(jnp.int32, (1, 2 * DA_HEAD_DIM), 1)
    zero = jnp.zeros_like(q)
    qs = jnp.concatenate([jnp.where(lane < DA_HEAD_DIM, q, zero),
                          jnp.where(lane >= DA_HEAD_DIM, q, zero)], axis=0)
    m_ref[...] = jnp.full(m_ref.shape, -jnp.inf, F32)
    l_ref[...] = jnp.zeros(l_ref.shape, F32)
    acc_ref[...] = jnp.zeros(acc_ref.shape, F32)

    def block(j, mask):
        ks = pl.ds(pl.multiple_of(j * bq, bq), bq)
        s = lax.dot_general(qs, k_ref[0, ks, :], (((1,), (1,)), ((), ())),
                            preferred_element_type=F32)
        if mask is not None:
            s = jnp.where(mask, s, -jnp.inf)
        m_old = m_ref[...]
        m_new = jnp.maximum(m_old, jnp.max(s, axis=-1, keepdims=True))
        alpha = jnp.exp(m_old - m_new)
        p = jnp.exp(s - m_new)
        l_ref[...] = alpha * l_ref[...] + jnp.sum(p, axis=-1, keepdims=True)
        acc_ref[...] = alpha * acc_ref[...] + _dot(p.astype(BF16), v_ref[0, ks, :])
        m_ref[...] = m_new

    def full_block(j, carry):
        block(j, None)
        return carry

    lax.fori_loop(0, qi, full_block, 0)
    r_chunk = (lax.broadcasted_iota(jnp.int32, (2 * bq, 1), 0) % bq) // CHUNK
    c_chunk = lax.broadcasted_iota(jnp.int32, (1, bq), 1) // CHUNK
    block(qi, c_chunk <= r_chunk)

    lam = (jnp.exp(jnp.sum(lq1_ref[...] * lk1_ref[...], axis=-1, keepdims=True))
           - jnp.exp(jnp.sum(lq2_ref[...] * lk2_ref[...], axis=-1, keepdims=True)) + lambda_init)
    o_all = acc_ref[...] / l_ref[...]
    o = o_all[:bq] - lam * o_all[bq:]
    o_ref[0] = (_rms(o, g_ref[...]) * (1.0 - lambda_init)).astype(BF16)


def _diff_attn(q3, k3, v3, lq1, lk1, lq2, lk2, g_subln, lambda_init, bq):
    nb, seq, _ = q3.shape
    hw = 2 * DA_HEAD_DIM
    lam_spec = pl.BlockSpec((1, DA_HEAD_DIM), lambda b, h, i: (0, 0))
    kv_spec = pl.BlockSpec((1, seq, hw), lambda b, h, i: (b, 0, h))
    return pl.pallas_call(
        functools.partial(_attn_kernel, bq=bq, lambda_init=lambda_init),
        out_shape=jax.ShapeDtypeStruct(q3.shape, BF16),
        grid=(nb, DA_HEADS, seq // bq),
        in_specs=[pl.BlockSpec((1, bq, hw), lambda b, h, i: (b, i, h)), kv_spec, kv_spec,
                  lam_spec, lam_spec, lam_spec, lam_spec,
                  pl.BlockSpec((1, hw), lambda b, h, i: (0, 0))],
        out_specs=pl.BlockSpec((1, bq, hw), lambda b, h, i: (b, i, h)),
        scratch_shapes=[pltpu.VMEM((2 * bq, 1), F32), pltpu.VMEM((2 * bq, 1), F32),
                        pltpu.VMEM((2 * bq, hw), F32)],
        compiler_params=_cparams("parallel", "parallel", "parallel"),
        name="diff_attn",
    )(q3, k3, v3, lq1, lk1, lq2, lk2, g_subln)


def _post_kernel(ys_ref, oa_ref, gt_ref, x_ref, mod_ref, wglu_ref, bglu_ref, wbs_ref, wba_ref,
                 wout_ref, g_ref, wr_ref, br_ref, h_ref, u2_ref, cw_ref):
    d = x_ref.shape[1]
    mod = mod_ref[0]
    ys = ys_ref[...]
    yg = ys.astype(F32) * jax.nn.sigmoid(_dot(ys, wglu_ref[...]) + bglu_ref[...])
    y_ssm = _dot(yg.astype(BF16), wbs_ref[...])
    y_att = _dot(oa_ref[...], wba_ref[...])
    gates = gt_ref[...].astype(F32)
    mix_in = gates[:, :d] * y_ssm + gates[:, d:] * y_att
    mix = _dot(mix_in.astype(BF16), wout_ref[...])
    h = x_ref[...] + mod[2:3, :] * mix
    h_ref[...] = h
    u2 = _rms(h, g_ref[...]) * (1.0 + mod[4:5, :]) + mod[3:4, :]
    u2_ref[...] = u2.astype(BF16)

    logits = jnp.dot(u2, wr_ref[...], preferred_element_type=F32, precision=HIGHEST) + br_ref[...]
    lane = lax.broadcasted_iota(jnp.int32, logits.shape, 1)
    neg = jnp.full_like(logits, -jnp.inf)
    big = jnp.full_like(lane, LANES)
    is_grp = (lane >= N_EXPERTS) & (lane < N_EXPERTS + N_GROUPS)
    glog = jnp.where(is_grp, logits, neg)
    gmax = jnp.max(glog, axis=-1, keepdims=True)
    gp = 1.0 / jnp.sum(jnp.exp(glog - gmax), axis=-1, keepdims=True)
    gi = jnp.min(jnp.where(glog == gmax, lane, big), axis=-1, keepdims=True) - N_EXPERTS
    in_grp = (lane < N_EXPERTS) & ((lane // EXP_PER_GROUP) == gi)
    elog = jnp.where(in_grp, logits, neg)
    v1 = jnp.max(elog, axis=-1, keepdims=True)
    i1 = jnp.min(jnp.where(elog == v1, lane, big), axis=-1, keepdims=True)
    elog2 = jnp.where(lane == i1, neg, elog)
    v2 = jnp.max(elog2, axis=-1, keepdims=True)
    i2 = jnp.min(jnp.where(elog2 == v2, lane, big), axis=-1, keepdims=True)
    e2 = jnp.exp(v2 - v1)
    w1 = gp / (1.0 + e2)
    w2 = gp * e2 / (1.0 + e2)
    cw_ref[...] = jnp.where(lane == i1, w1, 0.0) + jnp.where(lane == i2, w2, 0.0)


def _post_mix(ys, oa, gates, x2, mod3, wglu, bglu, wbs, wba, wout, g, wr, br, seq, tm):
    t, d = x2.shape
    tpb = seq // tm
    row = lambda i: (i, 0)
    const = lambda i: (0, 0)
    full = lambda a: pl.BlockSpec(a.shape, const)
    return pl.pallas_call(
        _post_kernel,
        out_shape=[jax.ShapeDtypeStruct((t, d), F32), jax.ShapeDtypeStruct((t, d), BF16),
                   jax.ShapeDtypeStruct((t, LANES), F32)],
        grid=(t // tm,),
        in_specs=[pl.BlockSpec((tm, ys.shape[1]), row), pl.BlockSpec((tm, d), row),
                  pl.BlockSpec((tm, 2 * d), row), pl.BlockSpec((tm, d), row),
                  pl.BlockSpec((1, 6, d), lambda i: (i // tpb, 0, 0)),
                  full(wglu), full(bglu), full(wbs), full(wba), full(wout), full(g), full(wr), full(br)],
        out_specs=[pl.BlockSpec((tm, d), row), pl.BlockSpec((tm, d), row), pl.BlockSpec((tm, LANES), row)],
        compiler_params=_cparams("parallel"),
        name="post_mix",
    )(ys, oa, gates, x2, mod3, wglu, bglu, wbs, wba, wout, g, wr, br)


def _moe_kernel(u_ref, cw_ref, wgu_ref, wd_ref, h_ref, mod_ref, g_ref, o_ref, acc_ref, *, final_norm):
    e = pl.program_id(1)

    @pl.when(e == 0)
    def _():
        acc_ref[...] = jnp.zeros(acc_ref.shape, F32)

    gu = _dot(u_ref[...], wgu_ref[0])
    gate = gu[:, :D_EXPERT]
    hdn = gate * jax.nn.sigmoid(gate) * gu[:, D_EXPERT:]
    cw_all = cw_ref[...]
    lane = lax.broadcasted_iota(jnp.int32, cw_all.shape, 1)
    cw = jnp.sum(jnp.where(lane == e, cw_all, 0.0), axis=-1, keepdims=True)
    acc_ref[...] += _dot((hdn * cw).astype(BF16), wd_ref[0])

    @pl.when(e == pl.num_programs(1) - 1)
    def _():
        h = h_ref[...] + mod_ref[0][5:6, :] * acc_ref[...]
        o_ref[...] = _rms(h, g_ref[...]) if final_norm else h


def _moe_final(u2, cw, wgu, wd, h1, mod3, g_final, seq, tm, final_norm):
    t, d = h1.shape
    tpb = seq // tm
    row = lambda i, e: (i, 0)
    return pl.pallas_call(
        functools.partial(_moe_kernel, final_norm=final_norm),
        out_shape=jax.ShapeDtypeStruct((t, d), F32),
        grid=(t // tm, N_EXPERTS),
        in_specs=[pl.BlockSpec((tm, d), row), pl.BlockSpec((tm, LANES), row),
                  pl.BlockSpec((1, d, 2 * D_EXPERT), lambda i, e: (e, 0, 0)),
                  pl.BlockSpec((1, D_EXPERT, d), lambda i, e: (e, 0, 0)),
                  pl.BlockSpec((tm, d), row),
                  pl.BlockSpec((1, 6, d), lambda i, e: (i // tpb, 0, 0)),
                  pl.BlockSpec((1, d), lambda i, e: (0, 0))],
        out_specs=pl.BlockSpec((tm, d), row),
        scratch_shapes=[pltpu.VMEM((tm, d), F32)],
        compiler_params=_cparams("parallel", "arbitrary"),
        name="moe_final",
    )(u2, cw, wgu, wd, h1, mod3, g_final)


def kernel(x, c, w_ada, b_ada, g_norm_mix, w_in, b_in, s5_a_re, s5_a_im, s5_b_re, s5_b_im, s5_c_re, s5_c_im, s5_d, s5_log_dt, w_glu, b_glu, lambda_q1, lambda_k1, lambda_q2, lambda_k2, g_subln, w_br_ssm, w_br_attn, w_out, g_norm_ffn, w_router_grp, b_router_grp, w_router_exp, b_router_exp, w_exp_gate, w_exp_up, w_exp_down, g_final):
    nb, seq, d = x.shape
    depth = w_ada.shape[0]
    s5w = s5_d.shape[1]
    daw = w_br_attn.shape[1]
    ng = s5w // S5_GROUP
    nk = seq // S5_STEP
    assert nb % 8 == 0 and seq % 512 == 0 and ng % 2 == 0
    widths = (s5w, daw, daw, daw, 2 * d)
    tm = 512
    bq = 256
    row = lambda a: a.reshape(1, -1)

    h = x.reshape(nb * seq, d)
    for l in range(depth):
        lambda_init = 0.8 - 0.6 * math.exp(-0.3 * l)
        mod3 = _ada_mod(c, w_ada[l], row(b_ada[l])).reshape(nb, 6, d)

        s5_in, q, k, v, gates = _in_proj(h, mod3, row(g_norm_mix[l]), w_in[l].astype(BF16), row(b_in[l]),
                                         seq, widths, tm)

        prep = _s5_prep(s5_a_re[l], s5_a_im[l], s5_log_dt[l], s5_b_re[l], s5_b_im[l], s5_c_re[l], s5_c_im[l])
        ops = _s5_operators(*prep)
        u_lay = (s5_in.reshape(nb, nk, S5_STEP, ng, S5_GROUP).transpose(1, 0, 3, 2, 4)
                 .reshape(nk * nb, ng * S5_STEP * S5_GROUP))
        d_lay = jnp.broadcast_to(s5_d[l].reshape(ng, 1, S5_GROUP), (ng, S5_STEP, S5_GROUP)).reshape(1, -1)
        y_lay = _s5_core(u_lay, ops, d_lay, nb, nk)
        ys = (y_lay.reshape(nk, nb, ng, S5_STEP, S5_GROUP).transpose(1, 0, 3, 2, 4)
              .reshape(nb * seq, s5w))

        as3 = lambda a: a.reshape(nb, seq, daw)
        oa = _diff_attn(as3(q), as3(k), as3(v), row(lambda_q1[l]), row(lambda_k1[l]), row(lambda_q2[l]),
                        row(lambda_k2[l]), row(g_subln[l]), lambda_init, bq).reshape(nb * seq, daw)

        w_router = jnp.zeros((d, LANES), F32)
        w_router = w_router.at[:, :N_EXPERTS].set(w_router_exp[l])
        w_router = w_router.at[:, N_EXPERTS:N_EXPERTS + N_GROUPS].set(w_router_grp[l])
        b_router = jnp.zeros((1, LANES), F32)
        b_router = b_router.at[0, :N_EXPERTS].set(b_router_exp[l])
        b_router = b_router.at[0, N_EXPERTS:N_EXPERTS + N_GROUPS].set(b_router_grp[l])
        h1, u2, cw = _post_mix(ys, oa, gates, h, mod3, w_glu[l].astype(BF16), row(b_glu[l]),
                               w_br_ssm[l].astype(BF16), w_br_attn[l].astype(BF16), w_out[l].astype(BF16),
                               row(g_norm_ffn[l]), w_router, b_router, seq, tm)

        wgu = jnp.concatenate([w_exp_gate[l], w_exp_up[l]], axis=-1).astype(BF16)
        h = _moe_final(u2, cw, wgu, w_exp_down[l].astype(BF16), h1, mod3, row(g_final), seq, 1024,
                       final_norm=(l == depth - 1))
    return h.reshape(nb, seq, d)
```

```python
import functools
import math

import jax
import jax.numpy as jnp
from jax import lax
from jax.experimental import pallas as pl
from jax.experimental.pallas import tpu as pltpu

EPS = 1e-6
CHUNK = 64
S5_GROUP = 16
S5_STATE = 64
S5_STEP = 16
DA_HEADS = 8
DA_HEAD_DIM = 64
N_GROUPS = 4
EXP_PER_GROUP = 8
N_EXPERTS = N_GROUPS * EXP_PER_GROUP
D_EXPERT = 256
LANES = 128
Q_SCALE = DA_HEAD_DIM ** -0.5 * math.log2(math.e)
VMEM_LIMIT = 56 * 1024 * 1024

F32 = jnp.float32
BF16 = jnp.bfloat16
HIGHEST = lax.Precision.HIGHEST


def _cparams(*sem):
    return pltpu.CompilerParams(dimension_semantics=sem, vmem_limit_bytes=VMEM_LIMIT)


def _dot(a, b):
    return jnp.dot(a, b, preferred_element_type=F32)


def _rms(x, g):
    return x * lax.rsqrt(jnp.mean(x * x, axis=-1, keepdims=True) + EPS) * g


def _ada_kernel(c_ref, w_ref, b_ref, o_ref):
    c = c_ref[...]
    cs = c * jax.nn.sigmoid(c)
    o_ref[...] = jnp.dot(cs, w_ref[...], preferred_element_type=F32, precision=HIGHEST) + b_ref[...]


def _ada_mod(c, w, b):
    nb, d = c.shape
    n = w.shape[1]
    tn = n // 4
    return pl.pallas_call(
        _ada_kernel,
        out_shape=jax.ShapeDtypeStruct((nb, n), F32),
        grid=(n // tn,),
        in_specs=[pl.BlockSpec((nb, d), lambda j: (0, 0)),
                  pl.BlockSpec((d, tn), lambda j: (0, j)),
                  pl.BlockSpec((1, tn), lambda j: (0, j))],
        out_specs=pl.BlockSpec((nb, tn), lambda j: (0, j)),
        compiler_params=_cparams("parallel"),
        name="ada_mod",
    )(c, w, b)


def _inproj_kernel(x_ref, mod_ref, g_ref, w_ref, b_ref, s5_ref, q_ref, k_ref, v_ref, gt_ref, *, widths):
    x = x_ref[...]
    mod = mod_ref[0]
    u = _rms(x, g_ref[...]) * (1.0 + mod[1:2, :]) + mod[0:1, :]
    ub = u.astype(BF16)
    s5w, qw, kw, vw, gw = widths
    o = 0
    s5_ref[...] = (_dot(ub, w_ref[:, o:o + s5w]) + b_ref[:, o:o + s5w]).astype(BF16)
    o += s5w
    q_ref[...] = ((_dot(ub, w_ref[:, o:o + qw]) + b_ref[:, o:o + qw]) * Q_SCALE).astype(BF16)
    o += qw
    k_ref[...] = (_dot(ub, w_ref[:, o:o + kw]) + b_ref[:, o:o + kw]).astype(BF16)
    o += kw
    v_ref[...] = (_dot(ub, w_ref[:, o:o + vw]) + b_ref[:, o:o + vw]).astype(BF16)
    o += vw
    gt_ref[...] = jax.nn.sigmoid(_dot(ub, w_ref[:, o:o + gw]) + b_ref[:, o:o + gw]).astype(BF16)


def _in_proj(x2, mod3, g, w, b, seq, widths, tm):
    t, d = x2.shape
    n = w.shape[1]
    tpb = seq // tm
    row = lambda i: (i, 0)
    const = lambda i: (0, 0)
    outs = [jax.ShapeDtypeStruct((t, wd), BF16) for wd in widths]
    return pl.pallas_call(
        functools.partial(_inproj_kernel, widths=widths),
        out_shape=outs,
        grid=(t // tm,),
        in_specs=[pl.BlockSpec((tm, d), row),
                  pl.BlockSpec((1, 6, d), lambda i: (i // tpb, 0, 0)),
                  pl.BlockSpec((1, d), const),
                  pl.BlockSpec((d, n), const),
                  pl.BlockSpec((1, n), const)],
        out_specs=[pl.BlockSpec((tm, wd), row) for wd in widths],
        compiler_params=_cparams("parallel"),
        name="in_proj",
    )(x2, mod3, g, w, b)


def _s5_prep_kernel(ar_ref, ai_ref, dt_ref, br_ref, bi_ref, crt_ref, cit_ref, cr_ref, ci_ref,
                    er_ref, ei_ref, vr_ref, vi_ref, k_ref, a16r_ref, a16i_ref):
    lr = ar_ref[0]
    li = ai_ref[0]
    dt = jnp.exp(dt_ref[0])
    mag = jnp.exp(lr * dt)
    abr = mag * jnp.cos(li * dt)
    abi = mag * jnp.sin(li * dt)
    den = lr * lr + li * li
    nr = abr - 1.0
    f_re = (nr * lr + abi * li) / den
    f_im = (abi * lr - nr * li) / den
    b_re = br_ref[0]
    b_im = bi_ref[0]
    bb_re = f_re * b_re - f_im * b_im
    bb_im = f_re * b_im + f_im * b_re
    crt = crt_ref[0]
    cit = cit_ref[0]
    cr = cr_ref[0]
    ci = ci_ref[0]
    pr = jnp.ones_like(lr)
    pi = jnp.zeros_like(lr)
    for j in range(S5_STEP):
        e_re = pr * bb_re - pi * bb_im
        e_im = pr * bb_im + pi * bb_re
        er_ref[0, j] = e_re
        ei_ref[0, j] = e_im
        k_ref[0, j] = (jnp.dot(cr, e_re, preferred_element_type=F32, precision=HIGHEST)
                       - jnp.dot(ci, e_im, preferred_element_type=F32, precision=HIGHEST))
        pr, pi = pr * abr - pi * abi, pr * abi + pi * abr
        vr_ref[0, j] = pr * crt - pi * cit
        vi_ref[0, j] = pr * cit + pi * crt
    a16r_ref[0] = pr
    a16i_ref[0] = pi


def _s5_prep(a_re, a_im, log_dt, b_re, b_im, c_re, c_im):
    ng, p = a_re.shape
    gc = b_re.shape[-1]
    col = lambda a: a.reshape(ng, p, 1)
    g3 = lambda g: (g, 0, 0)
    g4 = lambda g: (g, 0, 0, 0)
    spec_p1 = pl.BlockSpec((1, p, 1), g3)
    spec_pc = pl.BlockSpec((1, p, gc), g3)
    spec_cp = pl.BlockSpec((1, gc, p), g3)
    spec_jpc = pl.BlockSpec((1, S5_STEP, p, gc), g4)
    shp_jpc = jax.ShapeDtypeStruct((ng, S5_STEP, p, gc), F32)
    return pl.pallas_call(
        _s5_prep_kernel,
        out_shape=[shp_jpc, shp_jpc, shp_jpc, shp_jpc,
                   jax.ShapeDtypeStruct((ng, S5_STEP, gc, gc), F32),
                   jax.ShapeDtypeStruct((ng, p, 1), F32),
                   jax.ShapeDtypeStruct((ng, p, 1), F32)],
        grid=(ng,),
        in_specs=[spec_p1, spec_p1, pl.BlockSpec((1, 1, 1), g3), spec_pc, spec_pc,
                  spec_pc, spec_pc, spec_cp, spec_cp],
        out_specs=[spec_jpc, spec_jpc, spec_jpc, spec_jpc,
                   pl.BlockSpec((1, S5_STEP, gc, gc), g4), spec_p1, spec_p1],
        compiler_params=_cparams("parallel"),
        name="s5_prep",
    )(col(a_re), col(a_im), log_dt.reshape(ng, 1, 1), b_re, b_im,
      jnp.swapaxes(c_re, 1, 2), jnp.swapaxes(c_im, 1, 2), c_re, c_im)


def _pair_diag(a):
    g, r, c = a.shape
    a = a.reshape(g // 2, 2, r, c)
    z = jnp.zeros((g // 2, r, c), a.dtype)
    top = jnp.concatenate([a[:, 0], z], axis=2)
    bot = jnp.concatenate([z, a[:, 1]], axis=2)
    return jnp.concatenate([top, bot], axis=1)


def _s5_operators(e_re, e_im, v_re, v_im, kk, a16r, a16i):
    ng, n, p, gc = e_re.shape
    w = n * gc
    s_idx = jnp.arange(n)
    lag = s_idx[None, :] - s_idx[:, None]
    toe = jnp.take(kk, jnp.clip(lag, 0, n - 1), axis=1)
    toe = jnp.where((lag >= 0)[None, :, :, None, None], toe, 0.0)
    m = toe.transpose(0, 1, 4, 2, 3).reshape(ng, w, w)
    to_w = lambda e: e[:, ::-1].transpose(0, 1, 3, 2).reshape(ng, w, p)
    to_v = lambda v: v.transpose(0, 2, 1, 3).reshape(ng, p, w)
    pair_row = lambda a: a.reshape(ng // 2, 1, 2 * p)
    return (_pair_diag(m).astype(BF16), _pair_diag(to_w(e_re)).astype(BF16),
            _pair_diag(to_w(e_im)).astype(BF16), _pair_diag(to_v(v_re)).astype(BF16),
            _pair_diag(-to_v(v_im)).astype(BF16), pair_row(a16r), pair_row(a16i))


def _s5_core_kernel(u_ref, m_ref, wr_ref, wi_ref, vr_ref, vi_ref, ar_ref, ai_ref, d_ref, o_ref,
                    sr_ref, si_ref, xr_ref, xi_ref, *, nb, nk):
    u = u_ref[...]
    sr_ref[...] = _dot(u, wr_ref[0])
    si_ref[...] = _dot(u, wi_ref[0])
    a_r = jnp.broadcast_to(ar_ref[0], (nb, 2 * S5_STATE))
    a_i = jnp.broadcast_to(ai_ref[0], (nb, 2 * S5_STATE))

    def step(k, carry):
        x_r, x_i = carry
        rows = pl.ds(pl.multiple_of(k * nb, nb), nb)
        xr_ref[rows, :] = x_r
        xi_ref[rows, :] = x_i
        n_r = a_r * x_r - a_i * x_i + sr_ref[rows, :]
        n_i = a_r * x_i + a_i * x_r + si_ref[rows, :]
        return n_r, n_i

    zero = jnp.zeros((nb, 2 * S5_STATE), F32)
    lax.fori_loop(0, nk, step, (zero, zero), unroll=8)
    y = (_dot(u, m_ref[0]) + _dot(xr_ref[...].astype(BF16), vr_ref[0])
         + _dot(xi_ref[...].astype(BF16), vi_ref[0]))
    y = y + d_ref[...] * u.astype(F32)
    o_ref[...] = jax.nn.gelu(y).astype(BF16)


def _s5_core(u_lay, ops, d_lay, nb, nk):
    m, wr, wi, vr, vi, ar, ai = ops
    npair, w2, _ = m.shape
    p2 = wr.shape[2]
    r = nb * nk
    g3 = lambda g: (g, 0, 0)
    return pl.pallas_call(
        functools.partial(_s5_core_kernel, nb=nb, nk=nk),
        out_shape=jax.ShapeDtypeStruct(u_lay.shape, BF16),
        grid=(npair,),
        in_specs=[pl.BlockSpec((r, w2), lambda g: (0, g)),
                  pl.BlockSpec((1, w2, w2), g3),
                  pl.BlockSpec((1, w2, p2), g3), pl.BlockSpec((1, w2, p2), g3),
                  pl.BlockSpec((1, p2, w2), g3), pl.BlockSpec((1, p2, w2), g3),
                  pl.BlockSpec((1, 1, p2), g3), pl.BlockSpec((1, 1, p2), g3),
                  pl.BlockSpec((1, w2), lambda g: (0, g))],
        out_specs=pl.BlockSpec((r, w2), lambda g: (0, g)),
        scratch_shapes=[pltpu.VMEM((r, p2), F32)] * 4,
        compiler_params=_cparams("parallel"),
        name="s5_core",
    )(u_lay, m, wr, wi, vr, vi, ar, ai, d_lay)


def _attn_kernel(q_ref, k_ref, v_ref, lq1_ref, lk1_ref, lq2_ref, lk2_ref, g_ref, o_ref,
                 m_ref, l_ref, acc_ref, *, bq, lambda_init):
    qi = pl.program_id(2)
    q = q_ref[0]
    lane = lax.broadcasted_iota(jnp.int32, (1, 2 * DA_HEAD_DIM), 1)
    zero = jnp.zeros_like(q)
    qs = jnp.concatenate([jnp.where(lane < DA_HEAD_DIM, q, zero),
                          jnp.where(lane >= DA_HEAD_DIM, q, zero)], axis=0)
    m_ref[...] = jnp.full(m_ref.shape, -jnp.inf, F32)
    l_ref[...] = jnp.zeros(l_ref.shape, F32)
    acc_ref[...] = jnp.zeros(acc_ref.shape, F32)
    reps = bq // LANES

    def block(j, mask):
        ks = pl.ds(pl.multiple_of(j * bq, bq), bq)
        s = lax.dot_general(qs, k_ref[0, ks, :], (((1,), (1,)), ((), ())),
                            preferred_element_type=F32)
        if mask is not None:
            s = jnp.where(mask, s, -jnp.inf)
        m_old = m_ref[...]
        m_new = jnp.maximum(m_old, jnp.max(s, axis=-1, keepdims=True))
        alpha = jnp.exp2(m_old - m_new)
        p = jnp.exp2(s - pltpu.repeat(m_new, reps, 1))
        l_ref[...] = alpha * l_ref[...] + jnp.sum(p, axis=-1, keepdims=True)
        acc_ref[...] = alpha * acc_ref[...] + _dot(p.astype(BF16), v_ref[0, ks, :])
        m_ref[...] = m_new

    def full_block(j, carry):
        block(j, None)
        return carry

    lax.fori_loop(0, qi, full_block, 0)
    r_chunk = (lax.broadcasted_iota(jnp.int32, (2 * bq, bq), 0) % bq) // CHUNK
    c_chunk = lax.broadcasted_iota(jnp.int32, (2 * bq, bq), 1) // CHUNK
    block(qi, c_chunk <= r_chunk)

    lam = (jnp.exp(jnp.sum(lq1_ref[...] * lk1_ref[...], axis=-1, keepdims=True))
           - jnp.exp(jnp.sum(lq2_ref[...] * lk2_ref[...], axis=-1, keepdims=True)) + lambda_init)
    o_all = acc_ref[...] / l_ref[...]
    o = o_all[:bq] - lam * o_all[bq:]
    o_ref[0] = (_rms(o, g_ref[...]) * (1.0 - lambda_init)).astype(BF16)


def _diff_attn(q3, k3, v3, lq1, lk1, lq2, lk2, g_subln, lambda_init, bq):
    nb, seq, _ = q3.shape
    hw = 2 * DA_HEAD_DIM
    lam_spec = pl.BlockSpec((1, DA_HEAD_DIM), lambda b, h, i: (0, 0))
    kv_spec = pl.BlockSpec((1, seq, hw), lambda b, h, i: (b, 0, h))
    return pl.pallas_call(
        functools.partial(_attn_kernel, bq=bq, lambda_init=lambda_init),
        out_shape=jax.ShapeDtypeStruct(q3.shape, BF16),
        grid=(nb, DA_HEADS, seq // bq),
        in_specs=[pl.BlockSpec((1, bq, hw), lambda b, h, i: (b, i, h)), kv_spec, kv_spec,
                  lam_spec, lam_spec, lam_spec, lam_spec,
                  pl.BlockSpec((1, hw), lambda b, h, i: (0, 0))],
        out_specs=pl.BlockSpec((1, bq, hw), lambda b, h, i: (b, i, h)),
        scratch_shapes=[pltpu.VMEM((2 * bq, hw), F32)] * 3,
        compiler_params=_cparams("parallel", "parallel", "parallel"),
        name="diff_attn",
    )(q3, k3, v3, lq1, lk1, lq2, lk2, g_subln)


def _post_kernel(ys_ref, oa_ref, gt_ref, x_ref, mod_ref, wglu_ref, bglu_ref, wbs_ref, wba_ref,
                 wout_ref, g_ref, wr_ref, br_ref, h_ref, u2_ref, cw_ref):
    d = x_ref.shape[1]
    mod = mod_ref[0]
    ys = ys_ref[...]
    yg = ys.astype(F32) * jax.nn.sigmoid(_dot(ys, wglu_ref[...]) + bglu_ref[...])
    y_ssm = _dot(yg.astype(BF16), wbs_ref[...])
    y_att = _dot(oa_ref[...], wba_ref[...])
    gates = gt_ref[...].astype(F32)
    mix_in = gates[:, :d] * y_ssm + gates[:, d:] * y_att
    mix = _dot(mix_in.astype(BF16), wout_ref[...])
    h = x_ref[...] + mod[2:3, :] * mix
    h_ref[...] = h
    u2 = _rms(h, g_ref[...]) * (1.0 + mod[4:5, :]) + mod[3:4, :]
    u2_ref[...] = u2.astype(BF16)

    logits = jnp.dot(u2, wr_ref[...], preferred_element_type=F32, precision=HIGHEST) + br_ref[...]
    lane = lax.broadcasted_iota(jnp.int32, logits.shape, 1)
    neg = jnp.full_like(logits, -jnp.inf)
    big = jnp.full_like(lane, LANES)
    is_grp = (lane >= N_EXPERTS) & (lane < N_EXPERTS + N_GROUPS)
    glog = jnp.where(is_grp, logits, neg)
    gmax = jnp.max(glog, axis=-1, keepdims=True)
    gp = 1.0 / jnp.sum(jnp.exp(glog - gmax), axis=-1, keepdims=True)
    gi = jnp.min(jnp.where(glog == gmax, lane, big), axis=-1, keepdims=True) - N_EXPERTS
    in_grp = (lane < N_EXPERTS) & ((lane // EXP_PER_GROUP) == gi)
    elog = jnp.where(in_grp, logits, neg)
    v1 = jnp.max(elog, axis=-1, keepdims=True)
    i1 = jnp.min(jnp.where(elog == v1, lane, big), axis=-1, keepdims=True)
    elog2 = jnp.where(lane == i1, neg, elog)
    v2 = jnp.max(elog2, axis=-1, keepdims=True)
    i2 = jnp.min(jnp.where(elog2 == v2, lane, big), axis=-1, keepdims=True)
    e2 = jnp.exp(v2 - v1)
    w1 = gp / (1.0 + e2)
    w2 = gp * e2 / (1.0 + e2)
    cw_ref[...] = jnp.where(lane == i1, w1, 0.0) + jnp.where(lane == i2, w2, 0.0)


def _post_mix(ys, oa, gates, x2, mod3, wglu, bglu, wbs, wba, wout, g, wr, br, seq, tm):
    t, d = x2.shape
    tpb = seq // tm
    row = lambda i: (i, 0)
    const = lambda i: (0, 0)
    full = lambda a: pl.BlockSpec(a.shape, const)
    return pl.pallas_call(
        _post_kernel,
        out_shape=[jax.ShapeDtypeStruct((t, d), F32), jax.ShapeDtypeStruct((t, d), BF16),
                   jax.ShapeDtypeStruct((t, LANES), F32)],
        grid=(t // tm,),
        in_specs=[pl.BlockSpec((tm, ys.shape[1]), row), pl.BlockSpec((tm, d), row),
                  pl.BlockSpec((tm, 2 * d), row), pl.BlockSpec((tm, d), row),
                  pl.BlockSpec((1, 6, d), lambda i: (i // tpb, 0, 0)),
                  full(wglu), full(bglu), full(wbs), full(wba), full(wout), full(g), full(wr), full(br)],
        out_specs=[pl.BlockSpec((tm, d), row), pl.BlockSpec((tm, d), row), pl.BlockSpec((tm, LANES), row)],
        compiler_params=_cparams("parallel"),
        name="post_mix",
    )(ys, oa, gates, x2, mod3, wglu, bglu, wbs, wba, wout, g, wr, br)


def _moe_kernel(u_ref, cw_ref, wgu_ref, wd_ref, h_ref, mod_ref, g_ref, o_ref, acc_ref, *, final_norm):
    e = pl.program_id(1)

    @pl.when(e == 0)
    def _():
        acc_ref[...] = jnp.zeros(acc_ref.shape, F32)

    gu = _dot(u_ref[...], wgu_ref[0])
    gate = gu[:, :D_EXPERT]
    hdn = gate * jax.nn.sigmoid(gate) * gu[:, D_EXPERT:]
    cw_all = cw_ref[...]
    lane = lax.broadcasted_iota(jnp.int32, cw_all.shape, 1)
    cw = jnp.sum(jnp.where(lane == e, cw_all, 0.0), axis=-1, keepdims=True)
    acc_ref[...] += _dot((hdn * cw).astype(BF16), wd_ref[0])

    @pl.when(e == pl.num_programs(1) - 1)
    def _():
        h = h_ref[...] + mod_ref[0][5:6, :] * acc_ref[...]
        o_ref[...] = _rms(h, g_ref[...]) if final_norm else h


def _moe_final(u2, cw, wgu, wd, h1, mod3, g_final, seq, tm, final_norm):
    t, d = h1.shape
    tpb = seq // tm
    row = lambda i, e: (i, 0)
    return pl.pallas_call(
        functools.partial(_moe_kernel, final_norm=final_norm),
        out_shape=jax.ShapeDtypeStruct((t, d), F32),
        grid=(t // tm, N_EXPERTS),
        in_specs=[pl.BlockSpec((tm, d), row), pl.BlockSpec((tm, LANES), row),
                  pl.BlockSpec((1, d, 2 * D_EXPERT), lambda i, e: (e, 0, 0)),
                  pl.BlockSpec((1, D_EXPERT, d), lambda i, e: (e, 0, 0)),
                  pl.BlockSpec((tm, d), row),
                  pl.BlockSpec((1, 6, d), lambda i, e: (i // tpb, 0, 0)),
                  pl.BlockSpec((1, d), lambda i, e: (0, 0))],
        out_specs=pl.BlockSpec((tm, d), row),
        scratch_shapes=[pltpu.VMEM((tm, d), F32)],
        compiler_params=_cparams("parallel", "arbitrary"),
        name="moe_final",
    )(u2, cw, wgu, wd, h1, mod3, g_final)


def kernel(x, c, w_ada, b_ada, g_norm_mix, w_in, b_in, s5_a_re, s5_a_im, s5_b_re, s5_b_im, s5_c_re, s5_c_im, s5_d, s5_log_dt, w_glu, b_glu, lambda_q1, lambda_k1, lambda_q2, lambda_k2, g_subln, w_br_ssm, w_br_attn, w_out, g_norm_ffn, w_router_grp, b_router_grp, w_router_exp, b_router_exp, w_exp_gate, w_exp_up, w_exp_down, g_final):
    nb, seq, d = x.shape
    depth = w_ada.shape[0]
    s5w = s5_d.shape[1]
    daw = w_br_attn.shape[1]
    ng = s5w // S5_GROUP
    nk = seq // S5_STEP
    assert nb % 8 == 0 and seq % 512 == 0 and ng % 2 == 0
    widths = (s5w, daw, daw, daw, 2 * d)
    tm = 512
    bq = 512
    row = lambda a: a.reshape(1, -1)

    h = x.reshape(nb * seq, d)
    for l in range(depth):
        lambda_init = 0.8 - 0.6 * math.exp(-0.3 * l)
        mod3 = _ada_mod(c, w_ada[l], row(b_ada[l])).reshape(nb, 6, d)

        s5_in, q, k, v, gates = _in_proj(h, mod3, row(g_norm_mix[l]), w_in[l].astype(BF16), row(b_in[l]),
                                         seq, widths, tm)

        prep = _s5_prep(s5_a_re[l], s5_a_im[l], s5_log_dt[l], s5_b_re[l], s5_b_im[l], s5_c_re[l], s5_c_im[l])
        ops = _s5_operators(*prep)
        u_lay = (s5_in.reshape(nb, nk, S5_STEP, ng, S5_GROUP).transpose(1, 0, 3, 2, 4)
                 .reshape(nk * nb, ng * S5_STEP * S5_GROUP))
        d_lay = jnp.broadcast_to(s5_d[l].reshape(ng, 1, S5_GROUP), (ng, S5_STEP, S5_GROUP)).reshape(1, -1)
        y_lay = _s5_core(u_lay, ops, d_lay, nb, nk)
        ys = (y_lay.reshape(nk, nb, ng, S5_STEP, S5_GROUP).transpose(1, 0, 3, 2, 4)
              .reshape(nb * seq, s5w))

        as3 = lambda a: a.reshape(nb, seq, daw)
        oa = _diff_attn(as3(q), as3(k), as3(v), row(lambda_q1[l]), row(lambda_k1[l]), row(lambda_q2[l]),
                        row(lambda_k2[l]), row(g_subln[l]), lambda_init, bq).reshape(nb * seq, daw)

        w_router = jnp.zeros((d, LANES), F32)
        w_router = w_router.at[:, :N_EXPERTS].set(w_router_exp[l])
        w_router = w_router.at[:, N_EXPERTS:N_EXPERTS + N_GROUPS].set(w_router_grp[l])
        b_router = jnp.zeros((1, LANES), F32)
        b_router = b_router.at[0, :N_EXPERTS].set(b_router_exp[l])
        b_router = b_router.at[0, N_EXPERTS:N_EXPERTS + N_GROUPS].set(b_router_grp[l])
        h1, u2, cw = _post_mix(ys, oa, gates, h, mod3, w_glu[l].astype(BF16), row(b_glu[l]),
                               w_br_ssm[l].astype(BF16), w_br_attn[l].astype(BF16), w_out[l].astype(BF16),
                               row(g_norm_ffn[l]), w_router, b_router, seq, tm)

        wgu = jnp.concatenate([w_exp_gate[l], w_exp_up[l]], axis=-1).astype(BF16)
        h = _moe_final(u2, cw, wgu, w_exp_down[l].astype(BF16), h1, mod3, row(g_final), seq, 1024,
                       final_norm=(l == depth - 1))
    return h.reshape(nb, seq, d)
```

```python
import functools
import math

import jax
import jax.numpy as jnp
from jax import lax
from jax.experimental import pallas as pl
from jax.experimental.pallas import tpu as pltpu

EPS = 1e-6
CHUNK = 64
S5_GROUP = 16
S5_STATE = 64
S5_STEP = 16
DA_HEADS = 8
DA_HEAD_DIM = 64
N_GROUPS = 4
EXP_PER_GROUP = 8
N_EXPERTS = N_GROUPS * EXP_PER_GROUP
D_EXPERT = 256
LANES = 128
Q_SCALE = DA_HEAD_DIM ** -0.5 * math.log2(math.e)
VMEM_LIMIT = 56 * 1024 * 1024

F32 = jnp.float32
BF16 = jnp.bfloat16
HIGHEST = lax.Precision.HIGHEST


def _cparams(*sem):
    return pltpu.CompilerParams(dimension_semantics=sem, vmem_limit_bytes=VMEM_LIMIT)


def _dot(a, b):
    return jnp.dot(a, b, preferred_element_type=F32)


def _dot_nt(a, b):
    return lax.dot_general(a, b, (((1,), (1,)), ((), ())), preferred_element_type=F32)


def _rms(x, g):
    return x * lax.rsqrt(jnp.mean(x * x, axis=-1, keepdims=True) + EPS) * g


def _ada_kernel(c_ref, w_ref, b_ref, o_ref):
    c = c_ref[...]
    cs = c * jax.nn.sigmoid(c)
    o_ref[...] = jnp.dot(cs, w_ref[...], preferred_element_type=F32, precision=HIGHEST) + b_ref[...]


def _ada_mod(c, w, b):
    nb, d = c.shape
    n = w.shape[1]
    tn = n // 4
    return pl.pallas_call(
        _ada_kernel,
        out_shape=jax.ShapeDtypeStruct((nb, n), F32),
        grid=(n // tn,),
        in_specs=[pl.BlockSpec((nb, d), lambda j: (0, 0)),
                  pl.BlockSpec((d, tn), lambda j: (0, j)),
                  pl.BlockSpec((1, tn), lambda j: (0, j))],
        out_specs=pl.BlockSpec((nb, tn), lambda j: (0, j)),
        compiler_params=_cparams("parallel"),
        name="ada_mod",
    )(c, w, b)


def _step_rows(ref, width, sp):
    blk = ref[0]
    return jnp.concatenate([blk[:, i * width:(i + 1) * width] for i in range(sp)], axis=0)


def _put_step_rows(ref, val, sp):
    nk = val.shape[0] // sp
    ref[0] = jnp.concatenate([val[i * nk:(i + 1) * nk] for i in range(sp)], axis=1)


def _inproj_kernel(x_ref, mod_ref, g_ref, wst_ref, bs_ref, w_ref, b_ref, ut_ref, q_ref, k_ref, v_ref, gt_ref,
                   *, widths, sp):
    d = mod_ref.shape[2]
    x = _step_rows(x_ref, d, sp)
    nk = x.shape[0] // sp
    mod = mod_ref[0]
    u = _rms(x, g_ref[...]) * (1.0 + mod[1:2, :]) + mod[0:1, :]
    ub = u.astype(BF16)
    s5t = _dot_nt(wst_ref[...], ub) + bs_ref[...]
    for i in range(sp):
        ut_ref[i] = s5t[:, i * nk:(i + 1) * nk].astype(BF16)
    qw, kw, vw, gw = widths
    o = 0
    _put_step_rows(q_ref, ((_dot(ub, w_ref[:, o:o + qw]) + b_ref[:, o:o + qw]) * Q_SCALE).astype(BF16), sp)
    o += qw
    _put_step_rows(k_ref, (_dot(ub, w_ref[:, o:o + kw]) + b_ref[:, o:o + kw]).astype(BF16), sp)
    o += kw
    _put_step_rows(v_ref, (_dot(ub, w_ref[:, o:o + vw]) + b_ref[:, o:o + vw]).astype(BF16), sp)
    o += vw
    _put_step_rows(gt_ref, jax.nn.sigmoid(_dot(ub, w_ref[:, o:o + gw]) + b_ref[:, o:o + gw]).astype(BF16), sp)


def _in_proj(x3, mod3, g, wst, bs, w, b, widths, sp):
    nb, nk, _ = x3.shape
    d = mod3.shape[2]
    s5w = wst.shape[0]
    blk = lambda wd: pl.BlockSpec((1, nk, sp * wd), lambda bi, j: (bi, 0, j))
    full = lambda a: pl.BlockSpec(a.shape, lambda bi, j: (0, 0))
    outs = ([jax.ShapeDtypeStruct((S5_STEP, s5w, nb * nk), BF16)]
            + [jax.ShapeDtypeStruct((nb, nk, S5_STEP * wd), BF16) for wd in widths])
    return pl.pallas_call(
        functools.partial(_inproj_kernel, widths=widths, sp=sp),
        out_shape=outs,
        grid=(nb, S5_STEP // sp),
        in_specs=[blk(d), pl.BlockSpec((1, 6, d), lambda bi, j: (bi, 0, 0)),
                  full(g), full(wst), full(bs), full(w), full(b)],
        out_specs=[pl.BlockSpec((sp, s5w, nk), lambda bi, j: (j, 0, bi))] + [blk(wd) for wd in widths],
        compiler_params=_cparams("parallel", "parallel"),
        name="in_proj",
    )(x3, mod3, g, wst, bs, w, b)


def _s5_prep_kernel(ar_ref, ai_ref, dt_ref, br_ref, bi_ref, crt_ref, cit_ref, cr_ref, ci_ref,
                    er_ref, ei_ref, vr_ref, vi_ref, k_ref, a16r_ref, a16i_ref):
    lr = ar_ref[0]
    li = ai_ref[0]
    dt = jnp.exp(dt_ref[0])
    mag = jnp.exp(lr * dt)
    abr = mag * jnp.cos(li * dt)
    abi = mag * jnp.sin(li * dt)
    den = lr * lr + li * li
    nr = abr - 1.0
    f_re = (nr * lr + abi * li) / den
    f_im = (abi * lr - nr * li) / den
    b_re = br_ref[0]
    b_im = bi_ref[0]
    bb_re = f_re * b_re - f_im * b_im
    bb_im = f_re * b_im + f_im * b_re
    crt = crt_ref[0]
    cit = cit_ref[0]
    cr = cr_ref[0]
    ci = ci_ref[0]
    pr = jnp.ones_like(lr)
    pi = jnp.zeros_like(lr)
    for j in range(S5_STEP):
        e_re = pr * bb_re - pi * bb_im
        e_im = pr * bb_im + pi * bb_re
        er_ref[0, j] = e_re
        ei_ref[0, j] = e_im
        k_ref[0, j] = (jnp.dot(cr, e_re, preferred_element_type=F32, precision=HIGHEST)
                       - jnp.dot(ci, e_im, preferred_element_type=F32, precision=HIGHEST))
        pr, pi = pr * abr - pi * abi, pr * abi + pi * abr
        vr_ref[0, j] = pr * crt - pi * cit
        vi_ref[0, j] = pr * cit + pi * crt
    a16r_ref[0] = pr
    a16i_ref[0] = pi


def _s5_prep(a_re, a_im, log_dt, b_re, b_im, c_re, c_im):
    ng, p = a_re.shape
    gc = b_re.shape[-1]
    col = lambda a: a.reshape(ng, p, 1)
    g3 = lambda g: (g, 0, 0)
    g4 = lambda g: (g, 0, 0, 0)
    spec_p1 = pl.BlockSpec((1, p, 1), g3)
    spec_pc = pl.BlockSpec((1, p, gc), g3)
    spec_cp = pl.BlockSpec((1, gc, p), g3)
    spec_jpc = pl.BlockSpec((1, S5_STEP, p, gc), g4)
    shp_jpc = jax.ShapeDtypeStruct((ng, S5_STEP, p, gc), F32)
    return pl.pallas_call(
        _s5_prep_kernel,
        out_shape=[shp_jpc, shp_jpc, shp_jpc, shp_jpc,
                   jax.ShapeDtypeStruct((ng, S5_STEP, gc, gc), F32),
                   jax.ShapeDtypeStruct((ng, p, 1), F32),
                   jax.ShapeDtypeStruct((ng, p, 1), F32)],
        grid=(ng,),
        in_specs=[spec_p1, spec_p1, pl.BlockSpec((1, 1, 1), g3), spec_pc, spec_pc,
                  spec_pc, spec_pc, spec_cp, spec_cp],
        out_specs=[spec_jpc, spec_jpc, spec_jpc, spec_jpc,
                   pl.BlockSpec((1, S5_STEP, gc, gc), g4), spec_p1, spec_p1],
        compiler_params=_cparams("parallel"),
        name="s5_prep",
    )(col(a_re), col(a_im), log_dt.reshape(ng, 1, 1), b_re, b_im,
      jnp.swapaxes(c_re, 1, 2), jnp.swapaxes(c_im, 1, 2), c_re, c_im)


def _s5_operators(e_re, e_im, v_re, v_im, kk, a16r, a16i, d_skip):
    ng, n, p, gc = e_re.shape
    na = ng // 2
    eye = jnp.eye(2, dtype=F32)
    idx = jnp.arange(n)
    lag = idx[:, None] - idx[None, :]
    toe = jnp.take(kk, jnp.clip(lag, 0, n - 1), axis=1)
    toe = jnp.where((lag >= 0)[None, :, :, None, None], toe, 0.0)
    mt = jnp.einsum('agtsoi,hg->athosgi', toe.reshape(na, 2, n, n, gc, gc), eye)
    mt = mt.reshape(na, 2 * n * gc, 2 * n * gc)

    def to_w(e):
        w = jnp.einsum('agspi,hg->ahpsgi', e[:, ::-1].reshape(na, 2, n, p, gc), eye)
        return w.reshape(na, 2 * p, 2 * n * gc)

    def to_v(v):
        w = jnp.einsum('agtpo,hg->athogp', v.reshape(na, 2, n, p, gc), eye)
        return w.reshape(na, 2 * n * gc, 2 * p)

    pair_row = lambda a: a.reshape(na, 1, 2 * p)
    d_col = jnp.broadcast_to(d_skip.reshape(na, 1, 2, gc), (na, n, 2, gc)).reshape(na, 2 * n * gc, 1)
    return (mt.astype(BF16), to_w(e_re).astype(BF16), to_w(e_im).astype(BF16), to_v(v_re).astype(BF16),
            to_v(-v_im).astype(BF16), pair_row(a16r), pair_row(a16i), d_col)


def _s5_core_kernel(u_ref, m_ref, wr_ref, wi_ref, vr_ref, vi_ref, ar_ref, ai_ref, d_ref, o_ref,
                    sr_ref, si_ref, xr_ref, xi_ref, *, nb, nk):
    n, gw, r = u_ref.shape
    u = u_ref[...].reshape(n * gw, r)
    sr_ref[...] = _dot(wr_ref[0], u).T
    si_ref[...] = _dot(wi_ref[0], u).T
    a_r = ar_ref[0]
    a_i = ai_ref[0]

    def step(k, carry):
        out = []
        for b in range(nb):
            x_r, x_i = carry[2 * b], carry[2 * b + 1]
            row = pl.ds(b * nk + k, 1)
            xr_ref[row, :] = x_r
            xi_ref[row, :] = x_i
            out.append(a_r * x_r - a_i * x_i + sr_ref[row, :])
            out.append(a_r * x_i + a_i * x_r + si_ref[row, :])
        return tuple(out)

    zero = jnp.zeros((1, xr_ref.shape[1]), F32)
    lax.fori_loop(0, nk, step, (zero,) * (2 * nb), unroll=4)
    y = (_dot(m_ref[0], u) + _dot(vr_ref[0], xr_ref[...].T.astype(BF16))
         + _dot(vi_ref[0], xi_ref[...].T.astype(BF16)))
    y = y + d_ref[0] * u.astype(F32)
    o_ref[...] = jax.nn.gelu(y).astype(BF16).reshape(n, gw, r)


def _s5_core(u_t, ops, nb, nk):
    m, wr, wi, vr, vi, ar, ai, d_col = ops
    npair, w2, _ = m.shape
    p2 = wr.shape[1]
    n, _, r = u_t.shape
    gw = w2 // n
    g3 = lambda g: (g, 0, 0)
    data = pl.BlockSpec((n, gw, r), lambda g: (0, g, 0))
    return pl.pallas_call(
        functools.partial(_s5_core_kernel, nb=nb, nk=nk),
        out_shape=jax.ShapeDtypeStruct(u_t.shape, BF16),
        grid=(npair,),
        in_specs=[data, pl.BlockSpec((1, w2, w2), g3),
                  pl.BlockSpec((1, p2, w2), g3), pl.BlockSpec((1, p2, w2), g3),
                  pl.BlockSpec((1, w2, p2), g3), pl.BlockSpec((1, w2, p2), g3),
                  pl.BlockSpec((1, 1, p2), g3), pl.BlockSpec((1, 1, p2), g3),
                  pl.BlockSpec((1, w2, 1), g3)],
        out_specs=data,
        scratch_shapes=[pltpu.VMEM((r, p2), F32)] * 4,
        compiler_params=_cparams("parallel"),
        name="s5_core",
    )(u_t, m, wr, wi, vr, vi, ar, ai, d_col)


def _attn_kernel(q_ref, k_ref, v_ref, lq1_ref, lk1_ref, lq2_ref, lk2_ref, g_ref, o_ref,
                 m_ref, l_ref, acc_ref, *, bq, lambda_init):
    qi = pl.program_id(2)
    q = q_ref[0]
    lane = lax.broadcasted_iota(jnp.int32, (1, 2 * DA_HEAD_DIM), 1)
    zero = jnp.zeros_like(q)
    qs = jnp.concatenate([jnp.where(lane < DA_HEAD_DIM, q, zero),
                          jnp.where(lane >= DA_HEAD_DIM, q, zero)], axis=0)
    m_ref[...] = jnp.full(m_ref.shape, -jnp.inf, F32)
    l_ref[...] = jnp.zeros(l_ref.shape, F32)
    acc_ref[...] = jnp.zeros(acc_ref.shape, F32)
    reps = bq // LANES

    def block(j, mask):
        ks = pl.ds(pl.multiple_of(j * bq, bq), bq)
        s = _dot_nt(qs, k_ref[0, ks, :])
        if mask is not None:
            s = jnp.where(mask, s, -jnp.inf)
        m_old = m_ref[...]
        m_new = jnp.maximum(m_old, jnp.max(s, axis=-1, keepdims=True))
        alpha = jnp.exp2(m_old - m_new)
        p = jnp.exp2(s - jnp.concatenate([m_new] * reps, axis=1))
        l_ref[...] = alpha * l_ref[...] + jnp.sum(p, axis=-1, keepdims=True)
        acc_ref[...] = alpha * acc_ref[...] + _dot(p.astype(BF16), v_ref[0, ks, :])
        m_ref[...] = m_new

    def full_block(j, carry):
        block(j, None)
        return carry

    lax.fori_loop(0, qi, full_block, 0)
    r_chunk = (lax.broadcasted_iota(jnp.int32, (2 * bq, bq), 0) % bq) // CHUNK
    c_chunk = lax.broadcasted_iota(jnp.int32, (2 * bq, bq), 1) // CHUNK
    block(qi, c_chunk <= r_chunk)

    lam = (jnp.exp(jnp.sum(lq1_ref[...] * lk1_ref[...], axis=-1, keepdims=True))
           - jnp.exp(jnp.sum(lq2_ref[...] * lk2_ref[...], axis=-1, keepdims=True)) + lambda_init)
    o_all = acc_ref[...] / l_ref[...]
    o = o_all[:bq] - lam * o_all[bq:]
    o_ref[0] = (_rms(o, g_ref[...]) * (1.0 - lambda_init)).astype(BF16)


def _diff_attn(q3, k3, v3, lq1, lk1, lq2, lk2, g_subln, lambda_init, bq):
    nb, seq, _ = q3.shape
    hw = 2 * DA_HEAD_DIM
    lam_spec = pl.BlockSpec((1, DA_HEAD_DIM), lambda b, h, i: (0, 0))
    kv_spec = pl.BlockSpec((1, seq, hw), lambda b, h, i: (b, 0, h))
    return pl.pallas_call(
        functools.partial(_attn_kernel, bq=bq, lambda_init=lambda_init),
        out_shape=jax.ShapeDtypeStruct(q3.shape, BF16),
        grid=(nb, DA_HEADS, seq // bq),
        in_specs=[pl.BlockSpec((1, bq, hw), lambda b, h, i: (b, i, h)), kv_spec, kv_spec,
                  lam_spec, lam_spec, lam_spec, lam_spec,
                  pl.BlockSpec((1, hw), lambda b, h, i: (0, 0))],
        out_specs=pl.BlockSpec((1, bq, hw), lambda b, h, i: (b, i, h)),
        scratch_shapes=[pltpu.VMEM((2 * bq, hw), F32)] * 3,
        compiler_params=_cparams("parallel", "parallel", "parallel"),
        name="diff_attn",
    )(q3, k3, v3, lq1, lk1, lq2, lk2, g_subln)


def _post_kernel(yt_ref, oa_ref, gt_ref, x_ref, mod_ref, wglut_ref, bglu_ref, wbs_ref, wba_ref,
                 wout_ref, g_ref, wr_ref, br_ref, h_ref, u2_ref, cw_ref, *, sp):
    d = mod_ref.shape[2]
    mod = mod_ref[0]
    yg = []
    for i in range(sp):
        yt = yt_ref[i]
        zt = _dot(wglut_ref[...], yt) + bglu_ref[...]
        yg.append((yt.astype(F32) * jax.nn.sigmoid(zt)).T)
    yg = jnp.concatenate(yg, axis=0)
    y_ssm = _dot(yg.astype(BF16), wbs_ref[...])
    y_att = _dot(_step_rows(oa_ref, wba_ref.shape[0], sp), wba_ref[...])
    gates = _step_rows(gt_ref, 2 * d, sp).astype(F32)
    mix_in = gates[:, :d] * y_ssm + gates[:, d:] * y_att
    mix = _dot(mix_in.astype(BF16), wout_ref[...])
    h = _step_rows(x_ref, d, sp) + mod[2:3, :] * mix
    _put_step_rows(h_ref, h, sp)
    u2 = _rms(h, g_ref[...]) * (1.0 + mod[4:5, :]) + mod[3:4, :]
    _put_step_rows(u2_ref, u2.astype(BF16), sp)

    logits = jnp.dot(u2, wr_ref[...], preferred_element_type=F32, precision=HIGHEST) + br_ref[...]
    lane = lax.broadcasted_iota(jnp.int32, logits.shape, 1)
    neg = jnp.full_like(logits, -jnp.inf)
    big = jnp.full_like(lane, LANES)
    is_grp = (lane >= N_EXPERTS) & (lane < N_EXPERTS + N_GROUPS)
    glog = jnp.where(is_grp, logits, neg)
    gmax = jnp.max(glog, axis=-1, keepdims=True)
    gp = 1.0 / jnp.sum(jnp.exp(glog - gmax), axis=-1, keepdims=True)
    gi = jnp.min(jnp.where(glog == gmax, lane, big), axis=-1, keepdims=True) - N_EXPERTS
    in_grp = (lane < N_EXPERTS) & ((lane // EXP_PER_GROUP) == gi)
    elog = jnp.where(in_grp, logits, neg)
    v1 = jnp.max(elog, axis=-1, keepdims=True)
    i1 = jnp.min(jnp.where(elog == v1, lane, big), axis=-1, keepdims=True)
    elog2 = jnp.where(lane == i1, neg, elog)
    v2 = jnp.max(elog2, axis=-1, keepdims=True)
    i2 = jnp.min(jnp.where(elog2 == v2, lane, big), axis=-1, keepdims=True)
    e2 = jnp.exp(v2 - v1)
    w1 = gp / (1.0 + e2)
    w2 = gp * e2 / (1.0 + e2)
    _put_step_rows(cw_ref, jnp.where(lane == i1, w1, 0.0) + jnp.where(lane == i2, w2, 0.0), sp)


def _post_mix(y_t, oa3, gates3, x3, mod3, wglut, bglu, wbs, wba, wout, g, wr, br, sp):
    nb, nk, _ = x3.shape
    d = mod3.shape[2]
    s5w = y_t.shape[1]
    daw = wba.shape[0]
    blk = lambda wd: pl.BlockSpec((1, nk, sp * wd), lambda bi, j: (bi, 0, j))
    full = lambda a: pl.BlockSpec(a.shape, lambda bi, j: (0, 0))
    shp = lambda wd, dt: jax.ShapeDtypeStruct((nb, nk, S5_STEP * wd), dt)
    return pl.pallas_call(
        functools.partial(_post_kernel, sp=sp),
        out_shape=[shp(d, F32), shp(d, BF16), shp(LANES, F32)],
        grid=(nb, S5_STEP // sp),
        in_specs=[pl.BlockSpec((sp, s5w, nk), lambda bi, j: (j, 0, bi)), blk(daw), blk(2 * d), blk(d),
                  pl.BlockSpec((1, 6, d), lambda bi, j: (bi, 0, 0)),
                  full(wglut), full(bglu), full(wbs), full(wba), full(wout), full(g), full(wr), full(br)],
        out_specs=[blk(d), blk(d), blk(LANES)],
        compiler_params=_cparams("parallel", "parallel"),
        name="post_mix",
    )(y_t, oa3, gates3, x3, mod3, wglut, bglu, wbs, wba, wout, g, wr, br)


def _moe_kernel(u_ref, cw_ref, wgu_ref, wd_ref, h_ref, mod_ref, g_ref, o_ref, acc_ref, *, final_norm):
    e = pl.program_id(1)

    @pl.when(e == 0)
    def _():
        acc_ref[...] = jnp.zeros(acc_ref.shape, F32)

    gu = _dot(u_ref[...], wgu_ref[0])
    gate = gu[:, :D_EXPERT]
    hdn = gate * jax.nn.sigmoid(gate) * gu[:, D_EXPERT:]
    cw_all = cw_ref[...]
    lane = lax.broadcasted_iota(jnp.int32, cw_all.shape, 1)
    cw = jnp.sum(jnp.where(lane == e, cw_all, 0.0), axis=-1, keepdims=True)
    acc_ref[...] += _dot((hdn * cw).astype(BF16), wd_ref[0])

    @pl.when(e == pl.num_programs(1) - 1)
    def _():
        h = h_ref[...] + mod_ref[0][5:6, :] * acc_ref[...]
        o_ref[...] = _rms(h, g_ref[...]) if final_norm else h


def _moe_final(u2, cw, wgu, wd, h1, mod3, g_final, seq, tm, final_norm):
    t, d = h1.shape
    tpb = seq // tm
    row = lambda i, e: (i, 0)
    return pl.pallas_call(
        functools.partial(_moe_kernel, final_norm=final_norm),
        out_shape=jax.ShapeDtypeStruct((t, d), F32),
        grid=(t // tm, N_EXPERTS),
        in_specs=[pl.BlockSpec((tm, d), row), pl.BlockSpec((tm, LANES), row),
                  pl.BlockSpec((1, d, 2 * D_EXPERT), lambda i, e: (e, 0, 0)),
                  pl.BlockSpec((1, D_EXPERT, d), lambda i, e: (e, 0, 0)),
                  pl.BlockSpec((tm, d), row),
                  pl.BlockSpec((1, 6, d), lambda i, e: (i // tpb, 0, 0)),
                  pl.BlockSpec((1, d), lambda i, e: (0, 0))],
        out_specs=pl.BlockSpec((tm, d), row),
        scratch_shapes=[pltpu.VMEM((tm, d), F32)],
        compiler_params=_cparams("parallel", "arbitrary"),
        name="moe_final",
    )(u2, cw, wgu, wd, h1, mod3, g_final)


def kernel(x, c, w_ada, b_ada, g_norm_mix, w_in, b_in, s5_a_re, s5_a_im, s5_b_re, s5_b_im, s5_c_re, s5_c_im, s5_d, s5_log_dt, w_glu, b_glu, lambda_q1, lambda_k1, lambda_q2, lambda_k2, g_subln, w_br_ssm, w_br_attn, w_out, g_norm_ffn, w_router_grp, b_router_grp, w_router_exp, b_router_exp, w_exp_gate, w_exp_up, w_exp_down, g_final):
    nb, seq, d = x.shape
    depth = w_ada.shape[0]
    s5w = s5_d.shape[1]
    daw = w_br_attn.shape[1]
    ng = s5w // S5_GROUP
    nk = seq // S5_STEP
    assert seq % 512 == 0 and nk % LANES == 0 and ng % 2 == 0
    widths = (daw, daw, daw, 2 * d)
    sp = 2
    bq = 512
    row = lambda a: a.reshape(1, -1)
    col = lambda a: a.reshape(-1, 1)
    by_step = lambda a: a.reshape(nb, nk, -1)

    h = x.reshape(nb * seq, d)
    for l in range(depth):
        lambda_init = 0.8 - 0.6 * math.exp(-0.3 * l)
        mod3 = _ada_mod(c, w_ada[l], row(b_ada[l])).reshape(nb, 6, d)

        w_in_b = w_in[l].astype(BF16)
        u_t, q, k, v, gates = _in_proj(by_step(h), mod3, row(g_norm_mix[l]), w_in_b[:, :s5w].T,
                                       col(b_in[l][:s5w]), w_in_b[:, s5w:], row(b_in[l][s5w:]), widths, sp)

        prep = _s5_prep(s5_a_re[l], s5_a_im[l], s5_log_dt[l], s5_b_re[l], s5_b_im[l], s5_c_re[l], s5_c_im[l])
        y_t = _s5_core(u_t, _s5_operators(*prep, s5_d[l]), nb, nk)

        as3 = lambda a: a.reshape(nb, seq, daw)
        oa = _diff_attn(as3(q), as3(k), as3(v), row(lambda_q1[l]), row(lambda_k1[l]), row(lambda_q2[l]),
                        row(lambda_k2[l]), row(g_subln[l]), lambda_init, bq)

        w_router = jnp.zeros((d, LANES), F32)
        w_router = w_router.at[:, :N_EXPERTS].set(w_router_exp[l])
        w_router = w_router.at[:, N_EXPERTS:N_EXPERTS + N_GROUPS].set(w_router_grp[l])
        b_router = jnp.zeros((1, LANES), F32)
        b_router = b_router.at[0, :N_EXPERTS].set(b_router_exp[l])
        b_router = b_router.at[0, N_EXPERTS:N_EXPERTS + N_GROUPS].set(b_router_grp[l])
        h1, u2, cw = _post_mix(y_t, by_step(oa), gates, by_step(h), mod3, w_glu[l].T.astype(BF16),
                               col(b_glu[l]), w_br_ssm[l].astype(BF16), w_br_attn[l].astype(BF16),
                               w_out[l].astype(BF16), row(g_norm_ffn[l]), w_router, b_router, sp)
        flat = lambda a: a.reshape(nb * seq, -1)

        wgu = jnp.concatenate([w_exp_gate[l], w_exp_up[l]], axis=-1).astype(BF16)
        h = _moe_final(flat(u2), flat(cw), wgu, w_exp_down[l].astype(BF16), flat(h1), mod3, row(g_final),
                       seq, 1024, final_norm=(l == depth - 1))
    return h.reshape(nb, seq, d)
```

```python
import functools
import math

import jax
import jax.numpy as jnp
from jax import lax
from jax.experimental import pallas as pl
from jax.experimental.pallas import tpu as pltpu

EPS = 1e-6
CHUNK = 64
S5_GROUP = 16
S5_STATE = 64
S5_STEP = 16
S5_TILE = 2048
DA_HEADS = 8
DA_HEAD_DIM = 64
N_GROUPS = 4
EXP_PER_GROUP = 8
N_EXPERTS = N_GROUPS * EXP_PER_GROUP
D_EXPERT = 256
LANES = 128
Q_SCALE = DA_HEAD_DIM ** -0.5 * math.log2(math.e)
VMEM_LIMIT = 56 * 1024 * 1024

F32 = jnp.float32
BF16 = jnp.bfloat16
HIGHEST = lax.Precision.HIGHEST


def _cparams(*sem):
    return pltpu.CompilerParams(dimension_semantics=sem, vmem_limit_bytes=VMEM_LIMIT)


def _dot(a, b):
    return jnp.dot(a, b, preferred_element_type=F32)


def _dot_nt(a, b):
    return lax.dot_general(a, b, (((1,), (1,)), ((), ())), preferred_element_type=F32)


def _rms(x, g):
    return x * lax.rsqrt(jnp.mean(x * x, axis=-1, keepdims=True) + EPS) * g


def _ada_kernel(c_ref, w_ref, b_ref, o_ref):
    c = c_ref[...]
    cs = c * jax.nn.sigmoid(c)
    o_ref[...] = jnp.dot(cs, w_ref[...], preferred_element_type=F32, precision=HIGHEST) + b_ref[...]


def _ada_mod(c, w, b):
    nb, d = c.shape
    n = w.shape[1]
    tn = n // 4
    return pl.pallas_call(
        _ada_kernel,
        out_shape=jax.ShapeDtypeStruct((nb, n), F32),
        grid=(n // tn,),
        in_specs=[pl.BlockSpec((nb, d), lambda j: (0, 0)),
                  pl.BlockSpec((d, tn), lambda j: (0, j)),
                  pl.BlockSpec((1, tn), lambda j: (0, j))],
        out_specs=pl.BlockSpec((nb, tn), lambda j: (0, j)),
        compiler_params=_cparams("parallel"),
        name="ada_mod",
    )(c, w, b)


def _modulated_norm(x_ref, mod_ref, g_ref):
    mod = mod_ref[0]
    return _rms(x_ref[...], g_ref[...]) * (1.0 + mod[1:2, :]) + mod[0:1, :]


def _inproj_kernel(x_ref, mod_ref, g_ref, w_ref, b_ref, q_ref, k_ref, v_ref, gt_ref, *, widths):
    ub = _modulated_norm(x_ref, mod_ref, g_ref).astype(BF16)
    qw, kw, vw, gw = widths
    o = 0
    q_ref[...] = ((_dot(ub, w_ref[:, o:o + qw]) + b_ref[:, o:o + qw]) * Q_SCALE).astype(BF16)
    o += qw
    k_ref[...] = (_dot(ub, w_ref[:, o:o + kw]) + b_ref[:, o:o + kw]).astype(BF16)
    o += kw
    v_ref[...] = (_dot(ub, w_ref[:, o:o + vw]) + b_ref[:, o:o + vw]).astype(BF16)
    o += vw
    gt_ref[...] = jax.nn.sigmoid(_dot(ub, w_ref[:, o:o + gw]) + b_ref[:, o:o + gw]).astype(BF16)


def _in_proj(x2, mod3, g, w, b, seq, widths, tm):
    t, d = x2.shape
    tpb = seq // tm
    row = lambda i: (i, 0)
    full = lambda a: pl.BlockSpec(a.shape, lambda i: (0, 0))
    return pl.pallas_call(
        functools.partial(_inproj_kernel, widths=widths),
        out_shape=[jax.ShapeDtypeStruct((t, wd), BF16) for wd in widths],
        grid=(t // tm,),
        in_specs=[pl.BlockSpec((tm, d), row), pl.BlockSpec((1, 6, d), lambda i: (i // tpb, 0, 0)),
                  full(g), full(w), full(b)],
        out_specs=[pl.BlockSpec((tm, wd), row) for wd in widths],
        compiler_params=_cparams("parallel"),
        name="in_proj",
    )(x2, mod3, g, w, b)


def _s5_in_kernel(x_ref, mod_ref, g_ref, wst_ref, bs_ref, ut_ref, slab_ref):
    u = _modulated_norm(x_ref, mod_ref, g_ref)
    nslab = slab_ref.shape[0]
    nk = x_ref.shape[0] // S5_STEP
    for j in range(nslab):
        slab_ref[j] = u[:, j * LANES:(j + 1) * LANES]
    for s in range(S5_STEP):
        us = jnp.concatenate([slab_ref[j, pl.ds(s, nk, stride=S5_STEP), :] for j in range(nslab)], axis=1)
        ut_ref[s] = (_dot_nt(wst_ref[...], us.astype(BF16)) + bs_ref[...]).astype(BF16)


def _s5_in(x2, mod3, g, wst, bs, seq, tm):
    t, d = x2.shape
    s5w = wst.shape[0]
    nkt = tm // S5_STEP
    tpb = seq // tm
    full = lambda a: pl.BlockSpec(a.shape, lambda i: (0, 0))
    return pl.pallas_call(
        _s5_in_kernel,
        out_shape=jax.ShapeDtypeStruct((S5_STEP, s5w, t // S5_STEP), BF16),
        grid=(t // tm,),
        in_specs=[pl.BlockSpec((tm, d), lambda i: (i, 0)), pl.BlockSpec((1, 6, d), lambda i: (i // tpb, 0, 0)),
                  full(g), full(wst), full(bs)],
        out_specs=pl.BlockSpec((S5_STEP, s5w, nkt), lambda i: (0, 0, i)),
        scratch_shapes=[pltpu.VMEM((d // LANES, tm, LANES), F32)],
        compiler_params=_cparams("parallel"),
        name="s5_in",
    )(x2, mod3, g, wst, bs)


def _s5_out_kernel(yt_ref, wglut_ref, bglu_ref, wbs_ref, o_ref, slab_ref):
    nslab = slab_ref.shape[0]
    nk = yt_ref.shape[2]
    for s in range(S5_STEP):
        yt = yt_ref[s]
        zt = _dot(wglut_ref[...], yt) + bglu_ref[...]
        yg = (yt.astype(F32) * jax.nn.sigmoid(zt)).T
        for j in range(nslab):
            slab_ref[j, pl.ds(s, nk, stride=S5_STEP), :] = yg[:, j * LANES:(j + 1) * LANES]
    yg = jnp.concatenate([slab_ref[j] for j in range(nslab)], axis=1)
    o_ref[...] = _dot(yg.astype(BF16), wbs_ref[...]).astype(BF16)


def _s5_out(y_t, wglut, bglu, wbs, tm):
    n, s5w, r = y_t.shape
    d = wbs.shape[1]
    nkt = tm // n
    full = lambda a: pl.BlockSpec(a.shape, lambda i: (0, 0))
    return pl.pallas_call(
        _s5_out_kernel,
        out_shape=jax.ShapeDtypeStruct((r * n, d), BF16),
        grid=(r // nkt,),
        in_specs=[pl.BlockSpec((n, s5w, nkt), lambda i: (0, 0, i)), full(wglut), full(bglu), full(wbs)],
        out_specs=pl.BlockSpec((tm, d), lambda i: (i, 0)),
        scratch_shapes=[pltpu.VMEM((s5w // LANES, tm, LANES), F32)],
        compiler_params=_cparams("parallel"),
        name="s5_out",
    )(y_t, wglut, bglu, wbs)


def _s5_prep_kernel(ar_ref, ai_ref, dt_ref, br_ref, bi_ref, crt_ref, cit_ref, cr_ref, ci_ref,
                    er_ref, ei_ref, vr_ref, vi_ref, k_ref, a16r_ref, a16i_ref):
    lr = ar_ref[0]
    li = ai_ref[0]
    dt = jnp.exp(dt_ref[0])
    mag = jnp.exp(lr * dt)
    abr = mag * jnp.cos(li * dt)
    abi = mag * jnp.sin(li * dt)
    den = lr * lr + li * li
    nr = abr - 1.0
    f_re = (nr * lr + abi * li) / den
    f_im = (abi * lr - nr * li) / den
    b_re = br_ref[0]
    b_im = bi_ref[0]
    bb_re = f_re * b_re - f_im * b_im
    bb_im = f_re * b_im + f_im * b_re
    crt = crt_ref[0]
    cit = cit_ref[0]
    cr = cr_ref[0]
    ci = ci_ref[0]
    pr = jnp.ones_like(lr)
    pi = jnp.zeros_like(lr)
    for j in range(S5_STEP):
        e_re = pr * bb_re - pi * bb_im
        e_im = pr * bb_im + pi * bb_re
        er_ref[0, j] = e_re
        ei_ref[0, j] = e_im
        k_ref[0, j] = (jnp.dot(cr, e_re, preferred_element_type=F32, precision=HIGHEST)
                       - jnp.dot(ci, e_im, preferred_element_type=F32, precision=HIGHEST))
        pr, pi = pr * abr - pi * abi, pr * abi + pi * abr
        vr_ref[0, j] = pr * crt - pi * cit
        vi_ref[0, j] = pr * cit + pi * crt
    a16r_ref[0] = pr
    a16i_ref[0] = pi


def _s5_prep(a_re, a_im, log_dt, b_re, b_im, c_re, c_im):
    ng, p = a_re.shape
    gc = b_re.shape[-1]
    col = lambda a: a.reshape(ng, p, 1)
    g3 = lambda g: (g, 0, 0)
    g4 = lambda g: (g, 0, 0, 0)
    spec_p1 = pl.BlockSpec((1, p, 1), g3)
    spec_pc = pl.BlockSpec((1, p, gc), g3)
    spec_cp = pl.BlockSpec((1, gc, p), g3)
    spec_jpc = pl.BlockSpec((1, S5_STEP, p, gc), g4)
    shp_jpc = jax.ShapeDtypeStruct((ng, S5_STEP, p, gc), F32)
    return pl.pallas_call(
        _s5_prep_kernel,
        out_shape=[shp_jpc, shp_jpc, shp_jpc, shp_jpc,
                   jax.ShapeDtypeStruct((ng, S5_STEP, gc, gc), F32),
                   jax.ShapeDtypeStruct((ng, p, 1), F32),
                   jax.ShapeDtypeStruct((ng, p, 1), F32)],
        grid=(ng,),
        in_specs=[spec_p1, spec_p1, pl.BlockSpec((1, 1, 1), g3), spec_pc, spec_pc,
                  spec_pc, spec_pc, spec_cp, spec_cp],
        out_specs=[spec_jpc, spec_jpc, spec_jpc, spec_jpc,
                   pl.BlockSpec((1, S5_STEP, gc, gc), g4), spec_p1, spec_p1],
        compiler_params=_cparams("parallel"),
        name="s5_prep",
    )(col(a_re), col(a_im), log_dt.reshape(ng, 1, 1), b_re, b_im,
      jnp.swapaxes(c_re, 1, 2), jnp.swapaxes(c_im, 1, 2), c_re, c_im)


def _s5_operators(e_re, e_im, v_re, v_im, kk, a16r, a16i, d_skip):
    ng, n, p, gc = e_re.shape
    na = ng // 2
    eye = jnp.eye(2, dtype=F32)
    idx = jnp.arange(n)
    lag = idx[:, None] - idx[None, :]
    toe = jnp.take(kk, jnp.clip(lag, 0, n - 1), axis=1)
    toe = jnp.where((lag >= 0)[None, :, :, None, None], toe, 0.0)
    mt = jnp.einsum('agtsoi,hg->athosgi', toe.reshape(na, 2, n, n, gc, gc), eye)
    mt = mt.reshape(na, 2 * n * gc, 2 * n * gc)

    def to_w(e):
        w = jnp.einsum('agspi,hg->ahpsgi', e[:, ::-1].reshape(na, 2, n, p, gc), eye)
        return w.reshape(na, 2 * p, 2 * n * gc)

    def to_v(v):
        w = jnp.einsum('agtpo,hg->athogp', v.reshape(na, 2, n, p, gc), eye)
        return w.reshape(na, 2 * n * gc, 2 * p)

    pair_row = lambda a: a.reshape(na, 1, 2 * p)
    d_col = jnp.broadcast_to(d_skip.reshape(na, 1, 2, gc), (na, n, 2, gc)).reshape(na, 2 * n * gc, 1)
    return (mt.astype(BF16), to_w(e_re).astype(BF16), to_w(e_im).astype(BF16), to_v(v_re).astype(BF16),
            to_v(-v_im).astype(BF16), pair_row(a16r), pair_row(a16i), d_col)


def _s5_core_kernel(u_ref, m_ref, wr_ref, wi_ref, vr_ref, vi_ref, ar_ref, ai_ref, d_ref, o_ref,
                    sr_ref, si_ref, xr_ref, xi_ref, *, nb, nk):
    n, gw, r = u_ref.shape
    u = u_ref[...].reshape(n * gw, r)
    sr_ref[...] = _dot(wr_ref[0], u).T
    si_ref[...] = _dot(wi_ref[0], u).T
    a_r = ar_ref[0]
    a_i = ai_ref[0]

    def step(k, carry):
        out = []
        for b in range(nb):
            x_r, x_i = carry[2 * b], carry[2 * b + 1]
            row = pl.ds(b * nk + k, 1)
            xr_ref[row, :] = x_r
            xi_ref[row, :] = x_i
            out.append(a_r * x_r - a_i * x_i + sr_ref[row, :])
            out.append(a_r * x_i + a_i * x_r + si_ref[row, :])
        return tuple(out)

    zero = jnp.zeros((1, xr_ref.shape[1]), F32)
    lax.fori_loop(0, nk, step, (zero,) * (2 * nb), unroll=4)
    y = (_dot(m_ref[0], u) + _dot(vr_ref[0], xr_ref[...].T.astype(BF16))
         + _dot(vi_ref[0], xi_ref[...].T.astype(BF16)))
    y = y + d_ref[0] * u.astype(F32)
    o_ref[...] = jax.nn.gelu(y).astype(BF16).reshape(n, gw, r)


def _s5_core(u_t, ops, nb, nk):
    m, wr, wi, vr, vi, ar, ai, d_col = ops
    npair, w2, _ = m.shape
    p2 = wr.shape[1]
    n, _, r = u_t.shape
    gw = w2 // n
    g3 = lambda g: (g, 0, 0)
    data = pl.BlockSpec((n, gw, r), lambda g: (0, g, 0))
    return pl.pallas_call(
        functools.partial(_s5_core_kernel, nb=nb, nk=nk),
        out_shape=jax.ShapeDtypeStruct(u_t.shape, BF16),
        grid=(npair,),
        in_specs=[data, pl.BlockSpec((1, w2, w2), g3),
                  pl.BlockSpec((1, p2, w2), g3), pl.BlockSpec((1, p2, w2), g3),
                  pl.BlockSpec((1, w2, p2), g3), pl.BlockSpec((1, w2, p2), g3),
                  pl.BlockSpec((1, 1, p2), g3), pl.BlockSpec((1, 1, p2), g3),
                  pl.BlockSpec((1, w2, 1), g3)],
        out_specs=data,
        scratch_shapes=[pltpu.VMEM((r, p2), F32)] * 4,
        compiler_params=_cparams("parallel"),
        name="s5_core",
    )(u_t, m, wr, wi, vr, vi, ar, ai, d_col)


def _attn_kernel(q_ref, k_ref, v_ref, lq1_ref, lk1_ref, lq2_ref, lk2_ref, g_ref, o_ref,
                 m_ref, l_ref, acc_ref, *, bq, lambda_init):
    qi = pl.program_id(2)
    q = q_ref[0]
    lane = lax.broadcasted_iota(jnp.int32, (1, 2 * DA_HEAD_DIM), 1)
    zero = jnp.zeros_like(q)
    qs = jnp.concatenate([jnp.where(lane < DA_HEAD_DIM, q, zero),
                          jnp.where(lane >= DA_HEAD_DIM, q, zero)], axis=0)
    m_ref[...] = jnp.full(m_ref.shape, -jnp.inf, F32)
    l_ref[...] = jnp.zeros(l_ref.shape, F32)
    acc_ref[...] = jnp.zeros(acc_ref.shape, F32)
    reps = bq // LANES

    def block(j, mask):
        ks = pl.ds(pl.multiple_of(j * bq, bq), bq)
        s = _dot_nt(qs, k_ref[0, ks, :])
        if mask is not None:
            s = jnp.where(mask, s, -jnp.inf)
        m_old = m_ref[...]
        m_new = jnp.maximum(m_old, jnp.max(s, axis=-1, keepdims=True))
        alpha = jnp.exp2(m_old - m_new)
        p = jnp.exp2(s - jnp.concatenate([m_new] * reps, axis=1))
        l_ref[...] = alpha * l_ref[...] + jnp.sum(p, axis=-1, keepdims=True)
        acc_ref[...] = alpha * acc_ref[...] + _dot(p.astype(BF16), v_ref[0, ks, :])
        m_ref[...] = m_new

    def full_block(j, carry):
        block(j, None)
        return carry

    lax.fori_loop(0, qi, full_block, 0)
    r_chunk = (lax.broadcasted_iota(jnp.int32, (2 * bq, bq), 0) % bq) // CHUNK
    c_chunk = lax.broadcasted_iota(jnp.int32, (2 * bq, bq), 1) // CHUNK
    block(qi, c_chunk <= r_chunk)

    lam = (jnp.exp(jnp.sum(lq1_ref[...] * lk1_ref[...], axis=-1, keepdims=True))
           - jnp.exp(jnp.sum(lq2_ref[...] * lk2_ref[...], axis=-1, keepdims=True)) + lambda_init)
    o_all = acc_ref[...] / l_ref[...]
    o = o_all[:bq] - lam * o_all[bq:]
    o_ref[0] = (_rms(o, g_ref[...]) * (1.0 - lambda_init)).astype(BF16)


def _diff_attn(q3, k3, v3, lq1, lk1, lq2, lk2, g_subln, lambda_init, bq):
    nb, seq, _ = q3.shape
    hw = 2 * DA_HEAD_DIM
    lam_spec = pl.BlockSpec((1, DA_HEAD_DIM), lambda b, h, i: (0, 0))
    kv_spec = pl.BlockSpec((1, seq, hw), lambda b, h, i: (b, 0, h))
    return pl.pallas_call(
        functools.partial(_attn_kernel, bq=bq, lambda_init=lambda_init),
        out_shape=jax.ShapeDtypeStruct(q3.shape, BF16),
        grid=(nb, DA_HEADS, seq // bq),
        in_specs=[pl.BlockSpec((1, bq, hw), lambda b, h, i: (b, i, h)), kv_spec, kv_spec,
                  lam_spec, lam_spec, lam_spec, lam_spec,
                  pl.BlockSpec((1, hw), lambda b, h, i: (0, 0))],
        out_specs=pl.BlockSpec((1, bq, hw), lambda b, h, i: (b, i, h)),
        scratch_shapes=[pltpu.VMEM((2 * bq, hw), F32)] * 3,
        compiler_params=_cparams("parallel", "parallel", "parallel"),
        name="diff_attn",
    )(q3, k3, v3, lq1, lk1, lq2, lk2, g_subln)


def _post_kernel(ys_ref, oa_ref, gt_ref, x_ref, mod_ref, wba_ref, wout_ref, g_ref, wr_ref, br_ref,
                 h_ref, u2_ref, cw_ref):
    d = x_ref.shape[1]
    mod = mod_ref[0]
    y_att = _dot(oa_ref[...], wba_ref[...])
    gates = gt_ref[...].astype(F32)
    mix_in = gates[:, :d] * ys_ref[...].astype(F32) + gates[:, d:] * y_att
    mix = _dot(mix_in.astype(BF16), wout_ref[...])
    h = x_ref[...] + mod[2:3, :] * mix
    h_ref[...] = h
    u2 = _rms(h, g_ref[...]) * (1.0 + mod[4:5, :]) + mod[3:4, :]
    u2_ref[...] = u2.astype(BF16)

    logits = jnp.dot(u2, wr_ref[...], preferred_element_type=F32, precision=HIGHEST) + br_ref[...]
    lane = lax.broadcasted_iota(jnp.int32, logits.shape, 1)
    neg = jnp.full_like(logits, -jnp.inf)
    big = jnp.full_like(lane, LANES)
    is_grp = (lane >= N_EXPERTS) & (lane < N_EXPERTS + N_GROUPS)
    glog = jnp.where(is_grp, logits, neg)
    gmax = jnp.max(glog, axis=-1, keepdims=True)
    gp = 1.0 / jnp.sum(jnp.exp(glog - gmax), axis=-1, keepdims=True)
    gi = jnp.min(jnp.where(glog == gmax, lane, big), axis=-1, keepdims=True) - N_EXPERTS
    in_grp = (lane < N_EXPERTS) & ((lane // EXP_PER_GROUP) == gi)
    elog = jnp.where(in_grp, logits, neg)
    v1 = jnp.max(elog, axis=-1, keepdims=True)
    i1 = jnp.min(jnp.where(elog == v1, lane, big), axis=-1, keepdims=True)
    elog2 = jnp.where(lane == i1, neg, elog)
    v2 = jnp.max(elog2, axis=-1, keepdims=True)
    i2 = jnp.min(jnp.where(elog2 == v2, lane, big), axis=-1, keepdims=True)
    e2 = jnp.exp(v2 - v1)
    w1 = gp / (1.0 + e2)
    w2 = gp * e2 / (1.0 + e2)
    cw_ref[...] = jnp.where(lane == i1, w1, 0.0) + jnp.where(lane == i2, w2, 0.0)


def _post_mix(ys, oa, gates, x2, mod3, wba, wout, g, wr, br, seq, tm):
    t, d = x2.shape
    tpb = seq // tm
    row = lambda i: (i, 0)
    full = lambda a: pl.BlockSpec(a.shape, lambda i: (0, 0))
    return pl.pallas_call(
        _post_kernel,
        out_shape=[jax.ShapeDtypeStruct((t, d), F32), jax.ShapeDtypeStruct((t, d), BF16),
                   jax.ShapeDtypeStruct((t, LANES), F32)],
        grid=(t // tm,),
        in_specs=[pl.BlockSpec((tm, d), row), pl.BlockSpec((tm, oa.shape[1]), row),
                  pl.BlockSpec((tm, 2 * d), row), pl.BlockSpec((tm, d), row),
                  pl.BlockSpec((1, 6, d), lambda i: (i // tpb, 0, 0)),
                  full(wba), full(wout), full(g), full(wr), full(br)],
        out_specs=[pl.BlockSpec((tm, d), row), pl.BlockSpec((tm, d), row), pl.BlockSpec((tm, LANES), row)],
        compiler_params=_cparams("parallel"),
        name="post_mix",
    )(ys, oa, gates, x2, mod3, wba, wout, g, wr, br)


def _moe_kernel(u_ref, cw_ref, wgu_ref, wd_ref, h_ref, mod_ref, g_ref, o_ref, acc_ref, *, final_norm):
    e = pl.program_id(1)

    @pl.when(e == 0)
    def _():
        acc_ref[...] = jnp.zeros(acc_ref.shape, F32)

    gu = _dot(u_ref[...], wgu_ref[0])
    gate = gu[:, :D_EXPERT]
    hdn = gate * jax.nn.sigmoid(gate) * gu[:, D_EXPERT:]
    cw_all = cw_ref[...]
    lane = lax.broadcasted_iota(jnp.int32, cw_all.shape, 1)
    cw = jnp.sum(jnp.where(lane == e, cw_all, 0.0), axis=-1, keepdims=True)
    acc_ref[...] += _dot((hdn * cw).astype(BF16), wd_ref[0])

    @pl.when(e == pl.num_programs(1) - 1)
    def _():
        h = h_ref[...] + mod_ref[0][5:6, :] * acc_ref[...]
        o_ref[...] = _rms(h, g_ref[...]) if final_norm else h


def _moe_final(u2, cw, wgu, wd, h1, mod3, g_final, seq, tm, final_norm):
    t, d = h1.shape
    tpb = seq // tm
    row = lambda i, e: (i, 0)
    return pl.pallas_call(
        functools.partial(_moe_kernel, final_norm=final_norm),
        out_shape=jax.ShapeDtypeStruct((t, d), F32),
        grid=(t // tm, N_EXPERTS),
        in_specs=[pl.BlockSpec((tm, d), row), pl.BlockSpec((tm, LANES), row),
                  pl.BlockSpec((1, d, 2 * D_EXPERT), lambda i, e: (e, 0, 0)),
                  pl.BlockSpec((1, D_EXPERT, d), lambda i, e: (e, 0, 0)),
                  pl.BlockSpec((tm, d), row),
                  pl.BlockSpec((1, 6, d), lambda i, e: (i // tpb, 0, 0)),
                  pl.BlockSpec((1, d), lambda i, e: (0, 0))],
        out_specs=pl.BlockSpec((tm, d), row),
        scratch_shapes=[pltpu.VMEM((tm, d), F32)],
        compiler_params=_cparams("parallel", "arbitrary"),
        name="moe_final",
    )(u2, cw, wgu, wd, h1, mod3, g_final)


def kernel(x, c, w_ada, b_ada, g_norm_mix, w_in, b_in, s5_a_re, s5_a_im, s5_b_re, s5_b_im, s5_c_re, s5_c_im, s5_d, s5_log_dt, w_glu, b_glu, lambda_q1, lambda_k1, lambda_q2, lambda_k2, g_subln, w_br_ssm, w_br_attn, w_out, g_norm_ffn, w_router_grp, b_router_grp, w_router_exp, b_router_exp, w_exp_gate, w_exp_up, w_exp_down, g_final):
    nb, seq, d = x.shape
    depth = w_ada.shape[0]
    s5w = s5_d.shape[1]
    daw = w_br_attn.shape[1]
    ng = s5w // S5_GROUP
    nk = seq // S5_STEP
    assert seq % S5_TILE == 0 and ng % 2 == 0
    widths = (daw, daw, daw, 2 * d)
    tm = 512
    bq = 512
    row = lambda a: a.reshape(1, -1)
    col = lambda a: a.reshape(-1, 1)

    h = x.reshape(nb * seq, d)
    for l in range(depth):
        lambda_init = 0.8 - 0.6 * math.exp(-0.3 * l)
        mod3 = _ada_mod(c, w_ada[l], row(b_ada[l])).reshape(nb, 6, d)

        w_in_b = w_in[l].astype(BF16)
        q, k, v, gates = _in_proj(h, mod3, row(g_norm_mix[l]), w_in_b[:, s5w:], row(b_in[l][s5w:]),
                                  seq, widths, tm)

        u_t = _s5_in(h, mod3, row(g_norm_mix[l]), w_in_b[:, :s5w].T, col(b_in[l][:s5w]), seq, S5_TILE)
        prep = _s5_prep(s5_a_re[l], s5_a_im[l], s5_log_dt[l], s5_b_re[l], s5_b_im[l], s5_c_re[l], s5_c_im[l])
        y_t = _s5_core(u_t, _s5_operators(*prep, s5_d[l]), nb, nk)
        ys = _s5_out(y_t, w_glu[l].T.astype(BF16), col(b_glu[l]), w_br_ssm[l].astype(BF16), S5_TILE)

        as3 = lambda a: a.reshape(nb, seq, daw)
        oa = _diff_attn(as3(q), as3(k), as3(v), row(lambda_q1[l]), row(lambda_k1[l]), row(lambda_q2[l]),
                        row(lambda_k2[l]), row(g_subln[l]), lambda_init, bq).reshape(nb * seq, daw)

        w_router = jnp.zeros((d, LANES), F32)
        w_router = w_router.at[:, :N_EXPERTS].set(w_router_exp[l])
        w_router = w_router.at[:, N_EXPERTS:N_EXPERTS + N_GROUPS].set(w_router_grp[l])
        b_router = jnp.zeros((1, LANES), F32)
        b_router = b_router.at[0, :N_EXPERTS].set(b_router_exp[l])
        b_router = b_router.at[0, N_EXPERTS:N_EXPERTS + N_GROUPS].set(b_router_grp[l])
        h1, u2, cw = _post_mix(ys, oa, gates, h, mod3, w_br_attn[l].astype(BF16), w_out[l].astype(BF16),
                               row(g_norm_ffn[l]), w_router, b_router, seq, tm)

        wgu = jnp.concatenate([w_exp_gate[l], w_exp_up[l]], axis=-1).astype(BF16)
        h = _moe_final(u2, cw, wgu, w_exp_down[l].astype(BF16), h1, mod3, row(g_final),
                       seq, 1024, final_norm=(l == depth - 1))
    return h.reshape(nb, seq, d)
```

```python
import functools
import math

import jax
import jax.numpy as jnp
from jax import lax
from jax.experimental import pallas as pl
from jax.experimental.pallas import tpu as pltpu

EPS = 1e-6
CHUNK = 64
S5_GROUP = 16
S5_STATE = 64
S5_STEP = 16
S5_TILE = 2048
DA_HEADS = 8
DA_HEAD_DIM = 64
N_GROUPS = 4
EXP_PER_GROUP = 8
N_EXPERTS = N_GROUPS * EXP_PER_GROUP
D_EXPERT = 256
EXPERT_TILE = 512
ROUTE_ROWS = 8
LANES = 128
Q_SCALE = DA_HEAD_DIM ** -0.5 * math.log2(math.e)
VMEM_LIMIT = 56 * 1024 * 1024

F32 = jnp.float32
BF16 = jnp.bfloat16
HIGHEST = lax.Precision.HIGHEST


def _cparams(*sem):
    return pltpu.CompilerParams(dimension_semantics=sem, vmem_limit_bytes=VMEM_LIMIT)


def _dot(a, b):
    return jnp.dot(a, b, preferred_element_type=F32)


def _dot_nt(a, b):
    return lax.dot_general(a, b, (((1,), (1,)), ((), ())), preferred_element_type=F32)


def _rms(x, g):
    return x * lax.rsqrt(jnp.mean(x * x, axis=-1, keepdims=True) + EPS) * g


def _ada_kernel(c_ref, w_ref, b_ref, o_ref):
    c = c_ref[...]
    cs = c * jax.nn.sigmoid(c)
    o_ref[...] = jnp.dot(cs, w_ref[...], preferred_element_type=F32, precision=HIGHEST) + b_ref[...]


def _ada_mod(c, w, b):
    nb, d = c.shape
    n = w.shape[1]
    tn = n // 4
    return pl.pallas_call(
        _ada_kernel,
        out_shape=jax.ShapeDtypeStruct((nb, n), F32),
        grid=(n // tn,),
        in_specs=[pl.BlockSpec((nb, d), lambda j: (0, 0)),
                  pl.BlockSpec((d, tn), lambda j: (0, j)),
                  pl.BlockSpec((1, tn), lambda j: (0, j))],
        out_specs=pl.BlockSpec((nb, tn), lambda j: (0, j)),
        compiler_params=_cparams("parallel"),
        name="ada_mod",
    )(c, w, b)


def _modulated_norm(x_ref, mod_ref, g_ref):
    mod = mod_ref[0]
    return _rms(x_ref[...], g_ref[...]) * (1.0 + mod[1:2, :]) + mod[0:1, :]


def _inproj_kernel(x_ref, mod_ref, g_ref, w_ref, b_ref, q_ref, k_ref, v_ref, gt_ref, *, widths):
    ub = _modulated_norm(x_ref, mod_ref, g_ref).astype(BF16)
    qw, kw, vw, gw = widths
    o = 0
    q_ref[...] = ((_dot(ub, w_ref[:, o:o + qw]) + b_ref[:, o:o + qw]) * Q_SCALE).astype(BF16)
    o += qw
    k_ref[...] = (_dot(ub, w_ref[:, o:o + kw]) + b_ref[:, o:o + kw]).astype(BF16)
    o += kw
    v_ref[...] = (_dot(ub, w_ref[:, o:o + vw]) + b_ref[:, o:o + vw]).astype(BF16)
    o += vw
    gt_ref[...] = jax.nn.sigmoid(_dot(ub, w_ref[:, o:o + gw]) + b_ref[:, o:o + gw]).astype(BF16)


def _in_proj(x2, mod3, g, w, b, seq, widths, tm):
    t, d = x2.shape
    tpb = seq // tm
    row = lambda i: (i, 0)
    full = lambda a: pl.BlockSpec(a.shape, lambda i: (0, 0))
    return pl.pallas_call(
        functools.partial(_inproj_kernel, widths=widths),
        out_shape=[jax.ShapeDtypeStruct((t, wd), BF16) for wd in widths],
        grid=(t // tm,),
        in_specs=[pl.BlockSpec((tm, d), row), pl.BlockSpec((1, 6, d), lambda i: (i // tpb, 0, 0)),
                  full(g), full(w), full(b)],
        out_specs=[pl.BlockSpec((tm, wd), row) for wd in widths],
        compiler_params=_cparams("parallel"),
        name="in_proj",
    )(x2, mod3, g, w, b)


def _s5_in_kernel(x_ref, mod_ref, g_ref, wst_ref, bs_ref, ut_ref, slab_ref):
    u = _modulated_norm(x_ref, mod_ref, g_ref)
    nslab = slab_ref.shape[0]
    nk = x_ref.shape[0] // S5_STEP
    for j in range(nslab):
        slab_ref[j] = u[:, j * LANES:(j + 1) * LANES]
    for s in range(S5_STEP):
        us = jnp.concatenate([slab_ref[j, pl.ds(s, nk, stride=S5_STEP), :] for j in range(nslab)], axis=1)
        ut_ref[s] = (_dot_nt(wst_ref[...], us.astype(BF16)) + bs_ref[...]).astype(BF16)


def _s5_in(x2, mod3, g, wst, bs, seq, tm):
    t, d = x2.shape
    s5w = wst.shape[0]
    nkt = tm // S5_STEP
    tpb = seq // tm
    full = lambda a: pl.BlockSpec(a.shape, lambda i: (0, 0))
    return pl.pallas_call(
        _s5_in_kernel,
        out_shape=jax.ShapeDtypeStruct((S5_STEP, s5w, t // S5_STEP), BF16),
        grid=(t // tm,),
        in_specs=[pl.BlockSpec((tm, d), lambda i: (i, 0)), pl.BlockSpec((1, 6, d), lambda i: (i // tpb, 0, 0)),
                  full(g), full(wst), full(bs)],
        out_specs=pl.BlockSpec((S5_STEP, s5w, nkt), lambda i: (0, 0, i)),
        scratch_shapes=[pltpu.VMEM((d // LANES, tm, LANES), F32)],
        compiler_params=_cparams("parallel"),
        name="s5_in",
    )(x2, mod3, g, wst, bs)


def _s5_out_kernel(yt_ref, wglut_ref, bglu_ref, wbs_ref, o_ref, slab_ref):
    nslab = slab_ref.shape[0]
    nk = yt_ref.shape[2]
    for s in range(S5_STEP):
        yt = yt_ref[s]
        zt = _dot(wglut_ref[...], yt) + bglu_ref[...]
        yg = (yt.astype(F32) * jax.nn.sigmoid(zt)).T
        for j in range(nslab):
            slab_ref[j, pl.ds(s, nk, stride=S5_STEP), :] = yg[:, j * LANES:(j + 1) * LANES]
    yg = jnp.concatenate([slab_ref[j] for j in range(nslab)], axis=1)
    o_ref[...] = _dot(yg.astype(BF16), wbs_ref[...]).astype(BF16)


def _s5_out(y_t, wglut, bglu, wbs, tm):
    n, s5w, r = y_t.shape
    d = wbs.shape[1]
    nkt = tm // n
    full = lambda a: pl.BlockSpec(a.shape, lambda i: (0, 0))
    return pl.pallas_call(
        _s5_out_kernel,
        out_shape=jax.ShapeDtypeStruct((r * n, d), BF16),
        grid=(r // nkt,),
        in_specs=[pl.BlockSpec((n, s5w, nkt), lambda i: (0, 0, i)), full(wglut), full(bglu), full(wbs)],
        out_specs=pl.BlockSpec((tm, d), lambda i: (i, 0)),
        scratch_shapes=[pltpu.VMEM((s5w // LANES, tm, LANES), F32)],
        compiler_params=_cparams("parallel"),
        name="s5_out",
    )(y_t, wglut, bglu, wbs)


def _s5_prep_kernel(ar_ref, ai_ref, dt_ref, br_ref, bi_ref, crt_ref, cit_ref, cr_ref, ci_ref,
                    er_ref, ei_ref, vr_ref, vi_ref, k_ref, a16r_ref, a16i_ref):
    lr = ar_ref[0]
    li = ai_ref[0]
    dt = jnp.exp(dt_ref[0])
    mag = jnp.exp(lr * dt)
    abr = mag * jnp.cos(li * dt)
    abi = mag * jnp.sin(li * dt)
    den = lr * lr + li * li
    nr = abr - 1.0
    f_re = (nr * lr + abi * li) / den
    f_im = (abi * lr - nr * li) / den
    b_re = br_ref[0]
    b_im = bi_ref[0]
    bb_re = f_re * b_re - f_im * b_im
    bb_im = f_re * b_im + f_im * b_re
    crt = crt_ref[0]
    cit = cit_ref[0]
    cr = cr_ref[0]
    ci = ci_ref[0]
    pr = jnp.ones_like(lr)
    pi = jnp.zeros_like(lr)
    for j in range(S5_STEP):
        e_re = pr * bb_re - pi * bb_im
        e_im = pr * bb_im + pi * bb_re
        er_ref[0, j] = e_re
        ei_ref[0, j] = e_im
        k_ref[0, j] = (jnp.dot(cr, e_re, preferred_element_type=F32, precision=HIGHEST)
                       - jnp.dot(ci, e_im, preferred_element_type=F32, precision=HIGHEST))
        pr, pi = pr * abr - pi * abi, pr * abi + pi * abr
        vr_ref[0, j] = pr * crt - pi * cit
        vi_ref[0, j] = pr * cit + pi * crt
    a16r_ref[0] = pr
    a16i_ref[0] = pi


def _s5_prep(a_re, a_im, log_dt, b_re, b_im, c_re, c_im):
    ng, p = a_re.shape
    gc = b_re.shape[-1]
    col = lambda a: a.reshape(ng, p, 1)
    g3 = lambda g: (g, 0, 0)
    g4 = lambda g: (g, 0, 0, 0)
    spec_p1 = pl.BlockSpec((1, p, 1), g3)
    spec_pc = pl.BlockSpec((1, p, gc), g3)
    spec_cp = pl.BlockSpec((1, gc, p), g3)
    spec_jpc = pl.BlockSpec((1, S5_STEP, p, gc), g4)
    shp_jpc = jax.ShapeDtypeStruct((ng, S5_STEP, p, gc), F32)
    return pl.pallas_call(
        _s5_prep_kernel,
        out_shape=[shp_jpc, shp_jpc, shp_jpc, shp_jpc,
                   jax.ShapeDtypeStruct((ng, S5_STEP, gc, gc), F32),
                   jax.ShapeDtypeStruct((ng, p, 1), F32),
                   jax.ShapeDtypeStruct((ng, p, 1), F32)],
        grid=(ng,),
        in_specs=[spec_p1, spec_p1, pl.BlockSpec((1, 1, 1), g3), spec_pc, spec_pc,
                  spec_pc, spec_pc, spec_cp, spec_cp],
        out_specs=[spec_jpc, spec_jpc, spec_jpc, spec_jpc,
                   pl.BlockSpec((1, S5_STEP, gc, gc), g4), spec_p1, spec_p1],
        compiler_params=_cparams("parallel"),
        name="s5_prep",
    )(col(a_re), col(a_im), log_dt.reshape(ng, 1, 1), b_re, b_im,
      jnp.swapaxes(c_re, 1, 2), jnp.swapaxes(c_im, 1, 2), c_re, c_im)


def _s5_operators(e_re, e_im, v_re, v_im, kk, a16r, a16i, d_skip):
    ng, n, p, gc = e_re.shape
    na = ng // 2
    eye = jnp.eye(2, dtype=F32)
    idx = jnp.arange(n)
    lag = idx[:, None] - idx[None, :]
    toe = jnp.take(kk, jnp.clip(lag, 0, n - 1), axis=1)
    toe = jnp.where((lag >= 0)[None, :, :, None, None], toe, 0.0)
    mt = jnp.einsum('agtsoi,hg->athosgi', toe.reshape(na, 2, n, n, gc, gc), eye)
    mt = mt.reshape(na, 2 * n * gc, 2 * n * gc)

    def to_w(e):
        w = jnp.einsum('agspi,hg->ahpsgi', e[:, ::-1].reshape(na, 2, n, p, gc), eye)
        return w.reshape(na, 2 * p, 2 * n * gc)

    def to_v(v):
        w = jnp.einsum('agtpo,hg->athogp', v.reshape(na, 2, n, p, gc), eye)
        return w.reshape(na, 2 * n * gc, 2 * p)

    pair_row = lambda a: a.reshape(na, 1, 2 * p)
    d_col = jnp.broadcast_to(d_skip.reshape(na, 1, 2, gc), (na, n, 2, gc)).reshape(na, 2 * n * gc, 1)
    return (mt.astype(BF16), to_w(e_re).astype(BF16), to_w(e_im).astype(BF16), to_v(v_re).astype(BF16),
            to_v(-v_im).astype(BF16), pair_row(a16r), pair_row(a16i), d_col)


def _s5_core_kernel(u_ref, m_ref, wr_ref, wi_ref, vr_ref, vi_ref, ar_ref, ai_ref, d_ref, o_ref,
                    sr_ref, si_ref, xr_ref, xi_ref, *, nb, nk):
    n, gw, r = u_ref.shape
    u = u_ref[...].reshape(n * gw, r)
    sr_ref[...] = _dot(wr_ref[0], u).T
    si_ref[...] = _dot(wi_ref[0], u).T
    a_r = ar_ref[0]
    a_i = ai_ref[0]

    def step(k, carry):
        out = []
        for b in range(nb):
            x_r, x_i = carry[2 * b], carry[2 * b + 1]
            row = pl.ds(b * nk + k, 1)
            xr_ref[row, :] = x_r
            xi_ref[row, :] = x_i
            out.append(a_r * x_r - a_i * x_i + sr_ref[row, :])
            out.append(a_r * x_i + a_i * x_r + si_ref[row, :])
        return tuple(out)

    zero = jnp.zeros((1, xr_ref.shape[1]), F32)
    lax.fori_loop(0, nk, step, (zero,) * (2 * nb), unroll=4)
    y = (_dot(m_ref[0], u) + _dot(vr_ref[0], xr_ref[...].T.astype(BF16))
         + _dot(vi_ref[0], xi_ref[...].T.astype(BF16)))
    y = y + d_ref[0] * u.astype(F32)
    o_ref[...] = jax.nn.gelu(y).astype(BF16).reshape(n, gw, r)


def _s5_core(u_t, ops, nb, nk):
    m, wr, wi, vr, vi, ar, ai, d_col = ops
    npair, w2, _ = m.shape
    p2 = wr.shape[1]
    n, _, r = u_t.shape
    gw = w2 // n
    g3 = lambda g: (g, 0, 0)
    data = pl.BlockSpec((n, gw, r), lambda g: (0, g, 0))
    return pl.pallas_call(
        functools.partial(_s5_core_kernel, nb=nb, nk=nk),
        out_shape=jax.ShapeDtypeStruct(u_t.shape, BF16),
        grid=(npair,),
        in_specs=[data, pl.BlockSpec((1, w2, w2), g3),
                  pl.BlockSpec((1, p2, w2), g3), pl.BlockSpec((1, p2, w2), g3),
                  pl.BlockSpec((1, w2, p2), g3), pl.BlockSpec((1, w2, p2), g3),
                  pl.BlockSpec((1, 1, p2), g3), pl.BlockSpec((1, 1, p2), g3),
                  pl.BlockSpec((1, w2, 1), g3)],
        out_specs=data,
        scratch_shapes=[pltpu.VMEM((r, p2), F32)] * 4,
        compiler_params=_cparams("parallel"),
        name="s5_core",
    )(u_t, m, wr, wi, vr, vi, ar, ai, d_col)


def _attn_kernel(q_ref, k_ref, v_ref, lq1_ref, lk1_ref, lq2_ref, lk2_ref, g_ref, o_ref,
                 m_ref, l_ref, acc_ref, *, bq, lambda_init):
    qi = pl.program_id(2)
    q = q_ref[0]
    lane = lax.broadcasted_iota(jnp.int32, (1, 2 * DA_HEAD_DIM), 1)
    zero = jnp.zeros_like(q)
    qs = jnp.concatenate([jnp.where(lane < DA_HEAD_DIM, q, zero),
                          jnp.where(lane >= DA_HEAD_DIM, q, zero)], axis=0)
    m_ref[...] = jnp.full(m_ref.shape, -jnp.inf, F32)
    l_ref[...] = jnp.zeros(l_ref.shape, F32)
    acc_ref[...] = jnp.zeros(acc_ref.shape, F32)
    reps = bq // LANES

    def block(j, mask):
        ks = pl.ds(pl.multiple_of(j * bq, bq), bq)
        s = _dot_nt(qs, k_ref[0, ks, :])
        if mask is not None:
            s = jnp.where(mask, s, -jnp.inf)
        m_old = m_ref[...]
        m_new = jnp.maximum(m_old, jnp.max(s, axis=-1, keepdims=True))
        alpha = jnp.exp2(m_old - m_new)
        p = jnp.exp2(s - jnp.concatenate([m_new] * reps, axis=1))
        l_ref[...] = alpha * l_ref[...] + jnp.sum(p, axis=-1, keepdims=True)
        acc_ref[...] = alpha * acc_ref[...] + _dot(p.astype(BF16), v_ref[0, ks, :])
        m_ref[...] = m_new

    def full_block(j, carry):
        block(j, None)
        return carry

    lax.fori_loop(0, qi, full_block, 0)
    r_chunk = (lax.broadcasted_iota(jnp.int32, (2 * bq, bq), 0) % bq) // CHUNK
    c_chunk = lax.broadcasted_iota(jnp.int32, (2 * bq, bq), 1) // CHUNK
    block(qi, c_chunk <= r_chunk)

    lam = (jnp.exp(jnp.sum(lq1_ref[...] * lk1_ref[...], axis=-1, keepdims=True))
           - jnp.exp(jnp.sum(lq2_ref[...] * lk2_ref[...], axis=-1, keepdims=True)) + lambda_init)
    o_all = acc_ref[...] / l_ref[...]
    o = o_all[:bq] - lam * o_all[bq:]
    o_ref[0] = (_rms(o, g_ref[...]) * (1.0 - lambda_init)).astype(BF16)


def _diff_attn(q3, k3, v3, lq1, lk1, lq2, lk2, g_subln, lambda_init, bq):
    nb, seq, _ = q3.shape
    hw = 2 * DA_HEAD_DIM
    lam_spec = pl.BlockSpec((1, DA_HEAD_DIM), lambda b, h, i: (0, 0))
    kv_spec = pl.BlockSpec((1, seq, hw), lambda b, h, i: (b, 0, h))
    return pl.pallas_call(
        functools.partial(_attn_kernel, bq=bq, lambda_init=lambda_init),
        out_shape=jax.ShapeDtypeStruct(q3.shape, BF16),
        grid=(nb, DA_HEADS, seq // bq),
        in_specs=[pl.BlockSpec((1, bq, hw), lambda b, h, i: (b, i, h)), kv_spec, kv_spec,
                  lam_spec, lam_spec, lam_spec, lam_spec,
                  pl.BlockSpec((1, hw), lambda b, h, i: (0, 0))],
        out_specs=pl.BlockSpec((1, bq, hw), lambda b, h, i: (b, i, h)),
        scratch_shapes=[pltpu.VMEM((2 * bq, hw), F32)] * 3,
        compiler_params=_cparams("parallel", "parallel", "parallel"),
        name="diff_attn",
    )(q3, k3, v3, lq1, lk1, lq2, lk2, g_subln)


def _post_kernel(ys_ref, oa_ref, gt_ref, x_ref, mod_ref, wba_ref, wout_ref, g_ref, wr_ref, br_ref,
                 h_ref, u2_ref, z_ref, zt_ref, cnt_ref, base_ref):
    d = x_ref.shape[1]

    @pl.when(pl.program_id(0) == 0)
    def _():
        base_ref[...] = jnp.zeros(base_ref.shape, F32)

    mod = mod_ref[0]
    y_att = _dot(oa_ref[...], wba_ref[...])
    gates = gt_ref[...].astype(F32)
    mix_in = gates[:, :d] * ys_ref[...].astype(F32) + gates[:, d:] * y_att
    mix = _dot(mix_in.astype(BF16), wout_ref[...])
    h = x_ref[...] + mod[2:3, :] * mix
    h_ref[...] = h
    u2 = _rms(h, g_ref[...]) * (1.0 + mod[4:5, :]) + mod[3:4, :]
    u2_ref[...] = u2

    logits = jnp.dot(u2, wr_ref[...], preferred_element_type=F32, precision=HIGHEST) + br_ref[...]
    lane = lax.broadcasted_iota(jnp.int32, logits.shape, 1)
    neg = jnp.full_like(logits, -jnp.inf)
    big = jnp.full_like(lane, LANES)
    is_grp = (lane >= N_EXPERTS) & (lane < N_EXPERTS + N_GROUPS)
    glog = jnp.where(is_grp, logits, neg)
    gmax = jnp.max(glog, axis=-1, keepdims=True)
    gp = 1.0 / jnp.sum(jnp.exp(glog - gmax), axis=-1, keepdims=True)
    gi = jnp.min(jnp.where(glog == gmax, lane, big), axis=-1, keepdims=True) - N_EXPERTS
    in_grp = (lane < N_EXPERTS) & ((lane // EXP_PER_GROUP) == gi)
    elog = jnp.where(in_grp, logits, neg)
    v1 = jnp.max(elog, axis=-1, keepdims=True)
    i1 = jnp.min(jnp.where(elog == v1, lane, big), axis=-1, keepdims=True)
    elog2 = jnp.where(lane == i1, neg, elog)
    v2 = jnp.max(elog2, axis=-1, keepdims=True)
    i2 = jnp.min(jnp.where(elog2 == v2, lane, big), axis=-1, keepdims=True)
    e2 = jnp.exp(v2 - v1)
    w1 = gp / (1.0 + e2)
    w2 = gp * e2 / (1.0 + e2)
    tm = logits.shape[0]
    hot = jnp.where((lane == i1) | (lane == i2), 1.0, 0.0)
    earlier = (lax.broadcasted_iota(jnp.int32, (tm, tm), 1) < lax.broadcasted_iota(jnp.int32, (tm, tm), 0))
    before = _dot(jnp.where(earlier, 1.0, 0.0).astype(BF16), hot.astype(BF16)) + base_ref[...]
    rank1 = jnp.sum(jnp.where(lane == i1, before, 0.0), axis=-1, keepdims=True)
    rank2 = jnp.sum(jnp.where(lane == i2, before, 0.0), axis=-1, keepdims=True)
    base_ref[...] += jnp.sum(hot, axis=0, keepdims=True)
    cnt_ref[...] = base_ref[...]
    z = jnp.zeros_like(logits)
    for k, val in enumerate((i1.astype(F32), i2.astype(F32), rank1, rank2, w1, w2)):
        z = jnp.where(lane == k, val, z)
    z_ref[...] = z
    zt_ref[...] = z.T[:ROUTE_ROWS, :]


def _post_mix(ys, oa, gates, x2, mod3, wba, wout, g, wr, br, seq, tm):
    t, d = x2.shape
    tpb = seq // tm
    row = lambda i: (i, 0)
    full = lambda a: pl.BlockSpec(a.shape, lambda i: (0, 0))
    return pl.pallas_call(
        _post_kernel,
        out_shape=[jax.ShapeDtypeStruct((t, d), F32), jax.ShapeDtypeStruct((t, d), F32),
                   jax.ShapeDtypeStruct((t, LANES), F32), jax.ShapeDtypeStruct((ROUTE_ROWS, t), F32),
                   jax.ShapeDtypeStruct((1, LANES), F32)],
        grid=(t // tm,),
        in_specs=[pl.BlockSpec((tm, d), row), pl.BlockSpec((tm, oa.shape[1]), row),
                  pl.BlockSpec((tm, 2 * d), row), pl.BlockSpec((tm, d), row),
                  pl.BlockSpec((1, 6, d), lambda i: (i // tpb, 0, 0)),
                  full(wba), full(wout), full(g), full(wr), full(br)],
        out_specs=[pl.BlockSpec((tm, d), row), pl.BlockSpec((tm, d), row), pl.BlockSpec((tm, LANES), row),
                   pl.BlockSpec((ROUTE_ROWS, tm), lambda i: (0, i)), pl.BlockSpec((1, LANES), lambda i: (0, 0))],
        scratch_shapes=[pltpu.VMEM((1, LANES), F32)],
        compiler_params=_cparams("arbitrary"),
        name="post_mix",
    )(ys, oa, gates, x2, mod3, wba, wout, g, wr, br)


def _row_copy(src_ref, src_row, dst_ref, dst_row, sem):
    return pltpu.make_async_copy(src_ref.at[pl.ds(src_row, 1), :], dst_ref.at[pl.ds(dst_row, 1), :], sem)


def _dispatch_kernel(pad_lo_ref, pad_hi_ref, n_used_ref, pos_ref, u_ref, xs_ref, zero_ref, sem, pad_sem):
    tm = u_ref.shape[0]
    tile = zero_ref.shape[0]
    n_tiles = xs_ref.shape[0] // tile

    def send(r, carry):
        _row_copy(u_ref, r, xs_ref, pos_ref[r], sem).start()
        _row_copy(u_ref, r, xs_ref, pos_ref[tm + r], sem).start()
        return carry

    lax.fori_loop(0, tm, send, 0, unroll=8)

    @pl.when(pl.program_id(0) == pl.num_programs(0) - 1)
    def _():
        zero_ref[...] = jnp.zeros(zero_ref.shape, F32)

        def pad_copies(e, wait):
            lo = pad_lo_ref[e]
            n = pad_hi_ref[e] - lo
            for b in range(tile.bit_length() - 1):
                size = 1 << b

                @pl.when((n >> b) & 1 == 1)
                def _():
                    row = pl.multiple_of(lo + (n & (size - 1)), min(size, 8))
                    cp = pltpu.make_async_copy(zero_ref.at[pl.ds(0, size), :],
                                               xs_ref.at[pl.ds(row, size), :], pad_sem)
                    cp.wait() if wait else cp.start()

        def tile_copy(i):
            return pltpu.make_async_copy(zero_ref, xs_ref.at[pl.ds(i * tile, tile), :], pad_sem)

        for wait in (False, True):
            @pl.loop(0, N_EXPERTS)
            def _(e):
                pad_copies(e, wait)

            @pl.loop(n_used_ref[0], n_tiles)
            def _(i):
                tile_copy(i).wait() if wait else tile_copy(i).start()

    for _ in range(2):
        pltpu.make_async_copy(u_ref, xs_ref.at[pl.ds(0, tm), :], sem).wait()


def _dispatch(pad_lo, pad_hi, n_used, pos, u2, n_rows, tm, tile):
    t, d = u2.shape
    return pl.pallas_call(
        _dispatch_kernel,
        out_shape=jax.ShapeDtypeStruct((n_rows, d), F32),
        grid_spec=pltpu.PrefetchScalarGridSpec(
            num_scalar_prefetch=3,
            grid=(t // tm,),
            in_specs=[pl.BlockSpec((2 * tm,), lambda i, *_: (i,), memory_space=pltpu.SMEM),
                      pl.BlockSpec((tm, d), lambda i, *_: (i, 0))],
            out_specs=pl.BlockSpec(memory_space=pl.ANY),
            scratch_shapes=[pltpu.VMEM((tile, d), F32), pltpu.SemaphoreType.DMA, pltpu.SemaphoreType.DMA]),
        compiler_params=_cparams("arbitrary"),
        name="moe_dispatch",
    )(pad_lo, pad_hi, n_used, pos, u2)


def _expert_kernel(tile_e_ref, n_used_ref, x_ref, wgu_ref, wd_ref, y_ref):
    used = pl.program_id(0) < n_used_ref[0]

    @pl.when(used)
    def _():
        gu = _dot(x_ref[...].astype(BF16), wgu_ref[0])
        gate = gu[:, :D_EXPERT]
        hdn = gate * jax.nn.sigmoid(gate) * gu[:, D_EXPERT:]
        y_ref[...] = _dot(hdn.astype(BF16), wd_ref[0])

    @pl.when(jnp.logical_not(used))
    def _():
        y_ref[...] = jnp.zeros(y_ref.shape, F32)


def _experts(tile_e, n_used, xs, wgu, wd, tm):
    n_rows, d = xs.shape
    return pl.pallas_call(
        _expert_kernel,
        out_shape=jax.ShapeDtypeStruct((n_rows, d), F32),
        grid_spec=pltpu.PrefetchScalarGridSpec(
            num_scalar_prefetch=2,
            grid=(n_rows // tm,),
            in_specs=[pl.BlockSpec((tm, d), lambda i, te, nu: (jnp.minimum(i, nu[0] - 1), 0)),
                      pl.BlockSpec((1, d, 2 * D_EXPERT), lambda i, te, nu: (te[i], 0, 0)),
                      pl.BlockSpec((1, D_EXPERT, d), lambda i, te, nu: (te[i], 0, 0))],
            out_specs=pl.BlockSpec((tm, d), lambda i, te, nu: (i, 0))),
        compiler_params=_cparams("arbitrary"),
        name="moe_experts",
    )(tile_e, n_used, xs, wgu, wd)


def _combine_kernel(pos_ref, ys_ref, z_ref, h_ref, mod_ref, g_ref, o_ref, buf_ref, sem, *, final_norm):
    tm = h_ref.shape[0]

    def fetch(r, carry):
        _row_copy(ys_ref, pos_ref[r], buf_ref.at[0], r, sem).start()
        _row_copy(ys_ref, pos_ref[tm + r], buf_ref.at[1], r, sem).start()
        return carry

    lax.fori_loop(0, tm, fetch, 0, unroll=8)
    for k in range(2):
        pltpu.make_async_copy(ys_ref.at[pl.ds(0, tm), :], buf_ref.at[k], sem).wait()
    z = z_ref[...]
    ffn = z[:, 4:5] * buf_ref[0] + z[:, 5:6] * buf_ref[1]
    h = h_ref[...] + mod_ref[0][5:6, :] * ffn
    o_ref[...] = _rms(h, g_ref[...]) if final_norm else h


def _combine(pos, ys, z, h1, mod3, g_final, seq, tm, final_norm):
    t, d = h1.shape
    tpb = seq // tm
    row = lambda i: (i, 0)
    return pl.pallas_call(
        functools.partial(_combine_kernel, final_norm=final_norm),
        out_shape=jax.ShapeDtypeStruct((t, d), F32),
        grid=(t // tm,),
        in_specs=[pl.BlockSpec((2 * tm,), lambda i: (i,), memory_space=pltpu.SMEM),
                  pl.BlockSpec(memory_space=pl.ANY),
                  pl.BlockSpec((tm, LANES), row), pl.BlockSpec((tm, d), row),
                  pl.BlockSpec((1, 6, d), lambda i: (i // tpb, 0, 0)),
                  pl.BlockSpec((1, d), lambda i: (0, 0))],
        out_specs=pl.BlockSpec((tm, d), row),
        scratch_shapes=[pltpu.VMEM((2, tm, d), F32), pltpu.SemaphoreType.DMA],
        compiler_params=_cparams("arbitrary"),
        name="moe_combine",
    )(pos, ys, z, h1, mod3, g_final)


def _route_tables(cnt, zt, n_tiles, tm_tok, tile):
    i32 = jnp.int32
    counts = cnt[0, :N_EXPERTS].astype(i32)
    padded = (counts + tile - 1) // tile * tile
    ends = jnp.cumsum(padded)
    starts = ends - padded
    n_used = (ends[-1] // tile).reshape(1)
    tile_row = jnp.minimum(jnp.arange(n_tiles, dtype=i32), n_used - 1) * tile
    tile_e = jnp.minimum(jnp.searchsorted(ends, tile_row, side='right'), N_EXPERTS - 1).astype(i32)
    pos1 = jnp.take(starts, zt[0].astype(i32)) + zt[2].astype(i32)
    pos2 = jnp.take(starts, zt[1].astype(i32)) + zt[3].astype(i32)
    pos = jnp.stack([pos1.reshape(-1, tm_tok), pos2.reshape(-1, tm_tok)], axis=1).reshape(-1)
    return starts + counts, ends, pos, tile_e, n_used.astype(i32)


def kernel(x, c, w_ada, b_ada, g_norm_mix, w_in, b_in, s5_a_re, s5_a_im, s5_b_re, s5_b_im, s5_c_re, s5_c_im, s5_d, s5_log_dt, w_glu, b_glu, lambda_q1, lambda_k1, lambda_q2, lambda_k2, g_subln, w_br_ssm, w_br_attn, w_out, g_norm_ffn, w_router_grp, b_router_grp, w_router_exp, b_router_exp, w_exp_gate, w_exp_up, w_exp_down, g_final):
    nb, seq, d = x.shape
    depth = w_ada.shape[0]
    s5w = s5_d.shape[1]
    daw = w_br_attn.shape[1]
    ng = s5w // S5_GROUP
    nk = seq // S5_STEP
    assert seq % S5_TILE == 0 and ng % 2 == 0
    widths = (daw, daw, daw, 2 * d)
    tm = 512
    bq = 512
    row = lambda a: a.reshape(1, -1)
    col = lambda a: a.reshape(-1, 1)

    h = x.reshape(nb * seq, d)
    for l in range(depth):
        lambda_init = 0.8 - 0.6 * math.exp(-0.3 * l)
        mod3 = _ada_mod(c, w_ada[l], row(b_ada[l])).reshape(nb, 6, d)

        w_in_b = w_in[l].astype(BF16)
        q, k, v, gates = _in_proj(h, mod3, row(g_norm_mix[l]), w_in_b[:, s5w:], row(b_in[l][s5w:]),
                                  seq, widths, tm)

        u_t = _s5_in(h, mod3, row(g_norm_mix[l]), w_in_b[:, :s5w].T, col(b_in[l][:s5w]), seq, S5_TILE)
        prep = _s5_prep(s5_a_re[l], s5_a_im[l], s5_log_dt[l], s5_b_re[l], s5_b_im[l], s5_c_re[l], s5_c_im[l])
        y_t = _s5_core(u_t, _s5_operators(*prep, s5_d[l]), nb, nk)
        ys = _s5_out(y_t, w_glu[l].T.astype(BF16), col(b_glu[l]), w_br_ssm[l].astype(BF16), S5_TILE)

        as3 = lambda a: a.reshape(nb, seq, daw)
        oa = _diff_attn(as3(q), as3(k), as3(v), row(lambda_q1[l]), row(lambda_k1[l]), row(lambda_q2[l]),
                        row(lambda_k2[l]), row(g_subln[l]), lambda_init, bq).reshape(nb * seq, daw)

        w_router = jnp.zeros((d, LANES), F32)
        w_router = w_router.at[:, :N_EXPERTS].set(w_router_exp[l])
        w_router = w_router.at[:, N_EXPERTS:N_EXPERTS + N_GROUPS].set(w_router_grp[l])
        b_router = jnp.zeros((1, LANES), F32)
        b_router = b_router.at[0, :N_EXPERTS].set(b_router_exp[l])
        b_router = b_router.at[0, N_EXPERTS:N_EXPERTS + N_GROUPS].set(b_router_grp[l])
        h1, u2, z, zt, cnt = _post_mix(ys, oa, gates, h, mod3, w_br_attn[l].astype(BF16),
                                       w_out[l].astype(BF16), row(g_norm_ffn[l]), w_router, b_router, seq, tm)

        n_tiles = (2 * nb * seq) // EXPERT_TILE + N_EXPERTS
        pad_lo, pad_hi, pos, tile_e, n_used = _route_tables(cnt, zt, n_tiles, tm, EXPERT_TILE)
        xs = _dispatch(pad_lo, pad_hi, n_used, pos, u2, n_tiles * EXPERT_TILE, tm, EXPERT_TILE)
        wgu = jnp.concatenate([w_exp_gate[l], w_exp_up[l]], axis=-1).astype(BF16)
        ys_e = _experts(tile_e, n_used, xs, wgu, w_exp_down[l].astype(BF16), EXPERT_TILE)
        h = _combine(pos, ys_e, z, h1, mod3, row(g_final), seq, tm, final_norm=(l == depth - 1))
    return h.reshape(nb, seq, d)
```

```python
import functools
import math

import jax
import jax.numpy as jnp
from jax import lax
from jax.experimental import pallas as pl
from jax.experimental.pallas import tpu as pltpu

EPS = 1e-6
CHUNK = 64
S5_GROUP = 16
S5_STATE = 64
S5_STEP = 16
S5_TILE = 2048
DA_HEADS = 8
DA_HEAD_DIM = 64
N_GROUPS = 4
EXP_PER_GROUP = 8
N_EXPERTS = N_GROUPS * EXP_PER_GROUP
D_EXPERT = 256
EXPERT_TILE = 512
ROUTE_ROWS = 8
LANES = 128
Q_SCALE = DA_HEAD_DIM ** -0.5 * math.log2(math.e)
VMEM_LIMIT = 56 * 1024 * 1024

F32 = jnp.float32
BF16 = jnp.bfloat16
HIGHEST = lax.Precision.HIGHEST


def _cparams(*sem):
    return pltpu.CompilerParams(dimension_semantics=sem, vmem_limit_bytes=VMEM_LIMIT)


def _dot(a, b):
    return jnp.dot(a, b, preferred_element_type=F32)


def _dot_nt(a, b):
    return lax.dot_general(a, b, (((1,), (1,)), ((), ())), preferred_element_type=F32)


def _rms(x, g):
    return x * lax.rsqrt(jnp.mean(x * x, axis=-1, keepdims=True) + EPS) * g


def _ada_kernel(c_ref, w_ref, b_ref, o_ref):
    c = c_ref[...]
    cs = c * jax.nn.sigmoid(c)
    o_ref[...] = jnp.dot(cs, w_ref[...], preferred_element_type=F32, precision=HIGHEST) + b_ref[...]


def _ada_mod(c, w, b):
    nb, d = c.shape
    n = w.shape[1]
    tn = n // 4
    return pl.pallas_call(
        _ada_kernel,
        out_shape=jax.ShapeDtypeStruct((nb, n), F32),
        grid=(n // tn,),
        in_specs=[pl.BlockSpec((nb, d), lambda j: (0, 0)),
                  pl.BlockSpec((d, tn), lambda j: (0, j)),
                  pl.BlockSpec((1, tn), lambda j: (0, j))],
        out_specs=pl.BlockSpec((nb, tn), lambda j: (0, j)),
        compiler_params=_cparams("parallel"),
        name="ada_mod",
    )(c, w, b)


def _modulated_norm(x_ref, mod_ref, g_ref):
    mod = mod_ref[0]
    return _rms(x_ref[...], g_ref[...]) * (1.0 + mod[1:2, :]) + mod[0:1, :]


def _inproj_kernel(x_ref, mod_ref, g_ref, w_ref, b_ref, q_ref, k_ref, v_ref, gt_ref, *, widths):
    ub = _modulated_norm(x_ref, mod_ref, g_ref).astype(BF16)
    qw, kw, vw, gw = widths
    o = 0
    q_ref[...] = ((_dot(ub, w_ref[:, o:o + qw]) + b_ref[:, o:o + qw]) * Q_SCALE).astype(BF16)
    o += qw
    k_ref[...] = (_dot(ub, w_ref[:, o:o + kw]) + b_ref[:, o:o + kw]).astype(BF16)
    o += kw
    v_ref[...] = (_dot(ub, w_ref[:, o:o + vw]) + b_ref[:, o:o + vw]).astype(BF16)
    o += vw
    gt_ref[...] = jax.nn.sigmoid(_dot(ub, w_ref[:, o:o + gw]) + b_ref[:, o:o + gw]).astype(BF16)


def _in_proj(x2, mod3, g, w, b, seq, widths, tm):
    t, d = x2.shape
    tpb = seq // tm
    row = lambda i: (i, 0)
    full = lambda a: pl.BlockSpec(a.shape, lambda i: (0, 0))
    return pl.pallas_call(
        functools.partial(_inproj_kernel, widths=widths),
        out_shape=[jax.ShapeDtypeStruct((t, wd), BF16) for wd in widths],
        grid=(t // tm,),
        in_specs=[pl.BlockSpec((tm, d), row), pl.BlockSpec((1, 6, d), lambda i: (i // tpb, 0, 0)),
                  full(g), full(w), full(b)],
        out_specs=[pl.BlockSpec((tm, wd), row) for wd in widths],
        compiler_params=_cparams("parallel"),
        name="in_proj",
    )(x2, mod3, g, w, b)


def _s5_in_kernel(x_ref, mod_ref, g_ref, wst_ref, bs_ref, ut_ref, slab_ref):
    u = _modulated_norm(x_ref, mod_ref, g_ref)
    nslab = slab_ref.shape[0]
    nk = x_ref.shape[0] // S5_STEP
    for j in range(nslab):
        slab_ref[j] = u[:, j * LANES:(j + 1) * LANES]
    for s in range(S5_STEP):
        us = jnp.concatenate([slab_ref[j, pl.ds(s, nk, stride=S5_STEP), :] for j in range(nslab)], axis=1)
        ut_ref[s] = (_dot_nt(wst_ref[...], us.astype(BF16)) + bs_ref[...]).astype(BF16)


def _s5_in(x2, mod3, g, wst, bs, seq, tm):
    t, d = x2.shape
    s5w = wst.shape[0]
    nkt = tm // S5_STEP
    tpb = seq // tm
    full = lambda a: pl.BlockSpec(a.shape, lambda i: (0, 0))
    return pl.pallas_call(
        _s5_in_kernel,
        out_shape=jax.ShapeDtypeStruct((S5_STEP, s5w, t // S5_STEP), BF16),
        grid=(t // tm,),
        in_specs=[pl.BlockSpec((tm, d), lambda i: (i, 0)), pl.BlockSpec((1, 6, d), lambda i: (i // tpb, 0, 0)),
                  full(g), full(wst), full(bs)],
        out_specs=pl.BlockSpec((S5_STEP, s5w, nkt), lambda i: (0, 0, i)),
        scratch_shapes=[pltpu.VMEM((d // LANES, tm, LANES), F32)],
        compiler_params=_cparams("parallel"),
        name="s5_in",
    )(x2, mod3, g, wst, bs)


def _s5_out_kernel(yt_ref, wglut_ref, bglu_ref, wbs_ref, o_ref, slab_ref):
    nslab = slab_ref.shape[0]
    nk = yt_ref.shape[2]
    for s in range(S5_STEP):
        yt = yt_ref[s]
        zt = _dot(wglut_ref[...], yt) + bglu_ref[...]
        yg = (yt.astype(F32) * jax.nn.sigmoid(zt)).T
        for j in range(nslab):
            slab_ref[j, pl.ds(s, nk, stride=S5_STEP), :] = yg[:, j * LANES:(j + 1) * LANES]
    yg = jnp.concatenate([slab_ref[j] for j in range(nslab)], axis=1)
    o_ref[...] = _dot(yg.astype(BF16), wbs_ref[...]).astype(BF16)


def _s5_out(y_t, wglut, bglu, wbs, tm):
    n, s5w, r = y_t.shape
    d = wbs.shape[1]
    nkt = tm // n
    full = lambda a: pl.BlockSpec(a.shape, lambda i: (0, 0))
    return pl.pallas_call(
        _s5_out_kernel,
        out_shape=jax.ShapeDtypeStruct((r * n, d), BF16),
        grid=(r // nkt,),
        in_specs=[pl.BlockSpec((n, s5w, nkt), lambda i: (0, 0, i)), full(wglut), full(bglu), full(wbs)],
        out_specs=pl.BlockSpec((tm, d), lambda i: (i, 0)),
        scratch_shapes=[pltpu.VMEM((s5w // LANES, tm, LANES), F32)],
        compiler_params=_cparams("parallel"),
        name="s5_out",
    )(y_t, wglut, bglu, wbs)


def _s5_prep_kernel(lrc_ref, lic_ref, dtc_ref, lrr_ref, lir_ref, dtr_ref, br_ref, bi_ref, cr_ref, ci_ref,
                    mt_ref, wtr_ref, wti_ref, vtr_ref, vti_ref, a16r_ref, a16i_ref):
    n, gc, ns = S5_STEP, S5_GROUP, S5_STATE
    w = n * gc
    iota = lambda shape, dim: lax.broadcasted_iota(jnp.int32, shape, dim)
    hdot = lambda a, b: jnp.dot(a, b, preferred_element_type=F32, precision=HIGHEST)

    def power(lr, li, dt, k):
        mag = jnp.exp(lr * dt * k)
        return mag * jnp.cos(li * dt * k), mag * jnp.sin(li * dt * k)

    lr, li, dt = lrc_ref[0], lic_ref[0], jnp.exp(dtc_ref[0])
    abr, abi = power(lr, li, dt, 1.0)
    den = lr * lr + li * li
    nr = abr - 1.0
    f_re = (nr * lr + abi * li) / den
    f_im = (abi * lr - nr * li) / den
    bb_re = f_re * br_ref[0] - f_im * bi_ref[0]
    bb_im = f_re * bi_ref[0] + f_im * br_ref[0]
    spread = jnp.where(iota((gc, 2 * w), 1) % gc == iota((gc, 2 * w), 0), 1.0, 0.0)
    bt_re, bt_im = hdot(bb_re, spread), hdot(bb_im, spread)
    k_in = (n - 1 - (iota((1, 2 * w), 1) % w) // gc).astype(F32)
    p_re, p_im = power(lr, li, dt, k_in)
    e_re = p_re * bt_re - p_im * bt_im
    e_im = p_re * bt_im + p_im * bt_re
    own = iota((2 * ns, 2 * w), 0) // ns == iota((2 * ns, 2 * w), 1) // w
    wtr_ref[0] = jnp.where(own, e_re, 0.0).astype(BF16)
    wti_ref[0] = jnp.where(own, e_im, 0.0).astype(BF16)

    lr_r, li_r, dt_r = lrr_ref[0], lir_ref[0], jnp.exp(dtr_ref[0])
    c_r, c_i = power(lr_r, li_r, dt_r, 1.0)
    for _ in range(n.bit_length() - 1):
        c_r, c_i = c_r * c_r - c_i * c_i, 2.0 * c_r * c_i
    a16r_ref[0], a16i_ref[0] = c_r, c_i
    k_out = ((iota((2 * w, 1), 0) % w) // gc + 1).astype(F32)
    q_re, q_im = power(lr_r, li_r, dt_r, k_out)
    c_re = jnp.concatenate([cr_ref[0]] * (2 * n), axis=0)
    c_im = jnp.concatenate([ci_ref[0]] * (2 * n), axis=0)
    own = iota((2 * w, 2 * ns), 0) // w == iota((2 * w, 2 * ns), 1) // ns
    vtr_ref[0] = jnp.where(own, c_re * q_re - c_im * q_im, 0.0).astype(BF16)
    vti_ref[0] = jnp.where(own, -(c_re * q_im + c_im * q_re), 0.0).astype(BF16)

    group_of_lane = iota((gc, 2 * ns), 1) // ns
    for h in range(2):
        mine = group_of_lane == h
        krev = (hdot(jnp.where(mine, cr_ref[0], 0.0), e_re[:, :w])
                - hdot(jnp.where(mine, ci_ref[0], 0.0), e_im[:, :w]))
        strip = jnp.concatenate([krev, jnp.zeros_like(krev)], axis=1)
        rows = []
        for t in range(n):
            shift = (n - 1 - t) * gc
            rows.append((pltpu.roll(strip, 2 * w - shift, 1) if shift else strip)[:, :w])
        mt_ref[0, h * w:(h + 1) * w, h * w:(h + 1) * w] = jnp.concatenate(rows, axis=0).astype(BF16)
        mt_ref[0, h * w:(h + 1) * w, (1 - h) * w:(2 - h) * w] = jnp.zeros((w, w), BF16)


def _s5_prep(a_re, a_im, log_dt, b_re, b_im, c_re, c_im, d_skip):
    ng, ns = a_re.shape
    gc = b_re.shape[-1]
    na = ng // 2
    w2 = 2 * S5_STEP * gc
    col = lambda a: a.reshape(na, 2 * ns, 1)
    row = lambda a: a.reshape(na, 1, 2 * ns)
    per_state = jnp.repeat(log_dt, ns)
    pair_c = lambda a: a.reshape(na, 2, gc, ns).transpose(0, 2, 1, 3).reshape(na, gc, 2 * ns)
    g3 = lambda g: (g, 0, 0)
    spec = lambda *shape: pl.BlockSpec((1,) + shape, g3)
    shp = lambda *shape, dt=BF16: jax.ShapeDtypeStruct((na,) + shape, dt)
    ops = pl.pallas_call(
        _s5_prep_kernel,
        out_shape=[shp(w2, w2), shp(2 * ns, w2), shp(2 * ns, w2), shp(w2, 2 * ns), shp(w2, 2 * ns),
                   shp(1, 2 * ns, dt=F32), shp(1, 2 * ns, dt=F32)],
        grid=(na,),
        in_specs=[spec(2 * ns, 1)] * 3 + [spec(1, 2 * ns)] * 3 + [spec(2 * ns, gc)] * 2 + [spec(gc, 2 * ns)] * 2,
        out_specs=[spec(w2, w2), spec(2 * ns, w2), spec(2 * ns, w2), spec(w2, 2 * ns), spec(w2, 2 * ns),
                   spec(1, 2 * ns), spec(1, 2 * ns)],
        compiler_params=_cparams("parallel"),
        name="s5_prep",
    )(col(a_re), col(a_im), col(per_state), row(a_re), row(a_im), row(per_state),
      b_re.reshape(na, 2 * ns, gc), b_im.reshape(na, 2 * ns, gc), pair_c(c_re), pair_c(c_im))
    d_col = jnp.broadcast_to(d_skip.reshape(na, 2, 1, gc), (na, 2, S5_STEP, gc)).reshape(na, w2, 1)
    return (*ops, d_col)


def _s5_core_kernel(u_ref, m_ref, wr_ref, wi_ref, vr_ref, vi_ref, ar_ref, ai_ref, d_ref, o_ref,
                    sr_ref, si_ref, xr_ref, xi_ref, *, nb, nk):
    n, gw, r = u_ref.shape
    gc = gw // 2
    u = jnp.concatenate([u_ref[:, h * gc:(h + 1) * gc, :].reshape(n * gc, r) for h in range(2)], axis=0)
    sr_ref[...] = _dot(wr_ref[0], u).T
    si_ref[...] = _dot(wi_ref[0], u).T
    a_r = ar_ref[0]
    a_i = ai_ref[0]

    def step(k, carry):
        out = []
        for b in range(nb):
            x_r, x_i = carry[2 * b], carry[2 * b + 1]
            row = pl.ds(b * nk + k, 1)
            xr_ref[row, :] = x_r
            xi_ref[row, :] = x_i
            out.append(a_r * x_r - a_i * x_i + sr_ref[row, :])
            out.append(a_r * x_i + a_i * x_r + si_ref[row, :])
        return tuple(out)

    zero = jnp.zeros((1, xr_ref.shape[1]), F32)
    lax.fori_loop(0, nk, step, (zero,) * (2 * nb), unroll=4)
    y = (_dot(m_ref[0], u) + _dot(vr_ref[0], xr_ref[...].T.astype(BF16))
         + _dot(vi_ref[0], xi_ref[...].T.astype(BF16)))
    y = y + d_ref[0] * u.astype(F32)
    y = jax.nn.gelu(y).astype(BF16)
    for h in range(2):
        o_ref[:, h * gc:(h + 1) * gc, :] = y[h * n * gc:(h + 1) * n * gc].reshape(n, gc, r)


def _s5_core(u_t, ops, nb, nk):
    m, wr, wi, vr, vi, ar, ai, d_col = ops
    npair, w2, _ = m.shape
    p2 = wr.shape[1]
    n, _, r = u_t.shape
    gw = w2 // n
    g3 = lambda g: (g, 0, 0)
    data = pl.BlockSpec((n, gw, r), lambda g: (0, g, 0))
    return pl.pallas_call(
        functools.partial(_s5_core_kernel, nb=nb, nk=nk),
        out_shape=jax.ShapeDtypeStruct(u_t.shape, BF16),
        grid=(npair,),
        in_specs=[data, pl.BlockSpec((1, w2, w2), g3),
                  pl.BlockSpec((1, p2, w2), g3), pl.BlockSpec((1, p2, w2), g3),
                  pl.BlockSpec((1, w2, p2), g3), pl.BlockSpec((1, w2, p2), g3),
                  pl.BlockSpec((1, 1, p2), g3), pl.BlockSpec((1, 1, p2), g3),
                  pl.BlockSpec((1, w2, 1), g3)],
        out_specs=data,
        scratch_shapes=[pltpu.VMEM((r, p2), F32)] * 4,
        compiler_params=_cparams("parallel"),
        name="s5_core",
    )(u_t, m, wr, wi, vr, vi, ar, ai, d_col)


def _attn_kernel(q_ref, k_ref, v_ref, lq1_ref, lk1_ref, lq2_ref, lk2_ref, g_ref, o_ref,
                 m_ref, l_ref, acc_ref, *, bq, lambda_init):
    qi = pl.program_id(2)
    q = q_ref[0]
    lane = lax.broadcasted_iota(jnp.int32, (1, 2 * DA_HEAD_DIM), 1)
    zero = jnp.zeros_like(q)
    qs = jnp.concatenate([jnp.where(lane < DA_HEAD_DIM, q, zero),
                          jnp.where(lane >= DA_HEAD_DIM, q, zero)], axis=0)
    m_ref[...] = jnp.full(m_ref.shape, -jnp.inf, F32)
    l_ref[...] = jnp.zeros(l_ref.shape, F32)
    acc_ref[...] = jnp.zeros(acc_ref.shape, F32)
    reps = bq // LANES

    def keys(j):
        return pl.ds(pl.multiple_of(j * bq, bq), bq)

    def scores(j):
        return _dot_nt(qs, k_ref[0, keys(j), :])

    def absorb(s, j, mask):
        if mask is not None:
            s = jnp.where(mask, s, -jnp.inf)
        m_old = m_ref[...]
        m_new = jnp.maximum(m_old, jnp.max(s, axis=-1, keepdims=True))
        alpha = jnp.exp2(m_old - m_new)
        p = jnp.exp2(s - jnp.concatenate([m_new] * reps, axis=1))
        l_ref[...] = alpha * l_ref[...] + jnp.sum(p, axis=-1, keepdims=True)
        acc_ref[...] = alpha * acc_ref[...] + _dot(p.astype(BF16), v_ref[0, keys(j), :])
        m_ref[...] = m_new

    def visible_block(j, carry):
        absorb(scores(j), j, None)
        return carry

    lax.fori_loop(0, qi, visible_block, 0)
    r_chunk = (lax.broadcasted_iota(jnp.int32, (2 * bq, bq), 0) % bq) // CHUNK
    c_chunk = lax.broadcasted_iota(jnp.int32, (2 * bq, bq), 1) // CHUNK
    absorb(scores(qi), qi, c_chunk <= r_chunk)

    lam = (jnp.exp(jnp.sum(lq1_ref[...] * lk1_ref[...], axis=-1, keepdims=True))
           - jnp.exp(jnp.sum(lq2_ref[...] * lk2_ref[...], axis=-1, keepdims=True)) + lambda_init)
    o_all = acc_ref[...] / l_ref[...]
    o = o_all[:bq] - lam * o_all[bq:]
    o_ref[0] = (_rms(o, g_ref[...]) * (1.0 - lambda_init)).astype(BF16)


def _diff_attn(q3, k3, v3, lq1, lk1, lq2, lk2, g_subln, lambda_init, bq):
    nb, seq, _ = q3.shape
    hw = 2 * DA_HEAD_DIM
    lam_spec = pl.BlockSpec((1, DA_HEAD_DIM), lambda b, h, i: (0, 0))
    kv_spec = pl.BlockSpec((1, seq, hw), lambda b, h, i: (b, 0, h))
    return pl.pallas_call(
        functools.partial(_attn_kernel, bq=bq, lambda_init=lambda_init),
        out_shape=jax.ShapeDtypeStruct(q3.shape, BF16),
        grid=(nb, DA_HEADS, seq // bq),
        in_specs=[pl.BlockSpec((1, bq, hw), lambda b, h, i: (b, i, h)), kv_spec, kv_spec,
                  lam_spec, lam_spec, lam_spec, lam_spec,
                  pl.BlockSpec((1, hw), lambda b, h, i: (0, 0))],
        out_specs=pl.BlockSpec((1, bq, hw), lambda b, h, i: (b, i, h)),
        scratch_shapes=[pltpu.VMEM((2 * bq, hw), F32)] * 3,
        compiler_params=_cparams("parallel", "parallel", "parallel"),
        name="diff_attn",
    )(q3, k3, v3, lq1, lk1, lq2, lk2, g_subln)


def _post_kernel(ys_ref, oa_ref, gt_ref, x_ref, mod_ref, wba_ref, wout_ref, g_ref, wr_ref, br_ref,
                 h_ref, u2_ref, z_ref, zt_ref, cnt_ref, base_ref):
    d = x_ref.shape[1]

    @pl.when(pl.program_id(0) == 0)
    def _():
        base_ref[...] = jnp.zeros(base_ref.shape, F32)

    mod = mod_ref[0]
    y_att = _dot(oa_ref[...], wba_ref[...])
    gates = gt_ref[...].astype(F32)
    mix_in = gates[:, :d] * ys_ref[...].astype(F32) + gates[:, d:] * y_att
    mix = _dot(mix_in.astype(BF16), wout_ref[...])
    h = x_ref[...] + mod[2:3, :] * mix
    h_ref[...] = h
    u2 = _rms(h, g_ref[...]) * (1.0 + mod[4:5, :]) + mod[3:4, :]
    u2_ref[...] = u2

    u2_hi = pltpu.bitcast(pltpu.bitcast(u2, jnp.uint32) & jnp.uint32(0xFFFF0000), F32)
    u2_lo = (u2 - u2_hi).astype(BF16)
    u2_hi = u2_hi.astype(BF16)
    logits = (_dot(u2_hi, wr_ref[0]) + _dot(u2_hi, wr_ref[1]) + _dot(u2_lo, wr_ref[0])) + br_ref[...]
    lane = lax.broadcasted_iota(jnp.int32, logits.shape, 1)
    neg = jnp.full_like(logits, -jnp.inf)
    big = jnp.full_like(lane, LANES)
    is_grp = (lane >= N_EXPERTS) & (lane < N_EXPERTS + N_GROUPS)
    glog = jnp.where(is_grp, logits, neg)
    gmax = jnp.max(glog, axis=-1, keepdims=True)
    gp = 1.0 / jnp.sum(jnp.exp(glog - gmax), axis=-1, keepdims=True)
    gi = jnp.min(jnp.where(glog == gmax, lane, big), axis=-1, keepdims=True) - N_EXPERTS
    in_grp = (lane < N_EXPERTS) & ((lane // EXP_PER_GROUP) == gi)
    elog = jnp.where(in_grp, logits, neg)
    v1 = jnp.max(elog, axis=-1, keepdims=True)
    i1 = jnp.min(jnp.where(elog == v1, lane, big), axis=-1, keepdims=True)
    elog2 = jnp.where(lane == i1, neg, elog)
    v2 = jnp.max(elog2, axis=-1, keepdims=True)
    i2 = jnp.min(jnp.where(elog2 == v2, lane, big), axis=-1, keepdims=True)
    e2 = jnp.exp(v2 - v1)
    w1 = gp / (1.0 + e2)
    w2 = gp * e2 / (1.0 + e2)
    tm = logits.shape[0]
    hot = jnp.where((lane == i1) | (lane == i2), 1.0, 0.0)
    earlier = (lax.broadcasted_iota(jnp.int32, (tm, tm), 1) < lax.broadcasted_iota(jnp.int32, (tm, tm), 0))
    before = _dot(jnp.where(earlier, 1.0, 0.0).astype(BF16), hot.astype(BF16)) + base_ref[...]
    rank1 = jnp.sum(jnp.where(lane == i1, before, 0.0), axis=-1, keepdims=True)
    rank2 = jnp.sum(jnp.where(lane == i2, before, 0.0), axis=-1, keepdims=True)
    base_ref[...] += jnp.sum(hot, axis=0, keepdims=True)
    cnt_ref[...] = base_ref[...]
    z = jnp.zeros_like(logits)
    for k, val in enumerate((i1.astype(F32), i2.astype(F32), rank1, rank2, w1, w2)):
        z = jnp.where(lane == k, val, z)
    z_ref[...] = z
    zt_ref[...] = z.T[:ROUTE_ROWS, :]


def _post_mix(ys, oa, gates, x2, mod3, wba, wout, g, wr, br, seq, tm):
    t, d = x2.shape
    tpb = seq // tm
    row = lambda i: (i, 0)
    full = lambda a: pl.BlockSpec(a.shape, lambda i: (0,) * a.ndim)
    return pl.pallas_call(
        _post_kernel,
        out_shape=[jax.ShapeDtypeStruct((t, d), F32), jax.ShapeDtypeStruct((t, d), F32),
                   jax.ShapeDtypeStruct((t, LANES), F32), jax.ShapeDtypeStruct((ROUTE_ROWS, t), F32),
                   jax.ShapeDtypeStruct((1, LANES), F32)],
        grid=(t // tm,),
        in_specs=[pl.BlockSpec((tm, d), row), pl.BlockSpec((tm, oa.shape[1]), row),
                  pl.BlockSpec((tm, 2 * d), row), pl.BlockSpec((tm, d), row),
                  pl.BlockSpec((1, 6, d), lambda i: (i // tpb, 0, 0)),
                  full(wba), full(wout), full(g), full(wr), full(br)],
        out_specs=[pl.BlockSpec((tm, d), row), pl.BlockSpec((tm, d), row), pl.BlockSpec((tm, LANES), row),
                   pl.BlockSpec((ROUTE_ROWS, tm), lambda i: (0, i)), pl.BlockSpec((1, LANES), lambda i: (0, 0))],
        scratch_shapes=[pltpu.VMEM((1, LANES), F32)],
        compiler_params=_cparams("arbitrary"),
        name="post_mix",
    )(ys, oa, gates, x2, mod3, wba, wout, g, wr, br)


def _row_copy(src_ref, src_row, dst_ref, dst_row, sem):
    return pltpu.make_async_copy(src_ref.at[pl.ds(src_row, 1), :], dst_ref.at[pl.ds(dst_row, 1), :], sem)


def _dispatch_kernel(pad_lo_ref, pad_hi_ref, n_used_ref, pos_ref, u_ref, xs_ref, zero_ref, sem, pad_sem):
    tm = u_ref.shape[0]
    tile = zero_ref.shape[0]
    n_tiles = xs_ref.shape[0] // tile

    def send(r, carry):
        _row_copy(u_ref, r, xs_ref, pos_ref[r], sem).start()
        _row_copy(u_ref, r, xs_ref, pos_ref[tm + r], sem).start()
        return carry

    lax.fori_loop(0, tm, send, 0, unroll=8)

    @pl.when(pl.program_id(0) == pl.num_programs(0) - 1)
    def _():
        zero_ref[...] = jnp.zeros(zero_ref.shape, F32)

        def pad_copies(e, wait):
            lo = pad_lo_ref[e]
            n = pad_hi_ref[e] - lo
            for b in range(tile.bit_length() - 1):
                size = 1 << b

                @pl.when((n >> b) & 1 == 1)
                def _():
                    row = pl.multiple_of(lo + (n & (size - 1)), min(size, 8))
                    cp = pltpu.make_async_copy(zero_ref.at[pl.ds(0, size), :],
                                               xs_ref.at[pl.ds(row, size), :], pad_sem)
                    cp.wait() if wait else cp.start()

        def tile_copy(i):
            return pltpu.make_async_copy(zero_ref, xs_ref.at[pl.ds(i * tile, tile), :], pad_sem)

        for wait in (False, True):
            @pl.loop(0, N_EXPERTS)
            def _(e):
                pad_copies(e, wait)

            @pl.loop(n_used_ref[0], n_tiles)
            def _(i):
                tile_copy(i).wait() if wait else tile_copy(i).start()

    for _ in range(2):
        pltpu.make_async_copy(u_ref, xs_ref.at[pl.ds(0, tm), :], sem).wait()


def _dispatch(pad_lo, pad_hi, n_used, pos, u2, n_rows, tm, tile):
    t, d = u2.shape
    return pl.pallas_call(
        _dispatch_kernel,
        out_shape=jax.ShapeDtypeStruct((n_rows, d), F32),
        grid_spec=pltpu.PrefetchScalarGridSpec(
            num_scalar_prefetch=3,
            grid=(t // tm,),
            in_specs=[pl.BlockSpec((2 * tm,), lambda i, *_: (i,), memory_space=pltpu.SMEM),
                      pl.BlockSpec((tm, d), lambda i, *_: (i, 0))],
            out_specs=pl.BlockSpec(memory_space=pl.ANY),
            scratch_shapes=[pltpu.VMEM((tile, d), F32), pltpu.SemaphoreType.DMA, pltpu.SemaphoreType.DMA]),
        compiler_params=_cparams("arbitrary"),
        name="moe_dispatch",
    )(pad_lo, pad_hi, n_used, pos, u2)


def _expert_kernel(tile_e_ref, n_used_ref, x_ref, wgu_ref, wd_ref, y_ref):
    used = pl.program_id(0) < n_used_ref[0]

    @pl.when(used)
    def _():
        gu = _dot(x_ref[...].astype(BF16), wgu_ref[0])
        gate = gu[:, :D_EXPERT]
        hdn = gate * jax.nn.sigmoid(gate) * gu[:, D_EXPERT:]
        y_ref[...] = _dot(hdn.astype(BF16), wd_ref[0])

    @pl.when(jnp.logical_not(used))
    def _():
        y_ref[...] = jnp.zeros(y_ref.shape, F32)


def _experts(tile_e, n_used, xs, wgu, wd, tm):
    n_rows, d = xs.shape
    return pl.pallas_call(
        _expert_kernel,
        out_shape=jax.ShapeDtypeStruct((n_rows, d), F32),
        grid_spec=pltpu.PrefetchScalarGridSpec(
            num_scalar_prefetch=2,
            grid=(n_rows // tm,),
            in_specs=[pl.BlockSpec((tm, d), lambda i, te, nu: (jnp.minimum(i, nu[0] - 1), 0)),
                      pl.BlockSpec((1, d, 2 * D_EXPERT), lambda i, te, nu: (te[i], 0, 0)),
                      pl.BlockSpec((1, D_EXPERT, d), lambda i, te, nu: (te[i], 0, 0))],
            out_specs=pl.BlockSpec((tm, d), lambda i, te, nu: (i, 0))),
        compiler_params=_cparams("arbitrary"),
        name="moe_experts",
    )(tile_e, n_used, xs, wgu, wd)


def _combine_kernel(pos_ref, ys_ref, z_ref, h_ref, mod_ref, g_ref, o_ref, buf_ref, sem, *, final_norm):
    tm = h_ref.shape[0]

    def fetch(r, carry):
        _row_copy(ys_ref, pos_ref[r], buf_ref.at[0], r, sem).start()
        _row_copy(ys_ref, pos_ref[tm + r], buf_ref.at[1], r, sem).start()
        return carry

    lax.fori_loop(0, tm, fetch, 0, unroll=8)
    for k in range(2):
        pltpu.make_async_copy(ys_ref.at[pl.ds(0, tm), :], buf_ref.at[k], sem).wait()
    z = z_ref[...]
    ffn = z[:, 4:5] * buf_ref[0] + z[:, 5:6] * buf_ref[1]
    h = h_ref[...] + mod_ref[0][5:6, :] * ffn
    o_ref[...] = _rms(h, g_ref[...]) if final_norm else h


def _combine(pos, ys, z, h1, mod3, g_final, seq, tm, final_norm):
    t, d = h1.shape
    tpb = seq // tm
    row = lambda i: (i, 0)
    return pl.pallas_call(
        functools.partial(_combine_kernel, final_norm=final_norm),
        out_shape=jax.ShapeDtypeStruct((t, d), F32),
        grid=(t // tm,),
        in_specs=[pl.BlockSpec((2 * tm,), lambda i: (i,), memory_space=pltpu.SMEM),
                  pl.BlockSpec(memory_space=pl.ANY),
                  pl.BlockSpec((tm, LANES), row), pl.BlockSpec((tm, d), row),
                  pl.BlockSpec((1, 6, d), lambda i: (i // tpb, 0, 0)),
                  pl.BlockSpec((1, d), lambda i: (0, 0))],
        out_specs=pl.BlockSpec((tm, d), row),
        scratch_shapes=[pltpu.VMEM((2, tm, d), F32), pltpu.SemaphoreType.DMA],
        compiler_params=_cparams("arbitrary"),
        name="moe_combine",
    )(pos, ys, z, h1, mod3, g_final)


def _route_tables(cnt, zt, n_tiles, tm_tok, tile):
    i32 = jnp.int32
    counts = cnt[0, :N_EXPERTS].astype(i32)
    padded = (counts + tile - 1) // tile * tile
    ends = jnp.cumsum(padded)
    starts = ends - padded
    n_used = (ends[-1] // tile).reshape(1)
    tile_row = jnp.minimum(jnp.arange(n_tiles, dtype=i32), n_used - 1) * tile
    tile_e = jnp.minimum(jnp.sum(tile_row[:, None] >= ends[None, :], axis=1), N_EXPERTS - 1).astype(i32)
    pos1 = jnp.take(starts, zt[0].astype(i32)) + zt[2].astype(i32)
    pos2 = jnp.take(starts, zt[1].astype(i32)) + zt[3].astype(i32)
    pos = jnp.stack([pos1.reshape(-1, tm_tok), pos2.reshape(-1, tm_tok)], axis=1).reshape(-1)
    return starts + counts, ends, pos, tile_e, n_used.astype(i32)


def kernel(x, c, w_ada, b_ada, g_norm_mix, w_in, b_in, s5_a_re, s5_a_im, s5_b_re, s5_b_im, s5_c_re, s5_c_im, s5_d, s5_log_dt, w_glu, b_glu, lambda_q1, lambda_k1, lambda_q2, lambda_k2, g_subln, w_br_ssm, w_br_attn, w_out, g_norm_ffn, w_router_grp, b_router_grp, w_router_exp, b_router_exp, w_exp_gate, w_exp_up, w_exp_down, g_final):
    nb, seq, d = x.shape
    depth = w_ada.shape[0]
    s5w = s5_d.shape[1]
    daw = w_br_attn.shape[1]
    ng = s5w // S5_GROUP
    nk = seq // S5_STEP
    assert seq % S5_TILE == 0 and ng % 2 == 0
    widths = (daw, daw, daw, 2 * d)
    tm = 512
    bq = 512
    row = lambda a: a.reshape(1, -1)
    col = lambda a: a.reshape(-1, 1)

    h = x.reshape(nb * seq, d)
    for l in range(depth):
        lambda_init = 0.8 - 0.6 * math.exp(-0.3 * l)
        mod3 = _ada_mod(c, w_ada[l], row(b_ada[l])).reshape(nb, 6, d)

        w_in_b = w_in[l].astype(BF16)
        q, k, v, gates = _in_proj(h, mod3, row(g_norm_mix[l]), w_in_b[:, s5w:], row(b_in[l][s5w:]),
                                  seq, widths, tm)

        u_t = _s5_in(h, mod3, row(g_norm_mix[l]), w_in_b[:, :s5w].T, col(b_in[l][:s5w]), seq, S5_TILE)
        ops = _s5_prep(s5_a_re[l], s5_a_im[l], s5_log_dt[l], s5_b_re[l], s5_b_im[l], s5_c_re[l], s5_c_im[l],
                       s5_d[l])
        y_t = _s5_core(u_t, ops, nb, nk)
        ys = _s5_out(y_t, w_glu[l].T.astype(BF16), col(b_glu[l]), w_br_ssm[l].astype(BF16), S5_TILE)

        as3 = lambda a: a.reshape(nb, seq, daw)
        oa = _diff_attn(as3(q), as3(k), as3(v), row(lambda_q1[l]), row(lambda_k1[l]), row(lambda_q2[l]),
                        row(lambda_k2[l]), row(g_subln[l]), lambda_init, bq).reshape(nb * seq, daw)

        w_router = jnp.zeros((d, LANES), F32)
        w_router = w_router.at[:, :N_EXPERTS].set(w_router_exp[l])
        w_router = w_router.at[:, N_EXPERTS:N_EXPERTS + N_GROUPS].set(w_router_grp[l])
        b_router = jnp.zeros((1, LANES), F32)
        b_router = b_router.at[0, :N_EXPERTS].set(b_router_exp[l])
        b_router = b_router.at[0, N_EXPERTS:N_EXPERTS + N_GROUPS].set(b_router_grp[l])
        wr_hi = lax.bitcast_convert_type(
            lax.bitcast_convert_type(w_router, jnp.uint32) & jnp.uint32(0xFFFF0000), F32)
        wr_split = jnp.stack([wr_hi.astype(BF16), (w_router - wr_hi).astype(BF16)])
        h1, u2, z, zt, cnt = _post_mix(ys, oa, gates, h, mod3, w_br_attn[l].astype(BF16),
                                       w_out[l].astype(BF16), row(g_norm_ffn[l]), wr_split, b_router, seq, tm)

        n_tiles = (2 * nb * seq) // EXPERT_TILE + N_EXPERTS
        pad_lo, pad_hi, pos, tile_e, n_used = _route_tables(cnt, zt, n_tiles, tm, EXPERT_TILE)
        xs = _dispatch(pad_lo, pad_hi, n_used, pos, u2, n_tiles * EXPERT_TILE, tm, EXPERT_TILE)
        wgu = jnp.concatenate([w_exp_gate[l], w_exp_up[l]], axis=-1).astype(BF16)
        ys_e = _experts(tile_e, n_used, xs, wgu, w_exp_down[l].astype(BF16), EXPERT_TILE)
        h = _combine(pos, ys_e, z, h1, mod3, row(g_final), seq, tm, final_norm=(l == depth - 1))
    return h.reshape(nb, seq, d)
```

```python
import functools
import math

import jax
import jax.numpy as jnp
from jax import lax
from jax.experimental import pallas as pl
from jax.experimental.pallas import tpu as pltpu

EPS = 1e-6
CHUNK = 64
S5_GROUP = 16
S5_STATE = 64
S5_STEP = 16
S5_TILE = 2048
DA_HEADS = 8
DA_HEAD_DIM = 64
N_GROUPS = 4
EXP_PER_GROUP = 8
N_EXPERTS = N_GROUPS * EXP_PER_GROUP
D_EXPERT = 256
EXPERT_TILE = 512
ROUTE_ROWS = 8
LANES = 128
Q_SCALE = DA_HEAD_DIM ** -0.5 * math.log2(math.e)
VMEM_LIMIT = 56 * 1024 * 1024

F32 = jnp.float32
BF16 = jnp.bfloat16
HIGHEST = lax.Precision.HIGHEST


def _cparams(*sem):
    return pltpu.CompilerParams(dimension_semantics=sem, vmem_limit_bytes=VMEM_LIMIT)


def _dot(a, b):
    return jnp.dot(a, b, preferred_element_type=F32)


def _dot_nt(a, b):
    return lax.dot_general(a, b, (((1,), (1,)), ((), ())), preferred_element_type=F32)


def _rms(x, g):
    return x * lax.rsqrt(jnp.mean(x * x, axis=-1, keepdims=True) + EPS) * g


def _ada_kernel(c_ref, w_ref, b_ref, o_ref):
    c = c_ref[...]
    cs = c * jax.nn.sigmoid(c)
    o_ref[...] = jnp.dot(cs, w_ref[...], preferred_element_type=F32, precision=HIGHEST) + b_ref[...]


def _ada_mod(c, w, b):
    nb, d = c.shape
    n = w.shape[1]
    tn = n // 4
    return pl.pallas_call(
        _ada_kernel,
        out_shape=jax.ShapeDtypeStruct((nb, n), F32),
        grid=(n // tn,),
        in_specs=[pl.BlockSpec((nb, d), lambda j: (0, 0)),
                  pl.BlockSpec((d, tn), lambda j: (0, j)),
                  pl.BlockSpec((1, tn), lambda j: (0, j))],
        out_specs=pl.BlockSpec((nb, tn), lambda j: (0, j)),
        compiler_params=_cparams("parallel"),
        name="ada_mod",
    )(c, w, b)


def _modulated_norm(x_ref, mod_ref, g_ref):
    mod = mod_ref[0]
    return _rms(x_ref[...], g_ref[...]) * (1.0 + mod[1:2, :]) + mod[0:1, :]


def _inproj_kernel(x_ref, mod_ref, g_ref, w_ref, b_ref, q_ref, k_ref, v_ref, gt_ref, *, widths):
    ub = _modulated_norm(x_ref, mod_ref, g_ref).astype(BF16)
    qw, kw, vw, gw = widths
    o = 0
    q_ref[...] = ((_dot(ub, w_ref[:, o:o + qw]) + b_ref[:, o:o + qw]) * Q_SCALE).astype(BF16)
    o += qw
    k_ref[...] = (_dot(ub, w_ref[:, o:o + kw]) + b_ref[:, o:o + kw]).astype(BF16)
    o += kw
    v_ref[...] = (_dot(ub, w_ref[:, o:o + vw]) + b_ref[:, o:o + vw]).astype(BF16)
    o += vw
    gt_ref[...] = jax.nn.sigmoid(_dot(ub, w_ref[:, o:o + gw]) + b_ref[:, o:o + gw]).astype(BF16)


def _in_proj(x2, mod3, g, w, b, seq, widths, tm):
    t, d = x2.shape
    tpb = seq // tm
    row = lambda i: (i, 0)
    full = lambda a: pl.BlockSpec(a.shape, lambda i: (0, 0))
    return pl.pallas_call(
        functools.partial(_inproj_kernel, widths=widths),
        out_shape=[jax.ShapeDtypeStruct((t, wd), BF16) for wd in widths],
        grid=(t // tm,),
        in_specs=[pl.BlockSpec((tm, d), row), pl.BlockSpec((1, 6, d), lambda i: (i // tpb, 0, 0)),
                  full(g), full(w), full(b)],
        out_specs=[pl.BlockSpec((tm, wd), row) for wd in widths],
        compiler_params=_cparams("parallel"),
        name="in_proj",
    )(x2, mod3, g, w, b)


def _s5_in_kernel(x_ref, mod_ref, g_ref, wst_ref, bs_ref, ut_ref, slab_ref):
    u = _modulated_norm(x_ref, mod_ref, g_ref)
    nslab = slab_ref.shape[0]
    nk = x_ref.shape[0] // S5_STEP
    for j in range(nslab):
        slab_ref[j] = u[:, j * LANES:(j + 1) * LANES]
    for s in range(S5_STEP):
        us = jnp.concatenate([slab_ref[j, pl.ds(s, nk, stride=S5_STEP), :] for j in range(nslab)], axis=1)
        ut_ref[s] = (_dot_nt(wst_ref[...], us.astype(BF16)) + bs_ref[...]).astype(BF16)


def _s5_in(x2, mod3, g, wst, bs, seq, tm):
    t, d = x2.shape
    s5w = wst.shape[0]
    nkt = tm // S5_STEP
    tpb = seq // tm
    full = lambda a: pl.BlockSpec(a.shape, lambda i: (0, 0))
    return pl.pallas_call(
        _s5_in_kernel,
        out_shape=jax.ShapeDtypeStruct((S5_STEP, s5w, t // S5_STEP), BF16),
        grid=(t // tm,),
        in_specs=[pl.BlockSpec((tm, d), lambda i: (i, 0)), pl.BlockSpec((1, 6, d), lambda i: (i // tpb, 0, 0)),
                  full(g), full(wst), full(bs)],
        out_specs=pl.BlockSpec((S5_STEP, s5w, nkt), lambda i: (0, 0, i)),
        scratch_shapes=[pltpu.VMEM((d // LANES, tm, LANES), F32)],
        compiler_params=_cparams("parallel"),
        name="s5_in",
    )(x2, mod3, g, wst, bs)


def _s5_out_kernel(yt_ref, wglut_ref, bglu_ref, wbs_ref, o_ref, slab_ref):
    nslab = slab_ref.shape[0]
    nk = yt_ref.shape[2]
    for s in range(S5_STEP):
        yt = yt_ref[s]
        zt = _dot(wglut_ref[...], yt) + bglu_ref[...]
        yg = (yt.astype(F32) * jax.nn.sigmoid(zt)).T
        for j in range(nslab):
            slab_ref[j, pl.ds(s, nk, stride=S5_STEP), :] = yg[:, j * LANES:(j + 1) * LANES]
    yg = jnp.concatenate([slab_ref[j] for j in range(nslab)], axis=1)
    o_ref[...] = _dot(yg.astype(BF16), wbs_ref[...]).astype(BF16)


def _s5_out(y_t, wglut, bglu, wbs, tm):
    n, s5w, r = y_t.shape
    d = wbs.shape[1]
    nkt = tm // n
    full = lambda a: pl.BlockSpec(a.shape, lambda i: (0, 0))
    return pl.pallas_call(
        _s5_out_kernel,
        out_shape=jax.ShapeDtypeStruct((r * n, d), BF16),
        grid=(r // nkt,),
        in_specs=[pl.BlockSpec((n, s5w, nkt), lambda i: (0, 0, i)), full(wglut), full(bglu), full(wbs)],
        out_specs=pl.BlockSpec((tm, d), lambda i: (i, 0)),
        scratch_shapes=[pltpu.VMEM((s5w // LANES, tm, LANES), F32)],
        compiler_params=_cparams("parallel"),
        name="s5_out",
    )(y_t, wglut, bglu, wbs)


def _s5_prep_kernel(lrc_ref, lic_ref, dtc_ref, lrr_ref, lir_ref, dtr_ref, br_ref, bi_ref, cr_ref, ci_ref,
                    mt_ref, wtr_ref, wti_ref, vtr_ref, vti_ref, a16r_ref, a16i_ref):
    n, gc, ns = S5_STEP, S5_GROUP, S5_STATE
    w = n * gc
    iota = lambda shape, dim: lax.broadcasted_iota(jnp.int32, shape, dim)
    hdot = lambda a, b: jnp.dot(a, b, preferred_element_type=F32, precision=HIGHEST)

    def power(lr, li, dt, k):
        mag = jnp.exp(lr * dt * k)
        return mag * jnp.cos(li * dt * k), mag * jnp.sin(li * dt * k)

    lr, li, dt = lrc_ref[0], lic_ref[0], jnp.exp(dtc_ref[0])
    abr, abi = power(lr, li, dt, 1.0)
    den = lr * lr + li * li
    nr = abr - 1.0
    f_re = (nr * lr + abi * li) / den
    f_im = (abi * lr - nr * li) / den
    bb_re = f_re * br_ref[0] - f_im * bi_ref[0]
    bb_im = f_re * bi_ref[0] + f_im * br_ref[0]
    spread = jnp.where(iota((gc, 2 * w), 1) % gc == iota((gc, 2 * w), 0), 1.0, 0.0)
    bt_re, bt_im = hdot(bb_re, spread), hdot(bb_im, spread)
    k_in = (n - 1 - (iota((1, 2 * w), 1) % w) // gc).astype(F32)
    p_re, p_im = power(lr, li, dt, k_in)
    e_re = p_re * bt_re - p_im * bt_im
    e_im = p_re * bt_im + p_im * bt_re
    own = iota((2 * ns, 2 * w), 0) // ns == iota((2 * ns, 2 * w), 1) // w
    wtr_ref[0] = jnp.where(own, e_re, 0.0).astype(BF16)
    wti_ref[0] = jnp.where(own, e_im, 0.0).astype(BF16)

    lr_r, li_r, dt_r = lrr_ref[0], lir_ref[0], jnp.exp(dtr_ref[0])
    c_r, c_i = power(lr_r, li_r, dt_r, 1.0)
    for _ in range(n.bit_length() - 1):
        c_r, c_i = c_r * c_r - c_i * c_i, 2.0 * c_r * c_i
    a16r_ref[0], a16i_ref[0] = c_r, c_i
    k_out = ((iota((2 * w, 1), 0) % w) // gc + 1).astype(F32)
    q_re, q_im = power(lr_r, li_r, dt_r, k_out)
    c_re = jnp.concatenate([cr_ref[0]] * (2 * n), axis=0)
    c_im = jnp.concatenate([ci_ref[0]] * (2 * n), axis=0)
    own = iota((2 * w, 2 * ns), 0) // w == iota((2 * w, 2 * ns), 1) // ns
    vtr_ref[0] = jnp.where(own, c_re * q_re - c_im * q_im, 0.0).astype(BF16)
    vti_ref[0] = jnp.where(own, -(c_re * q_im + c_im * q_re), 0.0).astype(BF16)

    group_of_lane = iota((gc, 2 * ns), 1) // ns
    for h in range(2):
        mine = group_of_lane == h
        krev = (hdot(jnp.where(mine, cr_ref[0], 0.0), e_re[:, :w])
                - hdot(jnp.where(mine, ci_ref[0], 0.0), e_im[:, :w]))
        strip = jnp.concatenate([krev, jnp.zeros_like(krev)], axis=1)
        rows = []
        for t in range(n):
            shift = (n - 1 - t) * gc
            rows.append((pltpu.roll(strip, 2 * w - shift, 1) if shift else strip)[:, :w])
        mt_ref[0, h * w:(h + 1) * w, h * w:(h + 1) * w] = jnp.concatenate(rows, axis=0).astype(BF16)
        mt_ref[0, h * w:(h + 1) * w, (1 - h) * w:(2 - h) * w] = jnp.zeros((w, w), BF16)


def _s5_prep(a_re, a_im, log_dt, b_re, b_im, c_re, c_im, d_skip):
    ng, ns = a_re.shape
    gc = b_re.shape[-1]
    na = ng // 2
    w2 = 2 * S5_STEP * gc
    col = lambda a: a.reshape(na, 2 * ns, 1)
    row = lambda a: a.reshape(na, 1, 2 * ns)
    per_state = jnp.repeat(log_dt, ns)
    pair_c = lambda a: a.reshape(na, 2, gc, ns).transpose(0, 2, 1, 3).reshape(na, gc, 2 * ns)
    g3 = lambda g: (g, 0, 0)
    spec = lambda *shape: pl.BlockSpec((1,) + shape, g3)
    shp = lambda *shape, dt=BF16: jax.ShapeDtypeStruct((na,) + shape, dt)
    ops = pl.pallas_call(
        _s5_prep_kernel,
        out_shape=[shp(w2, w2), shp(2 * ns, w2), shp(2 * ns, w2), shp(w2, 2 * ns), shp(w2, 2 * ns),
                   shp(1, 2 * ns, dt=F32), shp(1, 2 * ns, dt=F32)],
        grid=(na,),
        in_specs=[spec(2 * ns, 1)] * 3 + [spec(1, 2 * ns)] * 3 + [spec(2 * ns, gc)] * 2 + [spec(gc, 2 * ns)] * 2,
        out_specs=[spec(w2, w2), spec(2 * ns, w2), spec(2 * ns, w2), spec(w2, 2 * ns), spec(w2, 2 * ns),
                   spec(1, 2 * ns), spec(1, 2 * ns)],
        compiler_params=_cparams("parallel"),
        name="s5_prep",
    )(col(a_re), col(a_im), col(per_state), row(a_re), row(a_im), row(per_state),
      b_re.reshape(na, 2 * ns, gc), b_im.reshape(na, 2 * ns, gc), pair_c(c_re), pair_c(c_im))
    d_col = jnp.broadcast_to(d_skip.reshape(na, 2, 1, gc), (na, 2, S5_STEP, gc)).reshape(na, w2, 1)
    return (*ops, d_col)


def _s5_core_kernel(u_ref, m_ref, wr_ref, wi_ref, vr_ref, vi_ref, ar_ref, ai_ref, d_ref, o_ref,
                    sr_ref, si_ref, xr_ref, xi_ref, *, nb, nk):
    n, gw, r = u_ref.shape
    gc = gw // 2
    u = jnp.concatenate([u_ref[:, h * gc:(h + 1) * gc, :].reshape(n * gc, r) for h in range(2)], axis=0)
    sr_ref[...] = _dot(wr_ref[0], u).T
    si_ref[...] = _dot(wi_ref[0], u).T
    a_r = ar_ref[0]
    a_i = ai_ref[0]

    def step(k, carry):
        out = []
        for b in range(nb):
            x_r, x_i = carry[2 * b], carry[2 * b + 1]
            row = pl.ds(b * nk + k, 1)
            xr_ref[row, :] = x_r
            xi_ref[row, :] = x_i
            out.append(a_r * x_r - a_i * x_i + sr_ref[row, :])
            out.append(a_r * x_i + a_i * x_r + si_ref[row, :])
        return tuple(out)

    zero = jnp.zeros((1, xr_ref.shape[1]), F32)
    lax.fori_loop(0, nk, step, (zero,) * (2 * nb), unroll=4)
    y = (_dot(m_ref[0], u) + _dot(vr_ref[0], xr_ref[...].T.astype(BF16))
         + _dot(vi_ref[0], xi_ref[...].T.astype(BF16)))
    y = y + d_ref[0] * u.astype(F32)
    y = jax.nn.gelu(y).astype(BF16)
    for h in range(2):
        o_ref[:, h * gc:(h + 1) * gc, :] = y[h * n * gc:(h + 1) * n * gc].reshape(n, gc, r)


def _s5_core(u_t, ops, nb, nk):
    m, wr, wi, vr, vi, ar, ai, d_col = ops
    npair, w2, _ = m.shape
    p2 = wr.shape[1]
    n, _, r = u_t.shape
    gw = w2 // n
    g3 = lambda g: (g, 0, 0)
    data = pl.BlockSpec((n, gw, r), lambda g: (0, g, 0))
    return pl.pallas_call(
        functools.partial(_s5_core_kernel, nb=nb, nk=nk),
        out_shape=jax.ShapeDtypeStruct(u_t.shape, BF16),
        grid=(npair,),
        in_specs=[data, pl.BlockSpec((1, w2, w2), g3),
                  pl.BlockSpec((1, p2, w2), g3), pl.BlockSpec((1, p2, w2), g3),
                  pl.BlockSpec((1, w2, p2), g3), pl.BlockSpec((1, w2, p2), g3),
                  pl.BlockSpec((1, 1, p2), g3), pl.BlockSpec((1, 1, p2), g3),
                  pl.BlockSpec((1, w2, 1), g3)],
        out_specs=data,
        scratch_shapes=[pltpu.VMEM((r, p2), F32)] * 4,
        compiler_params=_cparams("parallel"),
        name="s5_core",
    )(u_t, m, wr, wi, vr, vi, ar, ai, d_col)


def _attn_kernel(q_ref, k_ref, v_ref, lq1_ref, lk1_ref, lq2_ref, lk2_ref, g_ref, o_ref,
                 m_ref, l_ref, acc_ref, *, bq, lambda_init):
    qi = pl.program_id(2)
    q = q_ref[0]
    lane = lax.broadcasted_iota(jnp.int32, (1, 2 * DA_HEAD_DIM), 1)
    zero = jnp.zeros_like(q)
    qs = jnp.concatenate([jnp.where(lane < DA_HEAD_DIM, q, zero),
                          jnp.where(lane >= DA_HEAD_DIM, q, zero)], axis=0)
    m_ref[...] = jnp.full(m_ref.shape, -jnp.inf, F32)
    l_ref[...] = jnp.zeros(l_ref.shape, F32)
    acc_ref[...] = jnp.zeros(acc_ref.shape, F32)
    reps = bq // LANES

    def keys(j):
        return pl.ds(pl.multiple_of(j * bq, bq), bq)

    def scores(j):
        return _dot_nt(qs, k_ref[0, keys(j), :])

    def absorb(s, j, mask):
        if mask is not None:
            s = jnp.where(mask, s, -jnp.inf)
        m_old = m_ref[...]
        m_new = jnp.maximum(m_old, jnp.max(s, axis=-1, keepdims=True))
        alpha = jnp.exp2(m_old - m_new)
        p = jnp.exp2(s - jnp.concatenate([m_new] * reps, axis=1))
        l_ref[...] = alpha * l_ref[...] + jnp.sum(p, axis=-1, keepdims=True)
        acc_ref[...] = alpha * acc_ref[...] + _dot(p.astype(BF16), v_ref[0, keys(j), :])
        m_ref[...] = m_new

    def visible_block(j, carry):
        absorb(scores(j), j, None)
        return carry

    lax.fori_loop(0, qi, visible_block, 0)
    r_chunk = (lax.broadcasted_iota(jnp.int32, (2 * bq, bq), 0) % bq) // CHUNK
    c_chunk = lax.broadcasted_iota(jnp.int32, (2 * bq, bq), 1) // CHUNK
    absorb(scores(qi), qi, c_chunk <= r_chunk)

    lam = (jnp.exp(jnp.sum(lq1_ref[...] * lk1_ref[...], axis=-1, keepdims=True))
           - jnp.exp(jnp.sum(lq2_ref[...] * lk2_ref[...], axis=-1, keepdims=True)) + lambda_init)
    o_all = acc_ref[...] / l_ref[...]
    o = o_all[:bq] - lam * o_all[bq:]
    o_ref[0] = (_rms(o, g_ref[...]) * (1.0 - lambda_init)).astype(BF16)


def _diff_attn(q3, k3, v3, lq1, lk1, lq2, lk2, g_subln, lambda_init, bq):
    nb, seq, _ = q3.shape
    hw = 2 * DA_HEAD_DIM
    lam_spec = pl.BlockSpec((1, DA_HEAD_DIM), lambda b, h, i: (0, 0))
    kv_spec = pl.BlockSpec((1, seq, hw), lambda b, h, i: (b, 0, h))
    return pl.pallas_call(
        functools.partial(_attn_kernel, bq=bq, lambda_init=lambda_init),
        out_shape=jax.ShapeDtypeStruct(q3.shape, BF16),
        grid=(nb, DA_HEADS, seq // bq),
        in_specs=[pl.BlockSpec((1, bq, hw), lambda b, h, i: (b, i, h)), kv_spec, kv_spec,
                  lam_spec, lam_spec, lam_spec, lam_spec,
                  pl.BlockSpec((1, hw), lambda b, h, i: (0, 0))],
        out_specs=pl.BlockSpec((1, bq, hw), lambda b, h, i: (b, i, h)),
        scratch_shapes=[pltpu.VMEM((2 * bq, hw), F32)] * 3,
        compiler_params=_cparams("parallel", "parallel", "parallel"),
        name="diff_attn",
    )(q3, k3, v3, lq1, lk1, lq2, lk2, g_subln)


def _post_kernel(ys_ref, oa_ref, gt_ref, x_ref, mod_ref, wba_ref, wout_ref, g_ref, wr_ref, br_ref,
                 h_ref, u2_ref, z_ref, zt_ref, cnt_ref, base_ref):
    d = x_ref.shape[1]

    @pl.when(pl.program_id(0) == 0)
    def _():
        base_ref[...] = jnp.zeros(base_ref.shape, F32)

    mod = mod_ref[0]
    y_att = _dot(oa_ref[...], wba_ref[...])
    gates = gt_ref[...].astype(F32)
    mix_in = gates[:, :d] * ys_ref[...].astype(F32) + gates[:, d:] * y_att
    mix = _dot(mix_in.astype(BF16), wout_ref[...])
    h = x_ref[...] + mod[2:3, :] * mix
    h_ref[...] = h
    u2 = _rms(h, g_ref[...]) * (1.0 + mod[4:5, :]) + mod[3:4, :]
    _to_slabs(u2_ref, u2)

    u2_hi = pltpu.bitcast(pltpu.bitcast(u2, jnp.uint32) & jnp.uint32(0xFFFF0000), F32)
    u2_lo = (u2 - u2_hi).astype(BF16)
    u2_hi = u2_hi.astype(BF16)
    logits = (_dot(u2_hi, wr_ref[0]) + _dot(u2_hi, wr_ref[1]) + _dot(u2_lo, wr_ref[0])) + br_ref[...]
    lane = lax.broadcasted_iota(jnp.int32, logits.shape, 1)
    neg = jnp.full_like(logits, -jnp.inf)
    big = jnp.full_like(lane, LANES)
    is_grp = (lane >= N_EXPERTS) & (lane < N_EXPERTS + N_GROUPS)
    glog = jnp.where(is_grp, logits, neg)
    gmax = jnp.max(glog, axis=-1, keepdims=True)
    gp = 1.0 / jnp.sum(jnp.exp(glog - gmax), axis=-1, keepdims=True)
    gi = jnp.min(jnp.where(glog == gmax, lane, big), axis=-1, keepdims=True) - N_EXPERTS
    in_grp = (lane < N_EXPERTS) & ((lane // EXP_PER_GROUP) == gi)
    elog = jnp.where(in_grp, logits, neg)
    v1 = jnp.max(elog, axis=-1, keepdims=True)
    i1 = jnp.min(jnp.where(elog == v1, lane, big), axis=-1, keepdims=True)
    elog2 = jnp.where(lane == i1, neg, elog)
    v2 = jnp.max(elog2, axis=-1, keepdims=True)
    i2 = jnp.min(jnp.where(elog2 == v2, lane, big), axis=-1, keepdims=True)
    e2 = jnp.exp(v2 - v1)
    w1 = gp / (1.0 + e2)
    w2 = gp * e2 / (1.0 + e2)
    tm = logits.shape[0]
    hot = jnp.where((lane == i1) | (lane == i2), 1.0, 0.0)
    earlier = (lax.broadcasted_iota(jnp.int32, (tm, tm), 1) < lax.broadcasted_iota(jnp.int32, (tm, tm), 0))
    before = _dot(jnp.where(earlier, 1.0, 0.0).astype(BF16), hot.astype(BF16)) + base_ref[...]
    rank1 = jnp.sum(jnp.where(lane == i1, before, 0.0), axis=-1, keepdims=True)
    rank2 = jnp.sum(jnp.where(lane == i2, before, 0.0), axis=-1, keepdims=True)
    base_ref[...] += jnp.sum(hot, axis=0, keepdims=True)
    cnt_ref[...] = base_ref[...]
    z = jnp.zeros_like(logits)
    for k, val in enumerate((i1.astype(F32), i2.astype(F32), rank1, rank2, w1, w2)):
        z = jnp.where(lane == k, val, z)
    z_ref[...] = z
    zt_ref[...] = z.T[:ROUTE_ROWS, :]


def _post_mix(ys, oa, gates, x2, mod3, wba, wout, g, wr, br, seq, tm):
    t, d = x2.shape
    tpb = seq // tm
    row = lambda i: (i, 0)
    full = lambda a: pl.BlockSpec(a.shape, lambda i: (0,) * a.ndim)
    return pl.pallas_call(
        _post_kernel,
        out_shape=[jax.ShapeDtypeStruct((t, d), F32), jax.ShapeDtypeStruct((t * SLAB, LANES), F32),
                   jax.ShapeDtypeStruct((t, LANES), F32), jax.ShapeDtypeStruct((ROUTE_ROWS, t), F32),
                   jax.ShapeDtypeStruct((1, LANES), F32)],
        grid=(t // tm,),
        in_specs=[pl.BlockSpec((tm, d), row), pl.BlockSpec((tm, oa.shape[1]), row),
                  pl.BlockSpec((tm, 2 * d), row), pl.BlockSpec((tm, d), row),
                  pl.BlockSpec((1, 6, d), lambda i: (i // tpb, 0, 0)),
                  full(wba), full(wout), full(g), full(wr), full(br)],
        out_specs=[pl.BlockSpec((tm, d), row), pl.BlockSpec((tm * SLAB, LANES), row),
                   pl.BlockSpec((tm, LANES), row),
                   pl.BlockSpec((ROUTE_ROWS, tm), lambda i: (0, i)), pl.BlockSpec((1, LANES), lambda i: (0, 0))],
        scratch_shapes=[pltpu.VMEM((1, LANES), F32)],
        compiler_params=_cparams("arbitrary"),
        name="post_mix",
    )(ys, oa, gates, x2, mod3, wba, wout, g, wr, br)


SLAB = 8


def _to_slabs(ref, val):
    n = val.shape[0]
    for j in range(SLAB):
        ref[pl.ds(j, n, stride=SLAB), :] = val[:, j * LANES:(j + 1) * LANES]


def _from_slabs(ref, n):
    return jnp.concatenate([ref[pl.ds(j, n, stride=SLAB), :] for j in range(SLAB)], axis=1)


def _slab(flat_ref, r):
    return flat_ref.at[pl.ds(pl.multiple_of(r * SLAB, SLAB), SLAB), :]


def _dispatch_kernel(pad_lo_ref, pad_hi_ref, n_used_ref, pos_ref, u_ref, xs_ref, zero_ref, sem, pad_sem):
    tm = u_ref.shape[0] // SLAB
    tile = zero_ref.shape[0]
    n_tiles = xs_ref.shape[0] // tile

    def send(r, carry):
        for k in range(2):
            pltpu.make_async_copy(_slab(u_ref, r), xs_ref.at[pos_ref[k * tm + r]], sem).start(priority=k)
        return carry

    lax.fori_loop(0, tm, send, 0, unroll=8)

    @pl.when(pl.program_id(0) == pl.num_programs(0) - 1)
    def _():
        zero_ref[...] = jnp.zeros(zero_ref.shape, zero_ref.dtype)

        def pad_copies(e, wait):
            lo = pad_lo_ref[e]
            n = pad_hi_ref[e] - lo
            for b in range(tile.bit_length() - 1):
                size = 1 << b

                @pl.when((n >> b) & 1 == 1)
                def _():
                    cp = pltpu.make_async_copy(zero_ref.at[pl.ds(0, size)],
                                               xs_ref.at[pl.ds(lo + (n & (size - 1)), size)], pad_sem)
                    cp.wait() if wait else cp.start()

        def tile_copy(i):
            return pltpu.make_async_copy(zero_ref, xs_ref.at[pl.ds(i * tile, tile)], pad_sem)

        for wait in (False, True):
            @pl.loop(0, N_EXPERTS)
            def _(e):
                pad_copies(e, wait)

            @pl.loop(n_used_ref[0], n_tiles)
            def _(i):
                tile_copy(i).wait() if wait else tile_copy(i).start()

    for _ in range(2):
        pltpu.make_async_copy(zero_ref.at[pl.ds(0, tm)], xs_ref.at[pl.ds(0, tm)], sem).wait()


def _dispatch(pad_lo, pad_hi, n_used, pos, u2_flat, n_rows, tm, tile):
    t = u2_flat.shape[0] // SLAB
    return pl.pallas_call(
        _dispatch_kernel,
        out_shape=jax.ShapeDtypeStruct((n_rows, SLAB, LANES), F32),
        grid_spec=pltpu.PrefetchScalarGridSpec(
            num_scalar_prefetch=3,
            grid=(t // tm,),
            in_specs=[pl.BlockSpec((2 * tm,), lambda i, *_: (i,), memory_space=pltpu.SMEM),
                      pl.BlockSpec((tm * SLAB, LANES), lambda i, *_: (i, 0))],
            out_specs=pl.BlockSpec(memory_space=pl.ANY),
            scratch_shapes=[pltpu.VMEM((tile, SLAB, LANES), F32), pltpu.SemaphoreType.DMA,
                            pltpu.SemaphoreType.DMA]),
        compiler_params=_cparams("arbitrary"),
        name="moe_dispatch",
    )(pad_lo, pad_hi, n_used, pos, u2_flat)


def _expert_kernel(tile_e_ref, n_used_ref, x_ref, wg_ref, wu_ref, wd_ref, y_ref):
    used = pl.program_id(0) < n_used_ref[0]

    @pl.when(used)
    def _():
        x = _from_slabs(x_ref, x_ref.shape[0] // SLAB).astype(BF16)
        gate = _dot(x, wg_ref[0].astype(BF16))
        hdn = gate * jax.nn.sigmoid(gate) * _dot(x, wu_ref[0].astype(BF16))
        _to_slabs(y_ref, _dot(hdn.astype(BF16), wd_ref[0].astype(BF16)))

    @pl.when(jnp.logical_not(used))
    def _():
        y_ref[...] = jnp.zeros(y_ref.shape, y_ref.dtype)


def _experts(tile_e, n_used, xs_flat, wg, wu, wd, tm):
    _, d, de = wg.shape
    by_tile = lambda i, te, nu: (te[i], 0, 0)
    return pl.pallas_call(
        _expert_kernel,
        out_shape=jax.ShapeDtypeStruct(xs_flat.shape, F32),
        grid_spec=pltpu.PrefetchScalarGridSpec(
            num_scalar_prefetch=2,
            grid=(xs_flat.shape[0] // (tm * SLAB),),
            in_specs=[pl.BlockSpec((tm * SLAB, LANES), lambda i, te, nu: (jnp.minimum(i, nu[0] - 1), 0)),
                      pl.BlockSpec((1, d, de), by_tile), pl.BlockSpec((1, d, de), by_tile),
                      pl.BlockSpec((1, de, d), by_tile)],
            out_specs=pl.BlockSpec((tm * SLAB, LANES), lambda i, te, nu: (i, 0))),
        compiler_params=_cparams("arbitrary"),
        name="moe_experts",
    )(tile_e, n_used, xs_flat, wg, wu, wd)


def _combine_kernel(pos_ref, ys_ref, z_ref, h_ref, mod_ref, g_ref, o_ref, buf_ref, sem, *, final_norm):
    tm = h_ref.shape[0]

    def fetch(r, carry):
        for k in range(2):
            pltpu.make_async_copy(ys_ref.at[pos_ref[k * tm + r]], _slab(buf_ref.at[k], r), sem).start(priority=k)
        return carry

    lax.fori_loop(0, tm, fetch, 0, unroll=8)
    for k in range(2):
        pltpu.make_async_copy(buf_ref.at[k], buf_ref.at[k], sem).wait()
    z = z_ref[...]
    ffn = z[:, 4:5] * _from_slabs(buf_ref.at[0], tm) + z[:, 5:6] * _from_slabs(buf_ref.at[1], tm)
    h = h_ref[...] + mod_ref[0][5:6, :] * ffn
    o_ref[...] = _rms(h, g_ref[...]) if final_norm else h


def _combine(pos, ys, z, h1, mod3, g_final, seq, tm, final_norm):
    t, d = h1.shape
    tpb = seq // tm
    row = lambda i: (i, 0)
    return pl.pallas_call(
        functools.partial(_combine_kernel, final_norm=final_norm),
        out_shape=jax.ShapeDtypeStruct((t, d), F32),
        grid=(t // tm,),
        in_specs=[pl.BlockSpec((2 * tm,), lambda i: (i,), memory_space=pltpu.SMEM),
                  pl.BlockSpec(memory_space=pl.ANY),
                  pl.BlockSpec((tm, LANES), row), pl.BlockSpec((tm, d), row),
                  pl.BlockSpec((1, 6, d), lambda i: (i // tpb, 0, 0)),
                  pl.BlockSpec((1, d), lambda i: (0, 0))],
        out_specs=pl.BlockSpec((tm, d), row),
        scratch_shapes=[pltpu.VMEM((2, tm * SLAB, LANES), F32), pltpu.SemaphoreType.DMA],
        compiler_params=_cparams("arbitrary"),
        name="moe_combine",
    )(pos, ys, z, h1, mod3, g_final)


def _route_tables(cnt, zt, n_tiles, tm_tok, tile):
    i32 = jnp.int32
    counts = cnt[0, :N_EXPERTS].astype(i32)
    padded = (counts + tile - 1) // tile * tile
    ends = jnp.cumsum(padded)
    starts = ends - padded
    n_used = (ends[-1] // tile).reshape(1)
    tile_row = jnp.minimum(jnp.arange(n_tiles, dtype=i32), n_used - 1) * tile
    tile_e = jnp.minimum(jnp.sum(tile_row[:, None] >= ends[None, :], axis=1), N_EXPERTS - 1).astype(i32)
    pos1 = jnp.take(starts, zt[0].astype(i32)) + zt[2].astype(i32)
    pos2 = jnp.take(starts, zt[1].astype(i32)) + zt[3].astype(i32)
    pos = jnp.stack([pos1.reshape(-1, tm_tok), pos2.reshape(-1, tm_tok)], axis=1).reshape(-1)
    return starts + counts, ends, pos, tile_e, n_used.astype(i32)


def kernel(x, c, w_ada, b_ada, g_norm_mix, w_in, b_in, s5_a_re, s5_a_im, s5_b_re, s5_b_im, s5_c_re, s5_c_im, s5_d, s5_log_dt, w_glu, b_glu, lambda_q1, lambda_k1, lambda_q2, lambda_k2, g_subln, w_br_ssm, w_br_attn, w_out, g_norm_ffn, w_router_grp, b_router_grp, w_router_exp, b_router_exp, w_exp_gate, w_exp_up, w_exp_down, g_final):
    nb, seq, d = x.shape
    depth = w_ada.shape[0]
    s5w = s5_d.shape[1]
    daw = w_br_attn.shape[1]
    ng = s5w // S5_GROUP
    nk = seq // S5_STEP
    assert seq % S5_TILE == 0 and ng % 2 == 0
    widths = (daw, daw, daw, 2 * d)
    tm = 512
    bq = 512
    row = lambda a: a.reshape(1, -1)
    col = lambda a: a.reshape(-1, 1)

    h = x.reshape(nb * seq, d)
    for l in range(depth):
        lambda_init = 0.8 - 0.6 * math.exp(-0.3 * l)
        mod3 = _ada_mod(c, w_ada[l], row(b_ada[l])).reshape(nb, 6, d)

        w_in_b = w_in[l].astype(BF16)
        q, k, v, gates = _in_proj(h, mod3, row(g_norm_mix[l]), w_in_b[:, s5w:], row(b_in[l][s5w:]),
                                  seq, widths, tm)

        u_t = _s5_in(h, mod3, row(g_norm_mix[l]), w_in_b[:, :s5w].T, col(b_in[l][:s5w]), seq, S5_TILE)
        ops = _s5_prep(s5_a_re[l], s5_a_im[l], s5_log_dt[l], s5_b_re[l], s5_b_im[l], s5_c_re[l], s5_c_im[l],
                       s5_d[l])
        y_t = _s5_core(u_t, ops, nb, nk)
        ys = _s5_out(y_t, w_glu[l].T.astype(BF16), col(b_glu[l]), w_br_ssm[l].astype(BF16), S5_TILE)

        as3 = lambda a: a.reshape(nb, seq, daw)
        oa = _diff_attn(as3(q), as3(k), as3(v), row(lambda_q1[l]), row(lambda_k1[l]), row(lambda_q2[l]),
                        row(lambda_k2[l]), row(g_subln[l]), lambda_init, bq).reshape(nb * seq, daw)

        w_router = jnp.zeros((d, LANES), F32)
        w_router = w_router.at[:, :N_EXPERTS].set(w_router_exp[l])
        w_router = w_router.at[:, N_EXPERTS:N_EXPERTS + N_GROUPS].set(w_router_grp[l])
        b_router = jnp.zeros((1, LANES), F32)
        b_router = b_router.at[0, :N_EXPERTS].set(b_router_exp[l])
        b_router = b_router.at[0, N_EXPERTS:N_EXPERTS + N_GROUPS].set(b_router_grp[l])
        wr_hi = lax.bitcast_convert_type(
            lax.bitcast_convert_type(w_router, jnp.uint32) & jnp.uint32(0xFFFF0000), F32)
        wr_split = jnp.stack([wr_hi.astype(BF16), (w_router - wr_hi).astype(BF16)])
        h1, u2, z, zt, cnt = _post_mix(ys, oa, gates, h, mod3, w_br_attn[l].astype(BF16),
                                       w_out[l].astype(BF16), row(g_norm_ffn[l]), wr_split, b_router, seq, tm)

        n_tiles = (2 * nb * seq) // EXPERT_TILE + N_EXPERTS
        pad_lo, pad_hi, pos, tile_e, n_used = _route_tables(cnt, zt, n_tiles, tm, EXPERT_TILE)
        n_rows = n_tiles * EXPERT_TILE
        xs = _dispatch(pad_lo, pad_hi, n_used, pos, u2, n_rows, tm, EXPERT_TILE)
        ys_e = _experts(tile_e, n_used, xs.reshape(n_rows * SLAB, LANES), w_exp_gate[l], w_exp_up[l],
                        w_exp_down[l], EXPERT_TILE)
        h = _combine(pos, ys_e.reshape(n_rows, SLAB, LANES), z, h1, mod3, row(g_final), seq, tm,
                     final_norm=(l == depth - 1))
    return h.reshape(nb, seq, d)
```

```python
import functools
import math

import jax
import jax.numpy as jnp
from jax import lax
from jax.experimental import pallas as pl
from jax.experimental.pallas import tpu as pltpu

EPS = 1e-6
CHUNK = 64
S5_GROUP = 16
S5_STATE = 64
S5_STEP = 16
S5_TILE = 2048
DA_HEADS = 8
DA_HEAD_DIM = 64
N_GROUPS = 4
EXP_PER_GROUP = 8
N_EXPERTS = N_GROUPS * EXP_PER_GROUP
D_EXPERT = 256
EXPERT_TILE = 512
ROUTE_ROWS = 8
LANES = 128
Q_SCALE = DA_HEAD_DIM ** -0.5 * math.log2(math.e)
VMEM_LIMIT = 56 * 1024 * 1024

F32 = jnp.float32
BF16 = jnp.bfloat16
HIGHEST = lax.Precision.HIGHEST


def _cparams(*sem):
    return pltpu.CompilerParams(dimension_semantics=sem, vmem_limit_bytes=VMEM_LIMIT)


def _dot(a, b):
    return jnp.dot(a, b, preferred_element_type=F32)


def _dot_nt(a, b):
    return lax.dot_general(a, b, (((1,), (1,)), ((), ())), preferred_element_type=F32)


def _rms(x, g):
    return x * lax.rsqrt(jnp.mean(x * x, axis=-1, keepdims=True) + EPS) * g


def _ada_kernel(c_ref, w_ref, b_ref, o_ref):
    c = c_ref[...]
    cs = c * jax.nn.sigmoid(c)
    o_ref[...] = jnp.dot(cs, w_ref[...], preferred_element_type=F32, precision=HIGHEST) + b_ref[...]


def _ada_mod(c, w, b):
    nb, d = c.shape
    n = w.shape[1]
    tn = n // 4
    return pl.pallas_call(
        _ada_kernel,
        out_shape=jax.ShapeDtypeStruct((nb, n), F32),
        grid=(n // tn,),
        in_specs=[pl.BlockSpec((nb, d), lambda j: (0, 0)),
                  pl.BlockSpec((d, tn), lambda j: (0, j)),
                  pl.BlockSpec((1, tn), lambda j: (0, j))],
        out_specs=pl.BlockSpec((nb, tn), lambda j: (0, j)),
        compiler_params=_cparams("parallel"),
        name="ada_mod",
    )(c, w, b)


def _modulated_norm(x_ref, mod_ref, g_ref):
    mod = mod_ref[0]
    return _rms(x_ref[...], g_ref[...]) * (1.0 + mod[1:2, :]) + mod[0:1, :]


def _inproj_kernel(x_ref, mod_ref, g_ref, w_ref, b_ref, q_ref, k_ref, v_ref, gt_ref, *, widths):
    ub = _modulated_norm(x_ref, mod_ref, g_ref).astype(BF16)
    qw, kw, vw, gw = widths
    o = 0
    q_ref[...] = ((_dot(ub, w_ref[:, o:o + qw]) + b_ref[:, o:o + qw]) * Q_SCALE).astype(BF16)
    o += qw
    k_ref[...] = (_dot(ub, w_ref[:, o:o + kw]) + b_ref[:, o:o + kw]).astype(BF16)
    o += kw
    v_ref[...] = (_dot(ub, w_ref[:, o:o + vw]) + b_ref[:, o:o + vw]).astype(BF16)
    o += vw
    gt_ref[...] = jax.nn.sigmoid(_dot(ub, w_ref[:, o:o + gw]) + b_ref[:, o:o + gw]).astype(BF16)


def _in_proj(x2, mod3, g, w, b, seq, widths, tm):
    t, d = x2.shape
    tpb = seq // tm
    row = lambda i: (i, 0)
    full = lambda a: pl.BlockSpec(a.shape, lambda i: (0, 0))
    return pl.pallas_call(
        functools.partial(_inproj_kernel, widths=widths),
        out_shape=[jax.ShapeDtypeStruct((t, wd), BF16) for wd in widths],
        grid=(t // tm,),
        in_specs=[pl.BlockSpec((tm, d), row), pl.BlockSpec((1, 6, d), lambda i: (i // tpb, 0, 0)),
                  full(g), full(w), full(b)],
        out_specs=[pl.BlockSpec((tm, wd), row) for wd in widths],
        compiler_params=_cparams("parallel"),
        name="in_proj",
    )(x2, mod3, g, w, b)


def _s5_in_kernel(x_ref, mod_ref, g_ref, wst_ref, bs_ref, ut_ref, slab_ref):
    u = _modulated_norm(x_ref, mod_ref, g_ref)
    nslab = slab_ref.shape[0]
    nk = x_ref.shape[0] // S5_STEP
    for j in range(nslab):
        slab_ref[j] = u[:, j * LANES:(j + 1) * LANES]
    for s in range(S5_STEP):
        us = jnp.concatenate([slab_ref[j, pl.ds(s, nk, stride=S5_STEP), :] for j in range(nslab)], axis=1)
        ut_ref[s] = (_dot_nt(wst_ref[...], us.astype(BF16)) + bs_ref[...]).astype(BF16)


def _s5_in(x2, mod3, g, wst, bs, seq, tm):
    t, d = x2.shape
    s5w = wst.shape[0]
    nkt = tm // S5_STEP
    tpb = seq // tm
    full = lambda a: pl.BlockSpec(a.shape, lambda i: (0, 0))
    return pl.pallas_call(
        _s5_in_kernel,
        out_shape=jax.ShapeDtypeStruct((S5_STEP, s5w, t // S5_STEP), BF16),
        grid=(t // tm,),
        in_specs=[pl.BlockSpec((tm, d), lambda i: (i, 0)), pl.BlockSpec((1, 6, d), lambda i: (i // tpb, 0, 0)),
                  full(g), full(wst), full(bs)],
        out_specs=pl.BlockSpec((S5_STEP, s5w, nkt), lambda i: (0, 0, i)),
        scratch_shapes=[pltpu.VMEM((d // LANES, tm, LANES), F32)],
        compiler_params=_cparams("parallel"),
        name="s5_in",
    )(x2, mod3, g, wst, bs)


def _s5_out_kernel(yt_ref, wglut_ref, bglu_ref, wbs_ref, o_ref, slab_ref):
    nslab = slab_ref.shape[0]
    nk = yt_ref.shape[2]
    for s in range(S5_STEP):
        yt = yt_ref[s]
        zt = _dot(wglut_ref[...], yt) + bglu_ref[...]
        yg = (yt.astype(F32) * jax.nn.sigmoid(zt)).T
        for j in range(nslab):
            slab_ref[j, pl.ds(s, nk, stride=S5_STEP), :] = yg[:, j * LANES:(j + 1) * LANES]
    yg = jnp.concatenate([slab_ref[j] for j in range(nslab)], axis=1)
    o_ref[...] = _dot(yg.astype(BF16), wbs_ref[...]).astype(BF16)


def _s5_out(y_t, wglut, bglu, wbs, tm):
    n, s5w, r = y_t.shape
    d = wbs.shape[1]
    nkt = tm // n
    full = lambda a: pl.BlockSpec(a.shape, lambda i: (0, 0))
    return pl.pallas_call(
        _s5_out_kernel,
        out_shape=jax.ShapeDtypeStruct((r * n, d), BF16),
        grid=(r // nkt,),
        in_specs=[pl.BlockSpec((n, s5w, nkt), lambda i: (0, 0, i)), full(wglut), full(bglu), full(wbs)],
        out_specs=pl.BlockSpec((tm, d), lambda i: (i, 0)),
        scratch_shapes=[pltpu.VMEM((s5w // LANES, tm, LANES), F32)],
        compiler_params=_cparams("parallel"),
        name="s5_out",
    )(y_t, wglut, bglu, wbs)


def _s5_prep_kernel(lrc_ref, lic_ref, dtc_ref, lrr_ref, lir_ref, dtr_ref, br_ref, bi_ref, cr_ref, ci_ref,
                    mt_ref, wtr_ref, wti_ref, vtr_ref, vti_ref, a16r_ref, a16i_ref):
    n, gc, ns = S5_STEP, S5_GROUP, S5_STATE
    w = n * gc
    iota = lambda shape, dim: lax.broadcasted_iota(jnp.int32, shape, dim)
    hdot = lambda a, b: jnp.dot(a, b, preferred_element_type=F32, precision=HIGHEST)

    def power(lr, li, dt, k):
        mag = jnp.exp(lr * dt * k)
        return mag * jnp.cos(li * dt * k), mag * jnp.sin(li * dt * k)

    lr, li, dt = lrc_ref[0], lic_ref[0], jnp.exp(dtc_ref[0])
    abr, abi = power(lr, li, dt, 1.0)
    den = lr * lr + li * li
    nr = abr - 1.0
    f_re = (nr * lr + abi * li) / den
    f_im = (abi * lr - nr * li) / den
    bb_re = f_re * br_ref[0] - f_im * bi_ref[0]
    bb_im = f_re * bi_ref[0] + f_im * br_ref[0]
    spread = jnp.where(iota((gc, 2 * w), 1) % gc == iota((gc, 2 * w), 0), 1.0, 0.0)
    bt_re, bt_im = hdot(bb_re, spread), hdot(bb_im, spread)
    k_in = (n - 1 - (iota((1, 2 * w), 1) % w) // gc).astype(F32)
    p_re, p_im = power(lr, li, dt, k_in)
    e_re = p_re * bt_re - p_im * bt_im
    e_im = p_re * bt_im + p_im * bt_re
    own = iota((2 * ns, 2 * w), 0) // ns == iota((2 * ns, 2 * w), 1) // w
    wtr_ref[0] = jnp.where(own, e_re, 0.0).astype(BF16)
    wti_ref[0] = jnp.where(own, e_im, 0.0).astype(BF16)

    lr_r, li_r, dt_r = lrr_ref[0], lir_ref[0], jnp.exp(dtr_ref[0])
    c_r, c_i = power(lr_r, li_r, dt_r, 1.0)
    for _ in range(n.bit_length() - 1):
        c_r, c_i = c_r * c_r - c_i * c_i, 2.0 * c_r * c_i
    a16r_ref[0], a16i_ref[0] = c_r, c_i
    k_out = ((iota((2 * w, 1), 0) % w) // gc + 1).astype(F32)
    q_re, q_im = power(lr_r, li_r, dt_r, k_out)
    c_re = jnp.concatenate([cr_ref[0]] * (2 * n), axis=0)
    c_im = jnp.concatenate([ci_ref[0]] * (2 * n), axis=0)
    own = iota((2 * w, 2 * ns), 0) // w == iota((2 * w, 2 * ns), 1) // ns
    vtr_ref[0] = jnp.where(own, c_re * q_re - c_im * q_im, 0.0).astype(BF16)
    vti_ref[0] = jnp.where(own, -(c_re * q_im + c_im * q_re), 0.0).astype(BF16)

    group_of_lane = iota((gc, 2 * ns), 1) // ns
    for h in range(2):
        mine = group_of_lane == h
        krev = (hdot(jnp.where(mine, cr_ref[0], 0.0), e_re[:, :w])
                - hdot(jnp.where(mine, ci_ref[0], 0.0), e_im[:, :w]))
        strip = jnp.concatenate([krev, jnp.zeros_like(krev)], axis=1)
        rows = []
        for t in range(n):
            shift = (n - 1 - t) * gc
            rows.append((pltpu.roll(strip, 2 * w - shift, 1) if shift else strip)[:, :w])
        mt_ref[0, h * w:(h + 1) * w, h * w:(h + 1) * w] = jnp.concatenate(rows, axis=0).astype(BF16)
        mt_ref[0, h * w:(h + 1) * w, (1 - h) * w:(2 - h) * w] = jnp.zeros((w, w), BF16)


def _s5_prep(a_re, a_im, log_dt, b_re, b_im, c_re, c_im, d_skip):
    ng, ns = a_re.shape
    gc = b_re.shape[-1]
    na = ng // 2
    w2 = 2 * S5_STEP * gc
    col = lambda a: a.reshape(na, 2 * ns, 1)
    row = lambda a: a.reshape(na, 1, 2 * ns)
    per_state = jnp.repeat(log_dt, ns)
    pair_c = lambda a: a.reshape(na, 2, gc, ns).transpose(0, 2, 1, 3).reshape(na, gc, 2 * ns)
    g3 = lambda g: (g, 0, 0)
    spec = lambda *shape: pl.BlockSpec((1,) + shape, g3)
    shp = lambda *shape, dt=BF16: jax.ShapeDtypeStruct((na,) + shape, dt)
    ops = pl.pallas_call(
        _s5_prep_kernel,
        out_shape=[shp(w2, w2), shp(2 * ns, w2), shp(2 * ns, w2), shp(w2, 2 * ns), shp(w2, 2 * ns),
                   shp(1, 2 * ns, dt=F32), shp(1, 2 * ns, dt=F32)],
        grid=(na,),
        in_specs=[spec(2 * ns, 1)] * 3 + [spec(1, 2 * ns)] * 3 + [spec(2 * ns, gc)] * 2 + [spec(gc, 2 * ns)] * 2,
        out_specs=[spec(w2, w2), spec(2 * ns, w2), spec(2 * ns, w2), spec(w2, 2 * ns), spec(w2, 2 * ns),
                   spec(1, 2 * ns), spec(1, 2 * ns)],
        compiler_params=_cparams("parallel"),
        name="s5_prep",
    )(col(a_re), col(a_im), col(per_state), row(a_re), row(a_im), row(per_state),
      b_re.reshape(na, 2 * ns, gc), b_im.reshape(na, 2 * ns, gc), pair_c(c_re), pair_c(c_im))
    d_col = jnp.broadcast_to(d_skip.reshape(na, 2, 1, gc), (na, 2, S5_STEP, gc)).reshape(na, w2, 1)
    return (*ops, d_col)


def _s5_core_kernel(u_ref, m_ref, wr_ref, wi_ref, vr_ref, vi_ref, ar_ref, ai_ref, d_ref, o_ref,
                    sr_ref, si_ref, xr_ref, xi_ref, *, nb, nk):
    n, gw, r = u_ref.shape
    gc = gw // 2
    u = jnp.concatenate([u_ref[:, h * gc:(h + 1) * gc, :].reshape(n * gc, r) for h in range(2)], axis=0)
    sr_ref[...] = _dot(wr_ref[0], u).T
    si_ref[...] = _dot(wi_ref[0], u).T
    a_r = ar_ref[0]
    a_i = ai_ref[0]

    def step(k, carry):
        out = []
        for b in range(nb):
            x_r, x_i = carry[2 * b], carry[2 * b + 1]
            row = pl.ds(b * nk + k, 1)
            xr_ref[row, :] = x_r
            xi_ref[row, :] = x_i
            out.append(a_r * x_r - a_i * x_i + sr_ref[row, :])
            out.append(a_r * x_i + a_i * x_r + si_ref[row, :])
        return tuple(out)

    zero = jnp.zeros((1, xr_ref.shape[1]), F32)
    lax.fori_loop(0, nk, step, (zero,) * (2 * nb), unroll=4)
    y = (_dot(m_ref[0], u) + _dot(vr_ref[0], xr_ref[...].T.astype(BF16))
         + _dot(vi_ref[0], xi_ref[...].T.astype(BF16)))
    y = y + d_ref[0] * u.astype(F32)
    y = jax.nn.gelu(y).astype(BF16)
    for h in range(2):
        o_ref[:, h * gc:(h + 1) * gc, :] = y[h * n * gc:(h + 1) * n * gc].reshape(n, gc, r)


def _s5_core(u_t, ops, nb, nk):
    m, wr, wi, vr, vi, ar, ai, d_col = ops
    npair, w2, _ = m.shape
    p2 = wr.shape[1]
    n, _, r = u_t.shape
    gw = w2 // n
    g3 = lambda g: (g, 0, 0)
    data = pl.BlockSpec((n, gw, r), lambda g: (0, g, 0))
    return pl.pallas_call(
        functools.partial(_s5_core_kernel, nb=nb, nk=nk),
        out_shape=jax.ShapeDtypeStruct(u_t.shape, BF16),
        grid=(npair,),
        in_specs=[data, pl.BlockSpec((1, w2, w2), g3),
                  pl.BlockSpec((1, p2, w2), g3), pl.BlockSpec((1, p2, w2), g3),
                  pl.BlockSpec((1, w2, p2), g3), pl.BlockSpec((1, w2, p2), g3),
                  pl.BlockSpec((1, 1, p2), g3), pl.BlockSpec((1, 1, p2), g3),
                  pl.BlockSpec((1, w2, 1), g3)],
        out_specs=data,
        scratch_shapes=[pltpu.VMEM((r, p2), F32)] * 4,
        compiler_params=_cparams("parallel"),
        name="s5_core",
    )(u_t, m, wr, wi, vr, vi, ar, ai, d_col)


def _attn_kernel(q_ref, k_ref, v_ref, lq1_ref, lk1_ref, lq2_ref, lk2_ref, g_ref, o_ref,
                 m_ref, l_ref, acc_ref, *, bq, lambda_init):
    qi = pl.program_id(2)
    q = q_ref[0]
    lane = lax.broadcasted_iota(jnp.int32, (1, 2 * DA_HEAD_DIM), 1)
    zero = jnp.zeros_like(q)
    qs = jnp.concatenate([jnp.where(lane < DA_HEAD_DIM, q, zero),
                          jnp.where(lane >= DA_HEAD_DIM, q, zero)], axis=0)
    m_ref[...] = jnp.full(m_ref.shape, -jnp.inf, F32)
    l_ref[...] = jnp.zeros(l_ref.shape, F32)
    acc_ref[...] = jnp.zeros(acc_ref.shape, F32)
    reps = bq // LANES

    def keys(j):
        return pl.ds(pl.multiple_of(j * bq, bq), bq)

    def scores(j):
        return _dot_nt(qs, k_ref[0, keys(j), :])

    def absorb(s, j, mask):
        if mask is not None:
            s = jnp.where(mask, s, -jnp.inf)
        m_old = m_ref[...]
        m_new = jnp.maximum(m_old, jnp.max(s, axis=-1, keepdims=True))
        alpha = jnp.exp2(m_old - m_new)
        p = jnp.exp2(s - jnp.concatenate([m_new] * reps, axis=1))
        l_ref[...] = alpha * l_ref[...] + jnp.sum(p, axis=-1, keepdims=True)
        acc_ref[...] = alpha * acc_ref[...] + _dot(p.astype(BF16), v_ref[0, keys(j), :])
        m_ref[...] = m_new

    def visible_block(j, carry):
        absorb(scores(j), j, None)
        return carry

    lax.fori_loop(0, qi, visible_block, 0)
    r_chunk = (lax.broadcasted_iota(jnp.int32, (2 * bq, bq), 0) % bq) // CHUNK
    c_chunk = lax.broadcasted_iota(jnp.int32, (2 * bq, bq), 1) // CHUNK
    absorb(scores(qi), qi, c_chunk <= r_chunk)

    lam = (jnp.exp(jnp.sum(lq1_ref[...] * lk1_ref[...], axis=-1, keepdims=True))
           - jnp.exp(jnp.sum(lq2_ref[...] * lk2_ref[...], axis=-1, keepdims=True)) + lambda_init)
    o_all = acc_ref[...] / l_ref[...]
    o = o_all[:bq] - lam * o_all[bq:]
    o_ref[0] = (_rms(o, g_ref[...]) * (1.0 - lambda_init)).astype(BF16)


def _diff_attn(q3, k3, v3, lq1, lk1, lq2, lk2, g_subln, lambda_init, bq):
    nb, seq, _ = q3.shape
    hw = 2 * DA_HEAD_DIM
    lam_spec = pl.BlockSpec((1, DA_HEAD_DIM), lambda b, h, i: (0, 0))
    kv_spec = pl.BlockSpec((1, seq, hw), lambda b, h, i: (b, 0, h))
    return pl.pallas_call(
        functools.partial(_attn_kernel, bq=bq, lambda_init=lambda_init),
        out_shape=jax.ShapeDtypeStruct(q3.shape, BF16),
        grid=(nb, DA_HEADS, seq // bq),
        in_specs=[pl.BlockSpec((1, bq, hw), lambda b, h, i: (b, i, h)), kv_spec, kv_spec,
                  lam_spec, lam_spec, lam_spec, lam_spec,
                  pl.BlockSpec((1, hw), lambda b, h, i: (0, 0))],
        out_specs=pl.BlockSpec((1, bq, hw), lambda b, h, i: (b, i, h)),
        scratch_shapes=[pltpu.VMEM((2 * bq, hw), F32)] * 3,
        compiler_params=_cparams("parallel", "parallel", "parallel"),
        name="diff_attn",
    )(q3, k3, v3, lq1, lk1, lq2, lk2, g_subln)


def _post_kernel(ys_ref, oa_ref, gt_ref, x_ref, mod_ref, wba_ref, wout_ref, g_ref, wr_ref, br_ref,
                 h_ref, u2_ref, z_ref, zt_ref, cnt_ref, base_ref):
    d = x_ref.shape[1]

    @pl.when(pl.program_id(0) == 0)
    def _():
        base_ref[...] = jnp.zeros(base_ref.shape, F32)

    mod = mod_ref[0]
    y_att = _dot(oa_ref[...], wba_ref[...])
    gates = gt_ref[...].astype(F32)
    mix_in = gates[:, :d] * ys_ref[...].astype(F32) + gates[:, d:] * y_att
    mix = _dot(mix_in.astype(BF16), wout_ref[...])
    h = x_ref[...] + mod[2:3, :] * mix
    h_ref[...] = h
    u2 = _rms(h, g_ref[...]) * (1.0 + mod[4:5, :]) + mod[3:4, :]
    _to_slabs(u2_ref, u2)

    u2_hi = pltpu.bitcast(pltpu.bitcast(u2, jnp.uint32) & jnp.uint32(0xFFFF0000), F32)
    u2_lo = (u2 - u2_hi).astype(BF16)
    u2_hi = u2_hi.astype(BF16)
    logits = (_dot(u2_hi, wr_ref[0]) + _dot(u2_hi, wr_ref[1]) + _dot(u2_lo, wr_ref[0])) + br_ref[...]
    lane = lax.broadcasted_iota(jnp.int32, logits.shape, 1)
    neg = jnp.full_like(logits, -jnp.inf)
    big = jnp.full_like(lane, LANES)
    is_grp = (lane >= N_EXPERTS) & (lane < N_EXPERTS + N_GROUPS)
    glog = jnp.where(is_grp, logits, neg)
    gmax = jnp.max(glog, axis=-1, keepdims=True)
    gp = 1.0 / jnp.sum(jnp.exp(glog - gmax), axis=-1, keepdims=True)
    gi = jnp.min(jnp.where(glog == gmax, lane, big), axis=-1, keepdims=True) - N_EXPERTS
    in_grp = (lane < N_EXPERTS) & ((lane // EXP_PER_GROUP) == gi)
    elog = jnp.where(in_grp, logits, neg)
    v1 = jnp.max(elog, axis=-1, keepdims=True)
    i1 = jnp.min(jnp.where(elog == v1, lane, big), axis=-1, keepdims=True)
    elog2 = jnp.where(lane == i1, neg, elog)
    v2 = jnp.max(elog2, axis=-1, keepdims=True)
    i2 = jnp.min(jnp.where(elog2 == v2, lane, big), axis=-1, keepdims=True)
    e2 = jnp.exp(v2 - v1)
    w1 = gp / (1.0 + e2)
    w2 = gp * e2 / (1.0 + e2)
    tm = logits.shape[0]
    hot = jnp.where((lane == i1) | (lane == i2), 1.0, 0.0)
    earlier = (lax.broadcasted_iota(jnp.int32, (tm, tm), 1) < lax.broadcasted_iota(jnp.int32, (tm, tm), 0))
    before = _dot(jnp.where(earlier, 1.0, 0.0).astype(BF16), hot.astype(BF16)) + base_ref[...]
    rank1 = jnp.sum(jnp.where(lane == i1, before, 0.0), axis=-1, keepdims=True)
    rank2 = jnp.sum(jnp.where(lane == i2, before, 0.0), axis=-1, keepdims=True)
    base_ref[...] += jnp.sum(hot, axis=0, keepdims=True)
    cnt_ref[...] = base_ref[...]
    z = jnp.zeros_like(logits)
    for k, val in enumerate((i1.astype(F32), i2.astype(F32), rank1, rank2, w1, w2)):
        z = jnp.where(lane == k, val, z)
    z_ref[...] = z
    zt_ref[...] = z.T[:ROUTE_ROWS, :]


def _post_mix(ys, oa, gates, x2, mod3, wba, wout, g, wr, br, seq, tm):
    t, d = x2.shape
    tpb = seq // tm
    row = lambda i: (i, 0)
    full = lambda a: pl.BlockSpec(a.shape, lambda i: (0,) * a.ndim)
    return pl.pallas_call(
        _post_kernel,
        out_shape=[jax.ShapeDtypeStruct((t, d), F32), jax.ShapeDtypeStruct((t * SLAB, LANES), F32),
                   jax.ShapeDtypeStruct((t, LANES), F32), jax.ShapeDtypeStruct((ROUTE_ROWS, t), F32),
                   jax.ShapeDtypeStruct((1, LANES), F32)],
        grid=(t // tm,),
        in_specs=[pl.BlockSpec((tm, d), row), pl.BlockSpec((tm, oa.shape[1]), row),
                  pl.BlockSpec((tm, 2 * d), row), pl.BlockSpec((tm, d), row),
                  pl.BlockSpec((1, 6, d), lambda i: (i // tpb, 0, 0)),
                  full(wba), full(wout), full(g), full(wr), full(br)],
        out_specs=[pl.BlockSpec((tm, d), row), pl.BlockSpec((tm * SLAB, LANES), row),
                   pl.BlockSpec((tm, LANES), row),
                   pl.BlockSpec((ROUTE_ROWS, tm), lambda i: (0, i)), pl.BlockSpec((1, LANES), lambda i: (0, 0))],
        scratch_shapes=[pltpu.VMEM((1, LANES), F32)],
        compiler_params=_cparams("arbitrary"),
        name="post_mix",
    )(ys, oa, gates, x2, mod3, wba, wout, g, wr, br)


SLAB = 8


def _to_slabs(ref, val):
    n = val.shape[0]
    for j in range(SLAB):
        ref[pl.ds(j, n, stride=SLAB), :] = val[:, j * LANES:(j + 1) * LANES]


def _from_slabs(ref, n):
    return jnp.concatenate([ref[pl.ds(j, n, stride=SLAB), :] for j in range(SLAB)], axis=1)


def _slab(flat_ref, r):
    return flat_ref.at[pl.ds(pl.multiple_of(r * SLAB, SLAB), SLAB), :]


def _dispatch_kernel(pad_lo_ref, pad_hi_ref, n_used_ref, pos_ref, u_ref, xs_ref, zero_ref, sems, pad_sem):
    tm = pos_ref.shape[0] // 2
    tile = zero_ref.shape[0]
    n_tiles = xs_ref.shape[0] // tile
    i = pl.program_id(0)
    last = pl.num_programs(0) - 1

    def send(r, carry):
        for k in range(2):
            pltpu.make_async_copy(u_ref.at[i * tm + r], xs_ref.at[pos_ref[k * tm + r]],
                                  sems.at[i % 2]).start(priority=k)
        return carry

    lax.fori_loop(0, tm, send, 0, unroll=8)

    def wait_tile(slot):
        for _ in range(2):
            pltpu.make_async_copy(zero_ref.at[pl.ds(0, tm)], xs_ref.at[pl.ds(0, tm)], sems.at[slot]).wait()

    @pl.when(i > 0)
    def _():
        wait_tile((i - 1) % 2)

    @pl.when(i == last)
    def _():
        wait_tile(i % 2)
        zero_ref[...] = jnp.zeros(zero_ref.shape, zero_ref.dtype)

        def pad_copies(e, wait):
            lo = pad_lo_ref[e]
            n = pad_hi_ref[e] - lo
            for b in range(tile.bit_length() - 1):
                size = 1 << b

                @pl.when((n >> b) & 1 == 1)
                def _():
                    cp = pltpu.make_async_copy(zero_ref.at[pl.ds(0, size)],
                                               xs_ref.at[pl.ds(lo + (n & (size - 1)), size)], pad_sem)
                    cp.wait() if wait else cp.start()

        def tile_copy(i):
            return pltpu.make_async_copy(zero_ref, xs_ref.at[pl.ds(i * tile, tile)], pad_sem)

        for wait in (False, True):
            @pl.loop(0, N_EXPERTS)
            def _(e):
                pad_copies(e, wait)

            @pl.loop(n_used_ref[0], n_tiles)
            def _(j):
                tile_copy(j).wait() if wait else tile_copy(j).start()


def _dispatch(pad_lo, pad_hi, n_used, pos, u2_slabs, n_rows, tm, tile):
    t = u2_slabs.shape[0]
    return pl.pallas_call(
        _dispatch_kernel,
        out_shape=jax.ShapeDtypeStruct((n_rows, SLAB, LANES), F32),
        grid_spec=pltpu.PrefetchScalarGridSpec(
            num_scalar_prefetch=3,
            grid=(t // tm,),
            in_specs=[pl.BlockSpec((2 * tm,), lambda i, *_: (i,), memory_space=pltpu.SMEM),
                      pl.BlockSpec(memory_space=pl.ANY)],
            out_specs=pl.BlockSpec(memory_space=pl.ANY),
            scratch_shapes=[pltpu.VMEM((tile, SLAB, LANES), F32), pltpu.SemaphoreType.DMA((2,)),
                            pltpu.SemaphoreType.DMA]),
        compiler_params=_cparams("arbitrary"),
        name="moe_dispatch",
    )(pad_lo, pad_hi, n_used, pos, u2_slabs)


def _expert_kernel(tile_e_ref, n_used_ref, x_ref, wg_ref, wu_ref, wd_ref, y_ref):
    used = pl.program_id(0) < n_used_ref[0]

    @pl.when(used)
    def _():
        x = _from_slabs(x_ref, x_ref.shape[0] // SLAB).astype(BF16)
        gate = _dot(x, wg_ref[0].astype(BF16))
        hdn = gate * jax.nn.sigmoid(gate) * _dot(x, wu_ref[0].astype(BF16))
        _to_slabs(y_ref, _dot(hdn.astype(BF16), wd_ref[0].astype(BF16)))

    @pl.when(jnp.logical_not(used))
    def _():
        y_ref[...] = jnp.zeros(y_ref.shape, y_ref.dtype)


def _experts(tile_e, n_used, xs_flat, wg, wu, wd, tm):
    _, d, de = wg.shape
    by_tile = lambda i, te, nu: (te[i], 0, 0)
    return pl.pallas_call(
        _expert_kernel,
        out_shape=jax.ShapeDtypeStruct(xs_flat.shape, F32),
        grid_spec=pltpu.PrefetchScalarGridSpec(
            num_scalar_prefetch=2,
            grid=(xs_flat.shape[0] // (tm * SLAB),),
            in_specs=[pl.BlockSpec((tm * SLAB, LANES), lambda i, te, nu: (jnp.minimum(i, nu[0] - 1), 0)),
                      pl.BlockSpec((1, d, de), by_tile), pl.BlockSpec((1, d, de), by_tile),
                      pl.BlockSpec((1, de, d), by_tile)],
            out_specs=pl.BlockSpec((tm * SLAB, LANES), lambda i, te, nu: (i, 0))),
        compiler_params=_cparams("arbitrary"),
        name="moe_experts",
    )(tile_e, n_used, xs_flat, wg, wu, wd)


def _combine_kernel(pos_ref, pos_next_ref, ys_ref, z_ref, h_ref, mod_ref, g_ref, o_ref, buf_ref, sems,
                    *, final_norm):
    tm = h_ref.shape[0]
    i = pl.program_id(0)

    def fetch(p_ref, slot):
        def one(r, carry):
            for k in range(2):
                pltpu.make_async_copy(ys_ref.at[p_ref[k * tm + r]], _slab(buf_ref.at[slot, k], r),
                                      sems.at[slot]).start(priority=k)
            return carry

        lax.fori_loop(0, tm, one, 0, unroll=8)

    @pl.when(i == 0)
    def _():
        fetch(pos_ref, 0)

    @pl.when(i + 1 < pl.num_programs(0))
    def _():
        fetch(pos_next_ref, (i + 1) % 2)

    slot = i % 2
    for k in range(2):
        pltpu.make_async_copy(buf_ref.at[slot, k], buf_ref.at[slot, k], sems.at[slot]).wait()
    z = z_ref[...]
    ffn = z[:, 4:5] * _from_slabs(buf_ref.at[slot, 0], tm) + z[:, 5:6] * _from_slabs(buf_ref.at[slot, 1], tm)
    h = h_ref[...] + mod_ref[0][5:6, :] * ffn
    o_ref[...] = _rms(h, g_ref[...]) if final_norm else h


def _combine(pos, ys, z, h1, mod3, g_final, seq, tm, final_norm):
    t, d = h1.shape
    tpb = seq // tm
    row = lambda i: (i, 0)
    return pl.pallas_call(
        functools.partial(_combine_kernel, final_norm=final_norm),
        out_shape=jax.ShapeDtypeStruct((t, d), F32),
        grid=(t // tm,),
        in_specs=[pl.BlockSpec((2 * tm,), lambda i: (i,), memory_space=pltpu.SMEM),
                  pl.BlockSpec((2 * tm,), lambda i: (jnp.minimum(i + 1, t // tm - 1),), memory_space=pltpu.SMEM),
                  pl.BlockSpec(memory_space=pl.ANY),
                  pl.BlockSpec((tm, LANES), row), pl.BlockSpec((tm, d), row),
                  pl.BlockSpec((1, 6, d), lambda i: (i // tpb, 0, 0)),
                  pl.BlockSpec((1, d), lambda i: (0, 0))],
        out_specs=pl.BlockSpec((tm, d), row),
        scratch_shapes=[pltpu.VMEM((2, 2, tm * SLAB, LANES), F32), pltpu.SemaphoreType.DMA((2,))],
        compiler_params=_cparams("arbitrary"),
        name="moe_combine",
    )(pos, pos, ys, z, h1, mod3, g_final)


def _route_tables(cnt, zt, n_tiles, tm_tok, tile):
    i32 = jnp.int32
    counts = cnt[0, :N_EXPERTS].astype(i32)
    padded = (counts + tile - 1) // tile * tile
    ends = jnp.cumsum(padded)
    starts = ends - padded
    n_used = (ends[-1] // tile).reshape(1)
    tile_row = jnp.minimum(jnp.arange(n_tiles, dtype=i32), n_used - 1) * tile
    tile_e = jnp.minimum(jnp.sum(tile_row[:, None] >= ends[None, :], axis=1), N_EXPERTS - 1).astype(i32)
    pos1 = jnp.take(starts, zt[0].astype(i32)) + zt[2].astype(i32)
    pos2 = jnp.take(starts, zt[1].astype(i32)) + zt[3].astype(i32)
    pos = jnp.stack([pos1.reshape(-1, tm_tok), pos2.reshape(-1, tm_tok)], axis=1).reshape(-1)
    return starts + counts, ends, pos, tile_e, n_used.astype(i32)


def kernel(x, c, w_ada, b_ada, g_norm_mix, w_in, b_in, s5_a_re, s5_a_im, s5_b_re, s5_b_im, s5_c_re, s5_c_im, s5_d, s5_log_dt, w_glu, b_glu, lambda_q1, lambda_k1, lambda_q2, lambda_k2, g_subln, w_br_ssm, w_br_attn, w_out, g_norm_ffn, w_router_grp, b_router_grp, w_router_exp, b_router_exp, w_exp_gate, w_exp_up, w_exp_down, g_final):
    nb, seq, d = x.shape
    depth = w_ada.shape[0]
    s5w = s5_d.shape[1]
    daw = w_br_attn.shape[1]
    ng = s5w // S5_GROUP
    nk = seq // S5_STEP
    assert seq % S5_TILE == 0 and ng % 2 == 0
    widths = (daw, daw, daw, 2 * d)
    tm = 512
    bq = 512
    row = lambda a: a.reshape(1, -1)
    col = lambda a: a.reshape(-1, 1)

    h = x.reshape(nb * seq, d)
    for l in range(depth):
        lambda_init = 0.8 - 0.6 * math.exp(-0.3 * l)
        mod3 = _ada_mod(c, w_ada[l], row(b_ada[l])).reshape(nb, 6, d)

        w_in_b = w_in[l].astype(BF16)
        q, k, v, gates = _in_proj(h, mod3, row(g_norm_mix[l]), w_in_b[:, s5w:], row(b_in[l][s5w:]),
                                  seq, widths, tm)

        u_t = _s5_in(h, mod3, row(g_norm_mix[l]), w_in_b[:, :s5w].T, col(b_in[l][:s5w]), seq, S5_TILE)
        ops = _s5_prep(s5_a_re[l], s5_a_im[l], s5_log_dt[l], s5_b_re[l], s5_b_im[l], s5_c_re[l], s5_c_im[l],
                       s5_d[l])
        y_t = _s5_core(u_t, ops, nb, nk)
        ys = _s5_out(y_t, w_glu[l].T.astype(BF16), col(b_glu[l]), w_br_ssm[l].astype(BF16), S5_TILE)

        as3 = lambda a: a.reshape(nb, seq, daw)
        oa = _diff_attn(as3(q), as3(k), as3(v), row(lambda_q1[l]), row(lambda_k1[l]), row(lambda_q2[l]),
                        row(lambda_k2[l]), row(g_subln[l]), lambda_init, bq).reshape(nb * seq, daw)

        w_router = jnp.zeros((d, LANES), F32)
        w_router = w_router.at[:, :N_EXPERTS].set(w_router_exp[l])
        w_router = w_router.at[:, N_EXPERTS:N_EXPERTS + N_GROUPS].set(w_router_grp[l])
        b_router = jnp.zeros((1, LANES), F32)
        b_router = b_router.at[0, :N_EXPERTS].set(b_router_exp[l])
        b_router = b_router.at[0, N_EXPERTS:N_EXPERTS + N_GROUPS].set(b_router_grp[l])
        wr_hi = lax.bitcast_convert_type(
            lax.bitcast_convert_type(w_router, jnp.uint32) & jnp.uint32(0xFFFF0000), F32)
        wr_split = jnp.stack([wr_hi.astype(BF16), (w_router - wr_hi).astype(BF16)])
        h1, u2, z, zt, cnt = _post_mix(ys, oa, gates, h, mod3, w_br_attn[l].astype(BF16),
                                       w_out[l].astype(BF16), row(g_norm_ffn[l]), wr_split, b_router, seq, tm)

        n_tiles = (2 * nb * seq) // EXPERT_TILE + N_EXPERTS
        pad_lo, pad_hi, pos, tile_e, n_used = _route_tables(cnt, zt, n_tiles, tm, EXPERT_TILE)
        n_rows = n_tiles * EXPERT_TILE
        xs = _dispatch(pad_lo, pad_hi, n_used, pos, u2.reshape(nb * seq, SLAB, LANES), n_rows, tm, EXPERT_TILE)
        ys_e = _experts(tile_e, n_used, xs.reshape(n_rows * SLAB, LANES), w_exp_gate[l], w_exp_up[l],
                        w_exp_down[l], EXPERT_TILE)
        h = _combine(pos, ys_e.reshape(n_rows, SLAB, LANES), z, h1, mod3, row(g_final), seq, tm,
                     final_norm=(l == depth - 1))
    return h.reshape(nb, seq, d)
```

```python
import functools
import math

import jax
import jax.numpy as jnp
from jax import lax
from jax.experimental import pallas as pl
from jax.experimental.pallas import tpu as pltpu

EPS = 1e-6
CHUNK = 64
S5_GROUP = 16
S5_STATE = 64
S5_STEP = 16
S5_TILE = 2048
DA_HEADS = 8
DA_HEAD_DIM = 64
N_GROUPS = 4
EXP_PER_GROUP = 8
N_EXPERTS = N_GROUPS * EXP_PER_GROUP
D_EXPERT = 256
EXPERT_TILE = 512
ROUTE_ROWS = 8
LANES = 128
Q_SCALE = DA_HEAD_DIM ** -0.5 * math.log2(math.e)
VMEM_LIMIT = 56 * 1024 * 1024

F32 = jnp.float32
BF16 = jnp.bfloat16
HIGHEST = lax.Precision.HIGHEST


def _cparams(*sem):
    return pltpu.CompilerParams(dimension_semantics=sem, vmem_limit_bytes=VMEM_LIMIT)


def _dot(a, b):
    return jnp.dot(a, b, preferred_element_type=F32)


def _dot_nt(a, b):
    return lax.dot_general(a, b, (((1,), (1,)), ((), ())), preferred_element_type=F32)


def _rms(x, g):
    return x * lax.rsqrt(jnp.mean(x * x, axis=-1, keepdims=True) + EPS) * g


def _ada_kernel(c_ref, w_ref, b_ref, o_ref):
    c = c_ref[...]
    cs = c * jax.nn.sigmoid(c)
    o_ref[...] = jnp.dot(cs, w_ref[...], preferred_element_type=F32, precision=HIGHEST) + b_ref[...]


def _ada_mod(c, w, b):
    nb, d = c.shape
    n = w.shape[1]
    tn = n // 4
    return pl.pallas_call(
        _ada_kernel,
        out_shape=jax.ShapeDtypeStruct((nb, n), F32),
        grid=(n // tn,),
        in_specs=[pl.BlockSpec((nb, d), lambda j: (0, 0)),
                  pl.BlockSpec((d, tn), lambda j: (0, j)),
                  pl.BlockSpec((1, tn), lambda j: (0, j))],
        out_specs=pl.BlockSpec((nb, tn), lambda j: (0, j)),
        compiler_params=_cparams("parallel"),
        name="ada_mod",
    )(c, w, b)


def _modulated_norm(x_ref, mod_ref, g_ref):
    mod = mod_ref[0]
    return _rms(x_ref[...], g_ref[...]) * (1.0 + mod[1:2, :]) + mod[0:1, :]


def _inproj_kernel(x_ref, mod_ref, g_ref, w_ref, b_ref, q_ref, k_ref, v_ref, gt_ref, *, widths):
    ub = _modulated_norm(x_ref, mod_ref, g_ref).astype(BF16)
    qw, kw, vw, gw = widths
    o = 0
    q_ref[...] = ((_dot(ub, w_ref[:, o:o + qw]) + b_ref[:, o:o + qw]) * Q_SCALE).astype(BF16)
    o += qw
    k_ref[...] = (_dot(ub, w_ref[:, o:o + kw]) + b_ref[:, o:o + kw]).astype(BF16)
    o += kw
    v_ref[...] = (_dot(ub, w_ref[:, o:o + vw]) + b_ref[:, o:o + vw]).astype(BF16)
    o += vw
    gt_ref[...] = jax.nn.sigmoid(_dot(ub, w_ref[:, o:o + gw]) + b_ref[:, o:o + gw]).astype(BF16)


def _in_proj(x2, mod3, g, w, b, seq, widths, tm):
    t, d = x2.shape
    tpb = seq // tm
    row = lambda i: (i, 0)
    full = lambda a: pl.BlockSpec(a.shape, lambda i: (0, 0))
    return pl.pallas_call(
        functools.partial(_inproj_kernel, widths=widths),
        out_shape=[jax.ShapeDtypeStruct((t, wd), BF16) for wd in widths],
        grid=(t // tm,),
        in_specs=[pl.BlockSpec((tm, d), row), pl.BlockSpec((1, 6, d), lambda i: (i // tpb, 0, 0)),
                  full(g), full(w), full(b)],
        out_specs=[pl.BlockSpec((tm, wd), row) for wd in widths],
        compiler_params=_cparams("parallel"),
        name="in_proj",
    )(x2, mod3, g, w, b)


def _s5_in_kernel(x_ref, mod_ref, g_ref, wst_ref, bs_ref, ut_ref, slab_ref):
    u = _modulated_norm(x_ref, mod_ref, g_ref)
    nslab = slab_ref.shape[0]
    nk = x_ref.shape[0] // S5_STEP
    for j in range(nslab):
        slab_ref[j] = u[:, j * LANES:(j + 1) * LANES]
    for s in range(S5_STEP):
        us = jnp.concatenate([slab_ref[j, pl.ds(s, nk, stride=S5_STEP), :] for j in range(nslab)], axis=1)
        ut_ref[s] = (_dot_nt(wst_ref[...], us.astype(BF16)) + bs_ref[...]).astype(BF16)


def _s5_in(x2, mod3, g, wst, bs, seq, tm):
    t, d = x2.shape
    s5w = wst.shape[0]
    nkt = tm // S5_STEP
    tpb = seq // tm
    full = lambda a: pl.BlockSpec(a.shape, lambda i: (0, 0))
    return pl.pallas_call(
        _s5_in_kernel,
        out_shape=jax.ShapeDtypeStruct((S5_STEP, s5w, t // S5_STEP), BF16),
        grid=(t // tm,),
        in_specs=[pl.BlockSpec((tm, d), lambda i: (i, 0)), pl.BlockSpec((1, 6, d), lambda i: (i // tpb, 0, 0)),
                  full(g), full(wst), full(bs)],
        out_specs=pl.BlockSpec((S5_STEP, s5w, nkt), lambda i: (0, 0, i)),
        scratch_shapes=[pltpu.VMEM((d // LANES, tm, LANES), F32)],
        compiler_params=_cparams("parallel"),
        name="s5_in",
    )(x2, mod3, g, wst, bs)


def _s5_out_kernel(yt_ref, wglut_ref, bglu_ref, wbs_ref, o_ref, slab_ref):
    nslab = slab_ref.shape[0]
    nk = yt_ref.shape[2]
    for s in range(S5_STEP):
        yt = yt_ref[s]
        zt = _dot(wglut_ref[...], yt) + bglu_ref[...]
        yg = (yt.astype(F32) * jax.nn.sigmoid(zt)).T
        for j in range(nslab):
            slab_ref[j, pl.ds(s, nk, stride=S5_STEP), :] = yg[:, j * LANES:(j + 1) * LANES]
    yg = jnp.concatenate([slab_ref[j] for j in range(nslab)], axis=1)
    o_ref[...] = _dot(yg.astype(BF16), wbs_ref[...]).astype(BF16)


def _s5_out(y_t, wglut, bglu, wbs, tm):
    n, s5w, r = y_t.shape
    d = wbs.shape[1]
    nkt = tm // n
    full = lambda a: pl.BlockSpec(a.shape, lambda i: (0, 0))
    return pl.pallas_call(
        _s5_out_kernel,
        out_shape=jax.ShapeDtypeStruct((r * n, d), BF16),
        grid=(r // nkt,),
        in_specs=[pl.BlockSpec((n, s5w, nkt), lambda i: (0, 0, i)), full(wglut), full(bglu), full(wbs)],
        out_specs=pl.BlockSpec((tm, d), lambda i: (i, 0)),
        scratch_shapes=[pltpu.VMEM((s5w // LANES, tm, LANES), F32)],
        compiler_params=_cparams("parallel"),
        name="s5_out",
    )(y_t, wglut, bglu, wbs)


def _s5_prep_kernel(lrc_ref, lic_ref, dtc_ref, lrr_ref, lir_ref, dtr_ref, br_ref, bi_ref, cr_ref, ci_ref,
                    mt_ref, wtr_ref, wti_ref, vtr_ref, vti_ref, a16r_ref, a16i_ref):
    n, gc, ns = S5_STEP, S5_GROUP, S5_STATE
    w = n * gc
    iota = lambda shape, dim: lax.broadcasted_iota(jnp.int32, shape, dim)
    hdot = lambda a, b: jnp.dot(a, b, preferred_element_type=F32, precision=HIGHEST)

    def power(lr, li, dt, k):
        mag = jnp.exp(lr * dt * k)
        return mag * jnp.cos(li * dt * k), mag * jnp.sin(li * dt * k)

    lr, li, dt = lrc_ref[0], lic_ref[0], jnp.exp(dtc_ref[0])
    abr, abi = power(lr, li, dt, 1.0)
    den = lr * lr + li * li
    nr = abr - 1.0
    f_re = (nr * lr + abi * li) / den
    f_im = (abi * lr - nr * li) / den
    bb_re = f_re * br_ref[0] - f_im * bi_ref[0]
    bb_im = f_re * bi_ref[0] + f_im * br_ref[0]
    spread = jnp.where(iota((gc, 2 * w), 1) % gc == iota((gc, 2 * w), 0), 1.0, 0.0)
    bt_re, bt_im = hdot(bb_re, spread), hdot(bb_im, spread)
    k_in = (n - 1 - (iota((1, 2 * w), 1) % w) // gc).astype(F32)
    p_re, p_im = power(lr, li, dt, k_in)
    e_re = p_re * bt_re - p_im * bt_im
    e_im = p_re * bt_im + p_im * bt_re
    own = iota((2 * ns, 2 * w), 0) // ns == iota((2 * ns, 2 * w), 1) // w
    wtr_ref[0] = jnp.where(own, e_re, 0.0).astype(BF16)
    wti_ref[0] = jnp.where(own, e_im, 0.0).astype(BF16)

    lr_r, li_r, dt_r = lrr_ref[0], lir_ref[0], jnp.exp(dtr_ref[0])
    c_r, c_i = power(lr_r, li_r, dt_r, 1.0)
    for _ in range(n.bit_length() - 1):
        c_r, c_i = c_r * c_r - c_i * c_i, 2.0 * c_r * c_i
    a16r_ref[0], a16i_ref[0] = c_r, c_i
    k_out = ((iota((2 * w, 1), 0) % w) // gc + 1).astype(F32)
    q_re, q_im = power(lr_r, li_r, dt_r, k_out)
    c_re = jnp.concatenate([cr_ref[0]] * (2 * n), axis=0)
    c_im = jnp.concatenate([ci_ref[0]] * (2 * n), axis=0)
    own = iota((2 * w, 2 * ns), 0) // w == iota((2 * w, 2 * ns), 1) // ns
    vtr_ref[0] = jnp.where(own, c_re * q_re - c_im * q_im, 0.0).astype(BF16)
    vti_ref[0] = jnp.where(own, -(c_re * q_im + c_im * q_re), 0.0).astype(BF16)

    group_of_lane = iota((gc, 2 * ns), 1) // ns
    for h in range(2):
        mine = group_of_lane == h
        krev = (hdot(jnp.where(mine, cr_ref[0], 0.0), e_re[:, :w])
                - hdot(jnp.where(mine, ci_ref[0], 0.0), e_im[:, :w]))
        strip = jnp.concatenate([krev, jnp.zeros_like(krev)], axis=1)
        rows = []
        for t in range(n):
            shift = (n - 1 - t) * gc
            rows.append((pltpu.roll(strip, 2 * w - shift, 1) if shift else strip)[:, :w])
        mt_ref[0, h * w:(h + 1) * w, h * w:(h + 1) * w] = jnp.concatenate(rows, axis=0).astype(BF16)
        mt_ref[0, h * w:(h + 1) * w, (1 - h) * w:(2 - h) * w] = jnp.zeros((w, w), BF16)


def _s5_prep(a_re, a_im, log_dt, b_re, b_im, c_re, c_im, d_skip):
    ng, ns = a_re.shape
    gc = b_re.shape[-1]
    na = ng // 2
    w2 = 2 * S5_STEP * gc
    col = lambda a: a.reshape(na, 2 * ns, 1)
    row = lambda a: a.reshape(na, 1, 2 * ns)
    per_state = jnp.repeat(log_dt, ns)
    pair_c = lambda a: a.reshape(na, 2, gc, ns).transpose(0, 2, 1, 3).reshape(na, gc, 2 * ns)
    g3 = lambda g: (g, 0, 0)
    spec = lambda *shape: pl.BlockSpec((1,) + shape, g3)
    shp = lambda *shape, dt=BF16: jax.ShapeDtypeStruct((na,) + shape, dt)
    ops = pl.pallas_call(
        _s5_prep_kernel,
        out_shape=[shp(w2, w2), shp(2 * ns, w2), shp(2 * ns, w2), shp(w2, 2 * ns), shp(w2, 2 * ns),
                   shp(1, 2 * ns, dt=F32), shp(1, 2 * ns, dt=F32)],
        grid=(na,),
        in_specs=[spec(2 * ns, 1)] * 3 + [spec(1, 2 * ns)] * 3 + [spec(2 * ns, gc)] * 2 + [spec(gc, 2 * ns)] * 2,
        out_specs=[spec(w2, w2), spec(2 * ns, w2), spec(2 * ns, w2), spec(w2, 2 * ns), spec(w2, 2 * ns),
                   spec(1, 2 * ns), spec(1, 2 * ns)],
        compiler_params=_cparams("parallel"),
        name="s5_prep",
    )(col(a_re), col(a_im), col(per_state), row(a_re), row(a_im), row(per_state),
      b_re.reshape(na, 2 * ns, gc), b_im.reshape(na, 2 * ns, gc), pair_c(c_re), pair_c(c_im))
    d_col = jnp.broadcast_to(d_skip.reshape(na, 2, 1, gc), (na, 2, S5_STEP, gc)).reshape(na, w2, 1)
    return (*ops, d_col)


def _s5_core_kernel(u_ref, m_ref, wr_ref, wi_ref, vr_ref, vi_ref, ar_ref, ai_ref, d_ref, o_ref,
                    sr_ref, si_ref, xr_ref, xi_ref, *, nb, nk):
    n, gw, r = u_ref.shape
    gc = gw // 2
    u = jnp.concatenate([u_ref[:, h * gc:(h + 1) * gc, :].reshape(n * gc, r) for h in range(2)], axis=0)
    sr_ref[...] = _dot(wr_ref[0], u).T
    si_ref[...] = _dot(wi_ref[0], u).T
    a_r = ar_ref[0]
    a_i = ai_ref[0]

    def step(k, carry):
        out = []
        for b in range(nb):
            x_r, x_i = carry[2 * b], carry[2 * b + 1]
            row = pl.ds(b * nk + k, 1)
            xr_ref[row, :] = x_r
            xi_ref[row, :] = x_i
            out.append(a_r * x_r - a_i * x_i + sr_ref[row, :])
            out.append(a_r * x_i + a_i * x_r + si_ref[row, :])
        return tuple(out)

    zero = jnp.zeros((1, xr_ref.shape[1]), F32)
    lax.fori_loop(0, nk, step, (zero,) * (2 * nb), unroll=4)
    y = (_dot(m_ref[0], u) + _dot(vr_ref[0], xr_ref[...].T.astype(BF16))
         + _dot(vi_ref[0], xi_ref[...].T.astype(BF16)))
    y = y + d_ref[0] * u.astype(F32)
    y = jax.nn.gelu(y).astype(BF16)
    for h in range(2):
        o_ref[:, h * gc:(h + 1) * gc, :] = y[h * n * gc:(h + 1) * n * gc].reshape(n, gc, r)


def _s5_core(u_t, ops, nb, nk):
    m, wr, wi, vr, vi, ar, ai, d_col = ops
    npair, w2, _ = m.shape
    p2 = wr.shape[1]
    n, _, r = u_t.shape
    gw = w2 // n
    g3 = lambda g: (g, 0, 0)
    data = pl.BlockSpec((n, gw, r), lambda g: (0, g, 0))
    return pl.pallas_call(
        functools.partial(_s5_core_kernel, nb=nb, nk=nk),
        out_shape=jax.ShapeDtypeStruct(u_t.shape, BF16),
        grid=(npair,),
        in_specs=[data, pl.BlockSpec((1, w2, w2), g3),
                  pl.BlockSpec((1, p2, w2), g3), pl.BlockSpec((1, p2, w2), g3),
                  pl.BlockSpec((1, w2, p2), g3), pl.BlockSpec((1, w2, p2), g3),
                  pl.BlockSpec((1, 1, p2), g3), pl.BlockSpec((1, 1, p2), g3),
                  pl.BlockSpec((1, w2, 1), g3)],
        out_specs=data,
        scratch_shapes=[pltpu.VMEM((r, p2), F32)] * 4,
        compiler_params=_cparams("parallel"),
        name="s5_core",
    )(u_t, m, wr, wi, vr, vi, ar, ai, d_col)


def _attn_kernel(q_ref, k_ref, v_ref, lq1_ref, lk1_ref, lq2_ref, lk2_ref, g_ref, o_ref,
                 m_ref, l_ref, acc_ref, *, bq, lambda_init):
    qi = pl.program_id(2)
    q = q_ref[0]
    lane = lax.broadcasted_iota(jnp.int32, (1, 2 * DA_HEAD_DIM), 1)
    zero = jnp.zeros_like(q)
    qs = jnp.concatenate([jnp.where(lane < DA_HEAD_DIM, q, zero),
                          jnp.where(lane >= DA_HEAD_DIM, q, zero)], axis=0)
    m_ref[...] = jnp.full(m_ref.shape, -jnp.inf, F32)
    l_ref[...] = jnp.zeros(l_ref.shape, F32)
    acc_ref[...] = jnp.zeros(acc_ref.shape, F32)
    reps = bq // LANES

    def keys(j):
        return pl.ds(pl.multiple_of(j * bq, bq), bq)

    def scores(j):
        return _dot_nt(qs, k_ref[0, keys(j), :])

    def absorb(s, j, mask):
        if mask is not None:
            s = jnp.where(mask, s, -jnp.inf)
        m_old = m_ref[...]
        m_new = jnp.maximum(m_old, jnp.max(s, axis=-1, keepdims=True))
        alpha = jnp.exp2(m_old - m_new)
        p = jnp.exp2(s - jnp.concatenate([m_new] * reps, axis=1))
        l_ref[...] = alpha * l_ref[...] + jnp.sum(p, axis=-1, keepdims=True)
        acc_ref[...] = alpha * acc_ref[...] + _dot(p.astype(BF16), v_ref[0, keys(j), :])
        m_ref[...] = m_new

    def visible_block(j, carry):
        absorb(scores(j), j, None)
        return carry

    lax.fori_loop(0, qi, visible_block, 0)
    r_chunk = (lax.broadcasted_iota(jnp.int32, (2 * bq, bq), 0) % bq) // CHUNK
    c_chunk = lax.broadcasted_iota(jnp.int32, (2 * bq, bq), 1) // CHUNK
    absorb(scores(qi), qi, c_chunk <= r_chunk)

    lam = (jnp.exp(jnp.sum(lq1_ref[...] * lk1_ref[...], axis=-1, keepdims=True))
           - jnp.exp(jnp.sum(lq2_ref[...] * lk2_ref[...], axis=-1, keepdims=True)) + lambda_init)
    o_all = acc_ref[...] / l_ref[...]
    o = o_all[:bq] - lam * o_all[bq:]
    o_ref[0] = (_rms(o, g_ref[...]) * (1.0 - lambda_init)).astype(BF16)


def _diff_attn(q3, k3, v3, lq1, lk1, lq2, lk2, g_subln, lambda_init, bq):
    nb, seq, _ = q3.shape
    hw = 2 * DA_HEAD_DIM
    lam_spec = pl.BlockSpec((1, DA_HEAD_DIM), lambda b, h, i: (0, 0))
    kv_spec = pl.BlockSpec((1, seq, hw), lambda b, h, i: (b, 0, h))
    return pl.pallas_call(
        functools.partial(_attn_kernel, bq=bq, lambda_init=lambda_init),
        out_shape=jax.ShapeDtypeStruct(q3.shape, BF16),
        grid=(nb, DA_HEADS, seq // bq),
        in_specs=[pl.BlockSpec((1, bq, hw), lambda b, h, i: (b, i, h)), kv_spec, kv_spec,
                  lam_spec, lam_spec, lam_spec, lam_spec,
                  pl.BlockSpec((1, hw), lambda b, h, i: (0, 0))],
        out_specs=pl.BlockSpec((1, bq, hw), lambda b, h, i: (b, i, h)),
        scratch_shapes=[pltpu.VMEM((2 * bq, hw), F32)] * 3,
        compiler_params=_cparams("parallel", "parallel", "parallel"),
        name="diff_attn",
    )(q3, k3, v3, lq1, lk1, lq2, lk2, g_subln)


def _post_kernel(ys_ref, oa_ref, gt_ref, x_ref, mod_ref, wba_ref, wout_ref, g_ref, wr_ref, br_ref,
                 h_ref, u2_ref, z_ref, zt_ref, cnt_ref, base_ref):
    d = x_ref.shape[1]

    @pl.when(pl.program_id(0) == 0)
    def _():
        base_ref[...] = jnp.zeros(base_ref.shape, F32)

    mod = mod_ref[0]
    y_att = _dot(oa_ref[...], wba_ref[...])
    gates = gt_ref[...].astype(F32)
    mix_in = gates[:, :d] * ys_ref[...].astype(F32) + gates[:, d:] * y_att
    mix = _dot(mix_in.astype(BF16), wout_ref[...])
    h = x_ref[...] + mod[2:3, :] * mix
    h_ref[...] = h
    u2 = _rms(h, g_ref[...]) * (1.0 + mod[4:5, :]) + mod[3:4, :]
    _to_slabs(u2_ref, u2)

    u2_hi = pltpu.bitcast(pltpu.bitcast(u2, jnp.uint32) & jnp.uint32(0xFFFF0000), F32)
    u2_lo = (u2 - u2_hi).astype(BF16)
    u2_hi = u2_hi.astype(BF16)
    logits = (_dot(u2_hi, wr_ref[0]) + _dot(u2_hi, wr_ref[1]) + _dot(u2_lo, wr_ref[0])) + br_ref[...]
    lane = lax.broadcasted_iota(jnp.int32, logits.shape, 1)
    neg = jnp.full_like(logits, -jnp.inf)
    big = jnp.full_like(lane, LANES)
    is_grp = (lane >= N_EXPERTS) & (lane < N_EXPERTS + N_GROUPS)
    glog = jnp.where(is_grp, logits, neg)
    gmax = jnp.max(glog, axis=-1, keepdims=True)
    gp = 1.0 / jnp.sum(jnp.exp(glog - gmax), axis=-1, keepdims=True)
    gi = jnp.min(jnp.where(glog == gmax, lane, big), axis=-1, keepdims=True) - N_EXPERTS
    in_grp = (lane < N_EXPERTS) & ((lane // EXP_PER_GROUP) == gi)
    elog = jnp.where(in_grp, logits, neg)
    v1 = jnp.max(elog, axis=-1, keepdims=True)
    i1 = jnp.min(jnp.where(elog == v1, lane, big), axis=-1, keepdims=True)
    elog2 = jnp.where(lane == i1, neg, elog)
    v2 = jnp.max(elog2, axis=-1, keepdims=True)
    i2 = jnp.min(jnp.where(elog2 == v2, lane, big), axis=-1, keepdims=True)
    e2 = jnp.exp(v2 - v1)
    w1 = gp / (1.0 + e2)
    w2 = gp * e2 / (1.0 + e2)
    tm = logits.shape[0]
    hot = jnp.where((lane == i1) | (lane == i2), 1.0, 0.0)
    earlier = (lax.broadcasted_iota(jnp.int32, (tm, tm), 1) < lax.broadcasted_iota(jnp.int32, (tm, tm), 0))
    before = _dot(jnp.where(earlier, 1.0, 0.0).astype(BF16), hot.astype(BF16)) + base_ref[...]
    rank1 = jnp.sum(jnp.where(lane == i1, before, 0.0), axis=-1, keepdims=True)
    rank2 = jnp.sum(jnp.where(lane == i2, before, 0.0), axis=-1, keepdims=True)
    base_ref[...] += jnp.sum(hot, axis=0, keepdims=True)
    cnt_ref[...] = base_ref[...]
    z = jnp.zeros_like(logits)
    for k, val in enumerate((i1.astype(F32), i2.astype(F32), rank1, rank2, w1, w2)):
        z = jnp.where(lane == k, val, z)
    z_ref[...] = z
    zt_ref[...] = z.T[:ROUTE_ROWS, :]


def _post_mix(ys, oa, gates, x2, mod3, wba, wout, g, wr, br, seq, tm):
    t, d = x2.shape
    tpb = seq // tm
    row = lambda i: (i, 0)
    full = lambda a: pl.BlockSpec(a.shape, lambda i: (0,) * a.ndim)
    return pl.pallas_call(
        _post_kernel,
        out_shape=[jax.ShapeDtypeStruct((t, d), F32), jax.ShapeDtypeStruct((t * SLAB, LANES), F32),
                   jax.ShapeDtypeStruct((t, LANES), F32), jax.ShapeDtypeStruct((ROUTE_ROWS, t), F32),
                   jax.ShapeDtypeStruct((1, LANES), F32)],
        grid=(t // tm,),
        in_specs=[pl.BlockSpec((tm, d), row), pl.BlockSpec((tm, oa.shape[1]), row),
                  pl.BlockSpec((tm, 2 * d), row), pl.BlockSpec((tm, d), row),
                  pl.BlockSpec((1, 6, d), lambda i: (i // tpb, 0, 0)),
                  full(wba), full(wout), full(g), full(wr), full(br)],
        out_specs=[pl.BlockSpec((tm, d), row), pl.BlockSpec((tm * SLAB, LANES), row),
                   pl.BlockSpec((tm, LANES), row),
                   pl.BlockSpec((ROUTE_ROWS, tm), lambda i: (0, i)), pl.BlockSpec((1, LANES), lambda i: (0, 0))],
        scratch_shapes=[pltpu.VMEM((1, LANES), F32)],
        compiler_params=_cparams("arbitrary"),
        name="post_mix",
    )(ys, oa, gates, x2, mod3, wba, wout, g, wr, br)


SLAB = 8


def _to_slabs(ref, val):
    n = val.shape[0]
    for j in range(SLAB):
        ref[pl.ds(j, n, stride=SLAB), :] = val[:, j * LANES:(j + 1) * LANES]


def _from_slabs(ref, n):
    return jnp.concatenate([ref[pl.ds(j, n, stride=SLAB), :] for j in range(SLAB)], axis=1)


def _slab(flat_ref, r):
    return flat_ref.at[pl.ds(pl.multiple_of(r * SLAB, SLAB), SLAB), :]


def _dispatch_kernel(pad_lo_ref, pad_hi_ref, n_used_ref, pos_ref, u_ref, xs_ref, zero_ref, sem, pad_sem):
    tm = u_ref.shape[0] // SLAB
    tile = zero_ref.shape[0]
    n_tiles = xs_ref.shape[0] // tile

    def send(r, carry):
        for k in range(2):
            pltpu.make_async_copy(_slab(u_ref, r), xs_ref.at[pos_ref[k * tm + r]], sem).start(priority=k)
        return carry

    lax.fori_loop(0, tm, send, 0, unroll=8)

    @pl.when(pl.program_id(0) == pl.num_programs(0) - 1)
    def _():
        zero_ref[...] = jnp.zeros(zero_ref.shape, zero_ref.dtype)

        def pad_copies(e, wait):
            lo = pad_lo_ref[e]
            n = pad_hi_ref[e] - lo
            for b in range(tile.bit_length() - 1):
                size = 1 << b

                @pl.when((n >> b) & 1 == 1)
                def _():
                    cp = pltpu.make_async_copy(zero_ref.at[pl.ds(0, size)],
                                               xs_ref.at[pl.ds(lo + (n & (size - 1)), size)], pad_sem)
                    cp.wait() if wait else cp.start()

        def tile_copy(i):
            return pltpu.make_async_copy(zero_ref, xs_ref.at[pl.ds(i * tile, tile)], pad_sem)

        for wait in (False, True):
            @pl.loop(0, N_EXPERTS)
            def _(e):
                pad_copies(e, wait)

            @pl.loop(n_used_ref[0], n_tiles)
            def _(j):
                tile_copy(j).wait() if wait else tile_copy(j).start()

    for _ in range(2 * tm // tile):
        pltpu.make_async_copy(zero_ref, xs_ref.at[pl.ds(0, tile)], sem).wait()


def _dispatch(pad_lo, pad_hi, n_used, pos, u2_flat, n_rows, tm, tile):
    t = u2_flat.shape[0] // SLAB
    return pl.pallas_call(
        _dispatch_kernel,
        out_shape=jax.ShapeDtypeStruct((n_rows, SLAB, LANES), F32),
        grid_spec=pltpu.PrefetchScalarGridSpec(
            num_scalar_prefetch=3,
            grid=(t // tm,),
            in_specs=[pl.BlockSpec((2 * tm,), lambda i, *_: (i,), memory_space=pltpu.SMEM),
                      pl.BlockSpec((tm * SLAB, LANES), lambda i, *_: (i, 0))],
            out_specs=pl.BlockSpec(memory_space=pl.ANY),
            scratch_shapes=[pltpu.VMEM((tile, SLAB, LANES), F32), pltpu.SemaphoreType.DMA,
                            pltpu.SemaphoreType.DMA]),
        compiler_params=_cparams("arbitrary"),
        name="moe_dispatch",
    )(pad_lo, pad_hi, n_used, pos, u2_flat)


def _expert_kernel(tile_e_ref, n_used_ref, x_ref, wg_ref, wu_ref, wd_ref, y_ref):
    used = pl.program_id(0) < n_used_ref[0]

    @pl.when(used)
    def _():
        x = _from_slabs(x_ref, x_ref.shape[0] // SLAB).astype(BF16)
        gate = _dot(x, wg_ref[0].astype(BF16))
        hdn = gate * jax.nn.sigmoid(gate) * _dot(x, wu_ref[0].astype(BF16))
        _to_slabs(y_ref, _dot(hdn.astype(BF16), wd_ref[0].astype(BF16)))

    @pl.when(jnp.logical_not(used))
    def _():
        y_ref[...] = jnp.zeros(y_ref.shape, y_ref.dtype)


def _experts(tile_e, n_used, xs_flat, wg, wu, wd, tm):
    _, d, de = wg.shape
    by_tile = lambda i, te, nu: (te[i], 0, 0)
    return pl.pallas_call(
        _expert_kernel,
        out_shape=jax.ShapeDtypeStruct(xs_flat.shape, F32),
        grid_spec=pltpu.PrefetchScalarGridSpec(
            num_scalar_prefetch=2,
            grid=(xs_flat.shape[0] // (tm * SLAB),),
            in_specs=[pl.BlockSpec((tm * SLAB, LANES), lambda i, te, nu: (jnp.minimum(i, nu[0] - 1), 0)),
                      pl.BlockSpec((1, d, de), by_tile), pl.BlockSpec((1, d, de), by_tile),
                      pl.BlockSpec((1, de, d), by_tile)],
            out_specs=pl.BlockSpec((tm * SLAB, LANES), lambda i, te, nu: (i, 0))),
        compiler_params=_cparams("arbitrary"),
        name="moe_experts",
    )(tile_e, n_used, xs_flat, wg, wu, wd)


def _combine_kernel(pos_ref, pos_next_ref, ys_ref, z_ref, h_ref, mod_ref, g_ref, o_ref, buf_ref, sems,
                    *, final_norm):
    tm = h_ref.shape[0]
    i = pl.program_id(0)

    def fetch(p_ref, slot):
        def one(r, carry):
            for k in range(2):
                pltpu.make_async_copy(ys_ref.at[p_ref[k * tm + r]], _slab(buf_ref.at[slot, k], r),
                                      sems.at[slot]).start(priority=k)
            return carry

        lax.fori_loop(0, tm, one, 0, unroll=8)

    @pl.when(i == 0)
    def _():
        fetch(pos_ref, 0)

    @pl.when(i + 1 < pl.num_programs(0))
    def _():
        fetch(pos_next_ref, (i + 1) % 2)

    slot = i % 2
    for k in range(2):
        pltpu.make_async_copy(buf_ref.at[slot, k], buf_ref.at[slot, k], sems.at[slot]).wait()
    z = z_ref[...]
    ffn = z[:, 4:5] * _from_slabs(buf_ref.at[slot, 0], tm) + z[:, 5:6] * _from_slabs(buf_ref.at[slot, 1], tm)
    h = h_ref[...] + mod_ref[0][5:6, :] * ffn
    o_ref[...] = _rms(h, g_ref[...]) if final_norm else h


def _combine(pos, ys, z, h1, mod3, g_final, seq, tm, final_norm):
    t, d = h1.shape
    tpb = seq // tm
    row = lambda i: (i, 0)
    return pl.pallas_call(
        functools.partial(_combine_kernel, final_norm=final_norm),
        out_shape=jax.ShapeDtypeStruct((t, d), F32),
        grid=(t // tm,),
        in_specs=[pl.BlockSpec((2 * tm,), lambda i: (i,), memory_space=pltpu.SMEM),
                  pl.BlockSpec((2 * tm,), lambda i: (jnp.minimum(i + 1, t // tm - 1),), memory_space=pltpu.SMEM),
                  pl.BlockSpec(memory_space=pl.ANY),
                  pl.BlockSpec((tm, LANES), row), pl.BlockSpec((tm, d), row),
                  pl.BlockSpec((1, 6, d), lambda i: (i // tpb, 0, 0)),
                  pl.BlockSpec((1, d), lambda i: (0, 0))],
        out_specs=pl.BlockSpec((tm, d), row),
        scratch_shapes=[pltpu.VMEM((2, 2, tm * SLAB, LANES), F32), pltpu.SemaphoreType.DMA((2,))],
        compiler_params=_cparams("arbitrary"),
        name="moe_combine",
    )(pos, pos, ys, z, h1, mod3, g_final)


def _route_tables(cnt, zt, n_tiles, tm_tok, tile):
    i32 = jnp.int32
    counts = cnt[0, :N_EXPERTS].astype(i32)
    padded = (counts + tile - 1) // tile * tile
    ends = jnp.cumsum(padded)
    starts = ends - padded
    n_used = (ends[-1] // tile).reshape(1)
    tile_row = jnp.minimum(jnp.arange(n_tiles, dtype=i32), n_used - 1) * tile
    tile_e = jnp.minimum(jnp.sum(tile_row[:, None] >= ends[None, :], axis=1), N_EXPERTS - 1).astype(i32)
    pos1 = jnp.take(starts, zt[0].astype(i32)) + zt[2].astype(i32)
    pos2 = jnp.take(starts, zt[1].astype(i32)) + zt[3].astype(i32)
    pos = jnp.stack([pos1.reshape(-1, tm_tok), pos2.reshape(-1, tm_tok)], axis=1).reshape(-1)
    return starts + counts, ends, pos, tile_e, n_used.astype(i32)


def kernel(x, c, w_ada, b_ada, g_norm_mix, w_in, b_in, s5_a_re, s5_a_im, s5_b_re, s5_b_im, s5_c_re, s5_c_im, s5_d, s5_log_dt, w_glu, b_glu, lambda_q1, lambda_k1, lambda_q2, lambda_k2, g_subln, w_br_ssm, w_br_attn, w_out, g_norm_ffn, w_router_grp, b_router_grp, w_router_exp, b_router_exp, w_exp_gate, w_exp_up, w_exp_down, g_final):
    nb, seq, d = x.shape
    depth = w_ada.shape[0]
    s5w = s5_d.shape[1]
    daw = w_br_attn.shape[1]
    ng = s5w // S5_GROUP
    nk = seq // S5_STEP
    assert seq % S5_TILE == 0 and ng % 2 == 0
    widths = (daw, daw, daw, 2 * d)
    tm = 512
    tm_moe = 1024
    bq = 512
    row = lambda a: a.reshape(1, -1)
    col = lambda a: a.reshape(-1, 1)

    h = x.reshape(nb * seq, d)
    for l in range(depth):
        lambda_init = 0.8 - 0.6 * math.exp(-0.3 * l)
        mod3 = _ada_mod(c, w_ada[l], row(b_ada[l])).reshape(nb, 6, d)

        w_in_b = w_in[l].astype(BF16)
        q, k, v, gates = _in_proj(h, mod3, row(g_norm_mix[l]), w_in_b[:, s5w:], row(b_in[l][s5w:]),
                                  seq, widths, tm)

        u_t = _s5_in(h, mod3, row(g_norm_mix[l]), w_in_b[:, :s5w].T, col(b_in[l][:s5w]), seq, S5_TILE)
        ops = _s5_prep(s5_a_re[l], s5_a_im[l], s5_log_dt[l], s5_b_re[l], s5_b_im[l], s5_c_re[l], s5_c_im[l],
                       s5_d[l])
        y_t = _s5_core(u_t, ops, nb, nk)
        ys = _s5_out(y_t, w_glu[l].T.astype(BF16), col(b_glu[l]), w_br_ssm[l].astype(BF16), S5_TILE)

        as3 = lambda a: a.reshape(nb, seq, daw)
        oa = _diff_attn(as3(q), as3(k), as3(v), row(lambda_q1[l]), row(lambda_k1[l]), row(lambda_q2[l]),
                        row(lambda_k2[l]), row(g_subln[l]), lambda_init, bq).reshape(nb * seq, daw)

        w_router = jnp.zeros((d, LANES), F32)
        w_router = w_router.at[:, :N_EXPERTS].set(w_router_exp[l])
        w_router = w_router.at[:, N_EXPERTS:N_EXPERTS + N_GROUPS].set(w_router_grp[l])
        b_router = jnp.zeros((1, LANES), F32)
        b_router = b_router.at[0, :N_EXPERTS].set(b_router_exp[l])
        b_router = b_router.at[0, N_EXPERTS:N_EXPERTS + N_GROUPS].set(b_router_grp[l])
        wr_hi = lax.bitcast_convert_type(
            lax.bitcast_convert_type(w_router, jnp.uint32) & jnp.uint32(0xFFFF0000), F32)
        wr_split = jnp.stack([wr_hi.astype(BF16), (w_router - wr_hi).astype(BF16)])
        h1, u2, z, zt, cnt = _post_mix(ys, oa, gates, h, mod3, w_br_attn[l].astype(BF16),
                                       w_out[l].astype(BF16), row(g_norm_ffn[l]), wr_split, b_router, seq, tm)

        n_tiles = (2 * nb * seq) // EXPERT_TILE + N_EXPERTS
        pad_lo, pad_hi, pos, tile_e, n_used = _route_tables(cnt, zt, n_tiles, tm_moe, EXPERT_TILE)
        n_rows = n_tiles * EXPERT_TILE
        xs = _dispatch(pad_lo, pad_hi, n_used, pos, u2, n_rows, tm_moe, EXPERT_TILE)
        ys_e = _experts(tile_e, n_used, xs.reshape(n_rows * SLAB, LANES), w_exp_gate[l], w_exp_up[l],
                        w_exp_down[l], EXPERT_TILE)
        h = _combine(pos, ys_e.reshape(n_rows, SLAB, LANES), z, h1, mod3, row(g_final), seq, tm_moe,
                     final_norm=(l == depth - 1))
    return h.reshape(nb, seq, d)
```

```python
import functools
import math

import jax
import jax.numpy as jnp
from jax import lax
from jax.experimental import pallas as pl
from jax.experimental.pallas import tpu as pltpu

EPS = 1e-6
CHUNK = 64
S5_GROUP = 16
S5_STATE = 64
S5_STEP = 16
S5_TILE = 2048
DA_HEADS = 8
DA_HEAD_DIM = 64
N_GROUPS = 4
EXP_PER_GROUP = 8
N_EXPERTS = N_GROUPS * EXP_PER_GROUP
D_EXPERT = 256
EXPERT_TILE = 512
ROUTE_ROWS = 8
LANES = 128
Q_SCALE = DA_HEAD_DIM ** -0.5 * math.log2(math.e)
VMEM_LIMIT = 56 * 1024 * 1024

F32 = jnp.float32
BF16 = jnp.bfloat16
HIGHEST = lax.Precision.HIGHEST


def _cparams(*sem):
    return pltpu.CompilerParams(dimension_semantics=sem, vmem_limit_bytes=VMEM_LIMIT)


def _dot(a, b):
    return jnp.dot(a, b, preferred_element_type=F32)


def _dot_nt(a, b):
    return lax.dot_general(a, b, (((1,), (1,)), ((), ())), preferred_element_type=F32)


def _rms(x, g):
    return x * lax.rsqrt(jnp.mean(x * x, axis=-1, keepdims=True) + EPS) * g


def _ada_kernel(c_ref, w_ref, b_ref, o_ref):
    c = c_ref[...]
    cs = c * jax.nn.sigmoid(c)
    o_ref[...] = jnp.dot(cs, w_ref[...], preferred_element_type=F32, precision=HIGHEST) + b_ref[...]


def _ada_mod(c, w, b):
    nb, d = c.shape
    n = w.shape[1]
    tn = n // 4
    return pl.pallas_call(
        _ada_kernel,
        out_shape=jax.ShapeDtypeStruct((nb, n), F32),
        grid=(n // tn,),
        in_specs=[pl.BlockSpec((nb, d), lambda j: (0, 0)),
                  pl.BlockSpec((d, tn), lambda j: (0, j)),
                  pl.BlockSpec((1, tn), lambda j: (0, j))],
        out_specs=pl.BlockSpec((nb, tn), lambda j: (0, j)),
        compiler_params=_cparams("parallel"),
        name="ada_mod",
    )(c, w, b)


def _modulated_norm(x_ref, mod_ref, g_ref):
    mod = mod_ref[0]
    return _rms(x_ref[...], g_ref[...]) * (1.0 + mod[1:2, :]) + mod[0:1, :]


def _inproj_kernel(x_ref, mod_ref, g_ref, w_ref, b_ref, q_ref, k_ref, v_ref, gt_ref, *, widths):
    ub = _modulated_norm(x_ref, mod_ref, g_ref).astype(BF16)
    qw, kw, vw, gw = widths
    o = 0
    q_ref[...] = ((_dot(ub, w_ref[:, o:o + qw]) + b_ref[:, o:o + qw]) * Q_SCALE).astype(BF16)
    o += qw
    k_ref[...] = (_dot(ub, w_ref[:, o:o + kw]) + b_ref[:, o:o + kw]).astype(BF16)
    o += kw
    v_ref[...] = (_dot(ub, w_ref[:, o:o + vw]) + b_ref[:, o:o + vw]).astype(BF16)
    o += vw
    gt_ref[...] = jax.nn.sigmoid(_dot(ub, w_ref[:, o:o + gw]) + b_ref[:, o:o + gw]).astype(BF16)


def _in_proj(x2, mod3, g, w, b, seq, widths, tm):
    t, d = x2.shape
    tpb = seq // tm
    row = lambda i: (i, 0)
    full = lambda a: pl.BlockSpec(a.shape, lambda i: (0, 0))
    return pl.pallas_call(
        functools.partial(_inproj_kernel, widths=widths),
        out_shape=[jax.ShapeDtypeStruct((t, wd), BF16) for wd in widths],
        grid=(t // tm,),
        in_specs=[pl.BlockSpec((tm, d), row), pl.BlockSpec((1, 6, d), lambda i: (i // tpb, 0, 0)),
                  full(g), full(w), full(b)],
        out_specs=[pl.BlockSpec((tm, wd), row) for wd in widths],
        compiler_params=_cparams("parallel"),
        name="in_proj",
    )(x2, mod3, g, w, b)


def _s5_in_kernel(x_ref, mod_ref, g_ref, wst_ref, bs_ref, ut_ref, slab_ref):
    u = _modulated_norm(x_ref, mod_ref, g_ref)
    nslab = slab_ref.shape[0]
    nk = x_ref.shape[0] // S5_STEP
    for j in range(nslab):
        slab_ref[j] = u[:, j * LANES:(j + 1) * LANES]
    for s in range(S5_STEP):
        us = jnp.concatenate([slab_ref[j, pl.ds(s, nk, stride=S5_STEP), :] for j in range(nslab)], axis=1)
        ut_ref[s] = (_dot_nt(wst_ref[...], us.astype(BF16)) + bs_ref[...]).astype(BF16)


def _s5_in(x2, mod3, g, wst, bs, seq, tm):
    t, d = x2.shape
    s5w = wst.shape[0]
    nkt = tm // S5_STEP
    tpb = seq // tm
    full = lambda a: pl.BlockSpec(a.shape, lambda i: (0, 0))
    return pl.pallas_call(
        _s5_in_kernel,
        out_shape=jax.ShapeDtypeStruct((S5_STEP, s5w, t // S5_STEP), BF16),
        grid=(t // tm,),
        in_specs=[pl.BlockSpec((tm, d), lambda i: (i, 0)), pl.BlockSpec((1, 6, d), lambda i: (i // tpb, 0, 0)),
                  full(g), full(wst), full(bs)],
        out_specs=pl.BlockSpec((S5_STEP, s5w, nkt), lambda i: (0, 0, i)),
        scratch_shapes=[pltpu.VMEM((d // LANES, tm, LANES), F32)],
        compiler_params=_cparams("parallel"),
        name="s5_in",
    )(x2, mod3, g, wst, bs)


def _s5_out_kernel(yt_ref, wglut_ref, bglu_ref, wbs_ref, o_ref, slab_ref):
    nslab = slab_ref.shape[0]
    nk = yt_ref.shape[2]
    for s in range(S5_STEP):
        yt = yt_ref[s]
        zt = _dot(wglut_ref[...], yt) + bglu_ref[...]
        yg = (yt.astype(F32) * jax.nn.sigmoid(zt)).T
        for j in range(nslab):
            slab_ref[j, pl.ds(s, nk, stride=S5_STEP), :] = yg[:, j * LANES:(j + 1) * LANES]
    yg = jnp.concatenate([slab_ref[j] for j in range(nslab)], axis=1)
    o_ref[...] = _dot(yg.astype(BF16), wbs_ref[...]).astype(BF16)


def _s5_out(y_t, wglut, bglu, wbs, tm):
    n, s5w, r = y_t.shape
    d = wbs.shape[1]
    nkt = tm // n
    full = lambda a: pl.BlockSpec(a.shape, lambda i: (0, 0))
    return pl.pallas_call(
        _s5_out_kernel,
        out_shape=jax.ShapeDtypeStruct((r * n, d), BF16),
        grid=(r // nkt,),
        in_specs=[pl.BlockSpec((n, s5w, nkt), lambda i: (0, 0, i)), full(wglut), full(bglu), full(wbs)],
        out_specs=pl.BlockSpec((tm, d), lambda i: (i, 0)),
        scratch_shapes=[pltpu.VMEM((s5w // LANES, tm, LANES), F32)],
        compiler_params=_cparams("parallel"),
        name="s5_out",
    )(y_t, wglut, bglu, wbs)


def _s5_prep_kernel(lrc_ref, lic_ref, dtc_ref, lrr_ref, lir_ref, dtr_ref, br_ref, bi_ref, cr_ref, ci_ref,
                    mt_ref, wtr_ref, wti_ref, vtr_ref, vti_ref, a16r_ref, a16i_ref):
    n, gc, ns = S5_STEP, S5_GROUP, S5_STATE
    w = n * gc
    iota = lambda shape, dim: lax.broadcasted_iota(jnp.int32, shape, dim)
    hdot = lambda a, b: jnp.dot(a, b, preferred_element_type=F32, precision=HIGHEST)

    def power(lr, li, dt, k):
        mag = jnp.exp(lr * dt * k)
        return mag * jnp.cos(li * dt * k), mag * jnp.sin(li * dt * k)

    lr, li, dt = lrc_ref[0], lic_ref[0], jnp.exp(dtc_ref[0])
    abr, abi = power(lr, li, dt, 1.0)
    den = lr * lr + li * li
    nr = abr - 1.0
    f_re = (nr * lr + abi * li) / den
    f_im = (abi * lr - nr * li) / den
    bb_re = f_re * br_ref[0] - f_im * bi_ref[0]
    bb_im = f_re * bi_ref[0] + f_im * br_ref[0]
    spread = jnp.where(iota((gc, 2 * w), 1) % gc == iota((gc, 2 * w), 0), 1.0, 0.0)
    bt_re, bt_im = hdot(bb_re, spread), hdot(bb_im, spread)
    k_in = (n - 1 - (iota((1, 2 * w), 1) % w) // gc).astype(F32)
    p_re, p_im = power(lr, li, dt, k_in)
    e_re = p_re * bt_re - p_im * bt_im
    e_im = p_re * bt_im + p_im * bt_re
    own = iota((2 * ns, 2 * w), 0) // ns == iota((2 * ns, 2 * w), 1) // w
    wtr_ref[0] = jnp.where(own, e_re, 0.0).astype(BF16)
    wti_ref[0] = jnp.where(own, e_im, 0.0).astype(BF16)

    lr_r, li_r, dt_r = lrr_ref[0], lir_ref[0], jnp.exp(dtr_ref[0])
    c_r, c_i = power(lr_r, li_r, dt_r, 1.0)
    for _ in range(n.bit_length() - 1):
        c_r, c_i = c_r * c_r - c_i * c_i, 2.0 * c_r * c_i
    a16r_ref[0], a16i_ref[0] = c_r, c_i
    k_out = ((iota((2 * w, 1), 0) % w) // gc + 1).astype(F32)
    q_re, q_im = power(lr_r, li_r, dt_r, k_out)
    c_re = jnp.concatenate([cr_ref[0]] * (2 * n), axis=0)
    c_im = jnp.concatenate([ci_ref[0]] * (2 * n), axis=0)
    own = iota((2 * w, 2 * ns), 0) // w == iota((2 * w, 2 * ns), 1) // ns
    vtr_ref[0] = jnp.where(own, c_re * q_re - c_im * q_im, 0.0).astype(BF16)
    vti_ref[0] = jnp.where(own, -(c_re * q_im + c_im * q_re), 0.0).astype(BF16)

    group_of_lane = iota((gc, 2 * ns), 1) // ns
    for h in range(2):
        mine = group_of_lane == h
        krev = (hdot(jnp.where(mine, cr_ref[0], 0.0), e_re[:, :w])
                - hdot(jnp.where(mine, ci_ref[0], 0.0), e_im[:, :w]))
        strip = jnp.concatenate([krev, jnp.zeros_like(krev)], axis=1)
        rows = []
        for t in range(n):
            shift = (n - 1 - t) * gc
            rows.append((pltpu.roll(strip, 2 * w - shift, 1) if shift else strip)[:, :w])
        mt_ref[0, h * w:(h + 1) * w, h * w:(h + 1) * w] = jnp.concatenate(rows, axis=0).astype(BF16)
        mt_ref[0, h * w:(h + 1) * w, (1 - h) * w:(2 - h) * w] = jnp.zeros((w, w), BF16)


def _s5_prep(a_re, a_im, log_dt, b_re, b_im, c_re, c_im, d_skip):
    ng, ns = a_re.shape
    gc = b_re.shape[-1]
    na = ng // 2
    w2 = 2 * S5_STEP * gc
    col = lambda a: a.reshape(na, 2 * ns, 1)
    row = lambda a: a.reshape(na, 1, 2 * ns)
    per_state = jnp.repeat(log_dt, ns)
    pair_c = lambda a: a.reshape(na, 2, gc, ns).transpose(0, 2, 1, 3).reshape(na, gc, 2 * ns)
    g3 = lambda g: (g, 0, 0)
    spec = lambda *shape: pl.BlockSpec((1,) + shape, g3)
    shp = lambda *shape, dt=BF16: jax.ShapeDtypeStruct((na,) + shape, dt)
    ops = pl.pallas_call(
        _s5_prep_kernel,
        out_shape=[shp(w2, w2), shp(2 * ns, w2), shp(2 * ns, w2), shp(w2, 2 * ns), shp(w2, 2 * ns),
                   shp(1, 2 * ns, dt=F32), shp(1, 2 * ns, dt=F32)],
        grid=(na,),
        in_specs=[spec(2 * ns, 1)] * 3 + [spec(1, 2 * ns)] * 3 + [spec(2 * ns, gc)] * 2 + [spec(gc, 2 * ns)] * 2,
        out_specs=[spec(w2, w2), spec(2 * ns, w2), spec(2 * ns, w2), spec(w2, 2 * ns), spec(w2, 2 * ns),
                   spec(1, 2 * ns), spec(1, 2 * ns)],
        compiler_params=_cparams("parallel"),
        name="s5_prep",
    )(col(a_re), col(a_im), col(per_state), row(a_re), row(a_im), row(per_state),
      b_re.reshape(na, 2 * ns, gc), b_im.reshape(na, 2 * ns, gc), pair_c(c_re), pair_c(c_im))
    d_col = jnp.broadcast_to(d_skip.reshape(na, 2, 1, gc), (na, 2, S5_STEP, gc)).reshape(na, w2, 1)
    return (*ops, d_col)


def _s5_core_kernel(u_ref, m_ref, wr_ref, wi_ref, vr_ref, vi_ref, ar_ref, ai_ref, d_ref, o_ref,
                    sr_ref, si_ref, xr_ref, xi_ref, *, nb, nk):
    n, gw, r = u_ref.shape
    gc = gw // 2
    u = jnp.concatenate([u_ref[:, h * gc:(h + 1) * gc, :].reshape(n * gc, r) for h in range(2)], axis=0)
    sr_ref[...] = _dot(wr_ref[0], u).T
    si_ref[...] = _dot(wi_ref[0], u).T
    a_r = ar_ref[0]
    a_i = ai_ref[0]

    def step(k, carry):
        out = []
        for b in range(nb):
            x_r, x_i = carry[2 * b], carry[2 * b + 1]
            row = pl.ds(b * nk + k, 1)
            xr_ref[row, :] = x_r
            xi_ref[row, :] = x_i
            out.append(a_r * x_r - a_i * x_i + sr_ref[row, :])
            out.append(a_r * x_i + a_i * x_r + si_ref[row, :])
        return tuple(out)

    zero = jnp.zeros((1, xr_ref.shape[1]), F32)
    lax.fori_loop(0, nk, step, (zero,) * (2 * nb), unroll=4)
    y = (_dot(m_ref[0], u) + _dot(vr_ref[0], xr_ref[...].T.astype(BF16))
         + _dot(vi_ref[0], xi_ref[...].T.astype(BF16)))
    y = y + d_ref[0] * u.astype(F32)
    y = jax.nn.gelu(y).astype(BF16)
    for h in range(2):
        o_ref[:, h * gc:(h + 1) * gc, :] = y[h * n * gc:(h + 1) * n * gc].reshape(n, gc, r)


def _s5_core(u_t, ops, nb, nk):
    m, wr, wi, vr, vi, ar, ai, d_col = ops
    npair, w2, _ = m.shape
    p2 = wr.shape[1]
    n, _, r = u_t.shape
    gw = w2 // n
    g3 = lambda g: (g, 0, 0)
    data = pl.BlockSpec((n, gw, r), lambda g: (0, g, 0))
    return pl.pallas_call(
        functools.partial(_s5_core_kernel, nb=nb, nk=nk),
        out_shape=jax.ShapeDtypeStruct(u_t.shape, BF16),
        grid=(npair,),
        in_specs=[data, pl.BlockSpec((1, w2, w2), g3),
                  pl.BlockSpec((1, p2, w2), g3), pl.BlockSpec((1, p2, w2), g3),
                  pl.BlockSpec((1, w2, p2), g3), pl.BlockSpec((1, w2, p2), g3),
                  pl.BlockSpec((1, 1, p2), g3), pl.BlockSpec((1, 1, p2), g3),
                  pl.BlockSpec((1, w2, 1), g3)],
        out_specs=data,
        scratch_shapes=[pltpu.VMEM((r, p2), F32)] * 4,
        compiler_params=_cparams("parallel"),
        name="s5_core",
    )(u_t, m, wr, wi, vr, vi, ar, ai, d_col)


def _attn_kernel(q_ref, k_ref, v_ref, lq1_ref, lk1_ref, lq2_ref, lk2_ref, g_ref, o_ref,
                 m_ref, acc_ref, *, bq, lambda_init):
    qi = pl.program_id(2)
    q = q_ref[0]
    lane = lax.broadcasted_iota(jnp.int32, (1, 2 * DA_HEAD_DIM), 1)
    zero = jnp.zeros_like(q)
    qs = jnp.concatenate([jnp.where(lane < DA_HEAD_DIM, q, zero),
                          jnp.where(lane >= DA_HEAD_DIM, q, zero)], axis=0)
    hw = 2 * DA_HEAD_DIM
    m_ref[...] = jnp.full(m_ref.shape, -jnp.inf, F32)
    acc_ref[...] = jnp.zeros(acc_ref.shape, F32)
    def absorb(start, size, masked):
        keys = pl.ds(pl.multiple_of(start, bq), size)
        s = _dot_nt(qs, k_ref[0, keys, :])
        if masked:
            r_chunk = (lax.broadcasted_iota(jnp.int32, s.shape, 0) % bq) // CHUNK
            c_chunk = lax.broadcasted_iota(jnp.int32, s.shape, 1) // CHUNK
            s = jnp.where(c_chunk <= r_chunk, s, -jnp.inf)
        m_old = m_ref[...]
        m_new = jnp.maximum(m_old, jnp.max(s, axis=-1, keepdims=True))
        alpha = jnp.exp2(m_old - m_new)
        p = jnp.exp2((s - jnp.concatenate([m_new] * (size // LANES), axis=1)).astype(BF16))
        v_ext = jnp.concatenate([v_ref[0, keys, :], jnp.ones((size, hw), BF16)], axis=1)
        pv = jnp.concatenate([_dot(p[:bq], v_ext), _dot(p[bq:], v_ext)], axis=0)
        acc_ref[...] = jnp.concatenate([alpha, alpha], axis=1) * acc_ref[...] + pv
        m_ref[...] = m_new

    def visible_pair(jj, carry):
        absorb(2 * jj * bq, 2 * bq, False)
        return carry

    lax.fori_loop(0, qi // 2, visible_pair, 0)

    @pl.when(qi % 2 == 1)
    def _():
        absorb((qi - 1) * bq, bq, False)

    absorb(qi * bq, bq, True)

    lam = (jnp.exp(jnp.sum(lq1_ref[...] * lk1_ref[...], axis=-1, keepdims=True))
           - jnp.exp(jnp.sum(lq2_ref[...] * lk2_ref[...], axis=-1, keepdims=True)) + lambda_init)
    acc = acc_ref[...]
    o_all = acc[:, :hw] / acc[:, hw:]
    o = o_all[:bq] - lam * o_all[bq:]
    o_ref[0] = (_rms(o, g_ref[...]) * (1.0 - lambda_init)).astype(BF16)


def _diff_attn(q3, k3, v3, lq1, lk1, lq2, lk2, g_subln, lambda_init, bq):
    nb, seq, _ = q3.shape
    hw = 2 * DA_HEAD_DIM
    lam_spec = pl.BlockSpec((1, DA_HEAD_DIM), lambda b, h, i: (0, 0))
    kv_spec = pl.BlockSpec((1, seq, hw), lambda b, h, i: (b, 0, h))
    return pl.pallas_call(
        functools.partial(_attn_kernel, bq=bq, lambda_init=lambda_init),
        out_shape=jax.ShapeDtypeStruct(q3.shape, BF16),
        grid=(nb, DA_HEADS, seq // bq),
        in_specs=[pl.BlockSpec((1, bq, hw), lambda b, h, i: (b, i, h)), kv_spec, kv_spec,
                  lam_spec, lam_spec, lam_spec, lam_spec,
                  pl.BlockSpec((1, hw), lambda b, h, i: (0, 0))],
        out_specs=pl.BlockSpec((1, bq, hw), lambda b, h, i: (b, i, h)),
        scratch_shapes=[pltpu.VMEM((2 * bq, hw), F32), pltpu.VMEM((2 * bq, 2 * hw), F32)],
        compiler_params=_cparams("parallel", "parallel", "parallel"),
        name="diff_attn",
    )(q3, k3, v3, lq1, lk1, lq2, lk2, g_subln)


def _post_kernel(ys_ref, oa_ref, gt_ref, x_ref, mod_ref, wba_ref, wout_ref, g_ref, wr_ref, br_ref,
                 h_ref, u2_ref, z_ref, zt_ref, cnt_ref, base_ref):
    d = x_ref.shape[1]

    @pl.when(pl.program_id(0) == 0)
    def _():
        base_ref[...] = jnp.zeros(base_ref.shape, F32)

    mod = mod_ref[0]
    y_att = _dot(oa_ref[...], wba_ref[...])
    gates = gt_ref[...].astype(F32)
    mix_in = gates[:, :d] * ys_ref[...].astype(F32) + gates[:, d:] * y_att
    mix = _dot(mix_in.astype(BF16), wout_ref[...])
    h = x_ref[...] + mod[2:3, :] * mix
    h_ref[...] = h
    u2 = _rms(h, g_ref[...]) * (1.0 + mod[4:5, :]) + mod[3:4, :]
    _to_slabs(u2_ref, u2)

    u2_hi = pltpu.bitcast(pltpu.bitcast(u2, jnp.uint32) & jnp.uint32(0xFFFF0000), F32)
    u2_lo = (u2 - u2_hi).astype(BF16)
    u2_hi = u2_hi.astype(BF16)
    logits = (_dot(u2_hi, wr_ref[0]) + _dot(u2_hi, wr_ref[1]) + _dot(u2_lo, wr_ref[0])) + br_ref[...]
    lane = lax.broadcasted_iota(jnp.int32, logits.shape, 1)
    neg = jnp.full_like(logits, -jnp.inf)
    big = jnp.full_like(lane, LANES)
    is_grp = (lane >= N_EXPERTS) & (lane < N_EXPERTS + N_GROUPS)
    glog = jnp.where(is_grp, logits, neg)
    gmax = jnp.max(glog, axis=-1, keepdims=True)
    gp = 1.0 / jnp.sum(jnp.exp(glog - gmax), axis=-1, keepdims=True)
    gi = jnp.min(jnp.where(glog == gmax, lane, big), axis=-1, keepdims=True) - N_EXPERTS
    in_grp = (lane < N_EXPERTS) & ((lane // EXP_PER_GROUP) == gi)
    elog = jnp.where(in_grp, logits, neg)
    v1 = jnp.max(elog, axis=-1, keepdims=True)
    i1 = jnp.min(jnp.where(elog == v1, lane, big), axis=-1, keepdims=True)
    elog2 = jnp.where(lane == i1, neg, elog)
    v2 = jnp.max(elog2, axis=-1, keepdims=True)
    i2 = jnp.min(jnp.where(elog2 == v2, lane, big), axis=-1, keepdims=True)
    e2 = jnp.exp(v2 - v1)
    w1 = gp / (1.0 + e2)
    w2 = gp * e2 / (1.0 + e2)
    tm = logits.shape[0]
    hot = jnp.where((lane == i1) | (lane == i2), 1.0, 0.0)
    earlier = (lax.broadcasted_iota(jnp.int32, (tm, tm), 1) < lax.broadcasted_iota(jnp.int32, (tm, tm), 0))
    before = _dot(jnp.where(earlier, 1.0, 0.0).astype(BF16), hot.astype(BF16)) + base_ref[...]
    rank1 = jnp.sum(jnp.where(lane == i1, before, 0.0), axis=-1, keepdims=True)
    rank2 = jnp.sum(jnp.where(lane == i2, before, 0.0), axis=-1, keepdims=True)
    base_ref[...] += jnp.sum(hot, axis=0, keepdims=True)
    cnt_ref[...] = base_ref[...]
    z = jnp.zeros_like(logits)
    for k, val in enumerate((i1.astype(F32), i2.astype(F32), rank1, rank2, w1, w2)):
        z = jnp.where(lane == k, val, z)
    z_ref[...] = z
    zt_ref[...] = z.T[:ROUTE_ROWS, :]


def _post_mix(ys, oa, gates, x2, mod3, wba, wout, g, wr, br, seq, tm):
    t, d = x2.shape
    tpb = seq // tm
    row = lambda i: (i, 0)
    full = lambda a: pl.BlockSpec(a.shape, lambda i: (0,) * a.ndim)
    return pl.pallas_call(
        _post_kernel,
        out_shape=[jax.ShapeDtypeStruct((t, d), F32), jax.ShapeDtypeStruct((t * SLAB, LANES), F32),
                   jax.ShapeDtypeStruct((t, LANES), F32), jax.ShapeDtypeStruct((ROUTE_ROWS, t), F32),
                   jax.ShapeDtypeStruct((1, LANES), F32)],
        grid=(t // tm,),
        in_specs=[pl.BlockSpec((tm, d), row), pl.BlockSpec((tm, oa.shape[1]), row),
                  pl.BlockSpec((tm, 2 * d), row), pl.BlockSpec((tm, d), row),
                  pl.BlockSpec((1, 6, d), lambda i: (i // tpb, 0, 0)),
                  full(wba), full(wout), full(g), full(wr), full(br)],
        out_specs=[pl.BlockSpec((tm, d), row), pl.BlockSpec((tm * SLAB, LANES), row),
                   pl.BlockSpec((tm, LANES), row),
                   pl.BlockSpec((ROUTE_ROWS, tm), lambda i: (0, i)), pl.BlockSpec((1, LANES), lambda i: (0, 0))],
        scratch_shapes=[pltpu.VMEM((1, LANES), F32)],
        compiler_params=_cparams("arbitrary"),
        name="post_mix",
    )(ys, oa, gates, x2, mod3, wba, wout, g, wr, br)


SLAB = 8


def _to_slabs(ref, val):
    n = val.shape[0]
    for j in range(SLAB):
        ref[pl.ds(j, n, stride=SLAB), :] = val[:, j * LANES:(j + 1) * LANES]


def _from_slabs(ref, n):
    return jnp.concatenate([ref[pl.ds(j, n, stride=SLAB), :] for j in range(SLAB)], axis=1)


def _slab(flat_ref, r):
    return flat_ref.at[pl.ds(pl.multiple_of(r * SLAB, SLAB), SLAB), :]


def _dispatch_kernel(pad_lo_ref, pad_hi_ref, n_used_ref, pos_ref, u_ref, xs_ref, zero_ref, sem, pad_sem):
    tm = u_ref.shape[0] // SLAB
    tile = zero_ref.shape[0]
    n_tiles = xs_ref.shape[0] // tile

    def send(r, carry):
        for k in range(2):
            pltpu.make_async_copy(_slab(u_ref, r), xs_ref.at[pos_ref[k * tm + r]], sem).start(priority=k)
        return carry

    lax.fori_loop(0, tm, send, 0, unroll=8)

    @pl.when(pl.program_id(0) == pl.num_programs(0) - 1)
    def _():
        zero_ref[...] = jnp.zeros(zero_ref.shape, zero_ref.dtype)

        def pad_copies(e, wait):
            lo = pad_lo_ref[e]
            n = pad_hi_ref[e] - lo
            for b in range(tile.bit_length() - 1):
                size = 1 << b

                @pl.when((n >> b) & 1 == 1)
                def _():
                    cp = pltpu.make_async_copy(zero_ref.at[pl.ds(0, size)],
                                               xs_ref.at[pl.ds(lo + (n & (size - 1)), size)], pad_sem)
                    cp.wait() if wait else cp.start()

        def tile_copy(i):
            return pltpu.make_async_copy(zero_ref, xs_ref.at[pl.ds(i * tile, tile)], pad_sem)

        for wait in (False, True):
            @pl.loop(0, N_EXPERTS)
            def _(e):
                pad_copies(e, wait)

            @pl.loop(n_used_ref[0], n_tiles)
            def _(j):
                tile_copy(j).wait() if wait else tile_copy(j).start()

    for _ in range(2 * tm // tile):
        pltpu.make_async_copy(zero_ref, xs_ref.at[pl.ds(0, tile)], sem).wait()


def _dispatch(pad_lo, pad_hi, n_used, pos, u2_flat, n_rows, tm, tile):
    t = u2_flat.shape[0] // SLAB
    return pl.pallas_call(
        _dispatch_kernel,
        out_shape=jax.ShapeDtypeStruct((n_rows, SLAB, LANES), F32),
        grid_spec=pltpu.PrefetchScalarGridSpec(
            num_scalar_prefetch=3,
            grid=(t // tm,),
            in_specs=[pl.BlockSpec((2 * tm,), lambda i, *_: (i,), memory_space=pltpu.SMEM),
                      pl.BlockSpec((tm * SLAB, LANES), lambda i, *_: (i, 0))],
            out_specs=pl.BlockSpec(memory_space=pl.ANY),
            scratch_shapes=[pltpu.VMEM((tile, SLAB, LANES), F32), pltpu.SemaphoreType.DMA,
                            pltpu.SemaphoreType.DMA]),
        compiler_params=_cparams("arbitrary"),
        name="moe_dispatch",
    )(pad_lo, pad_hi, n_used, pos, u2_flat)


def _expert_kernel(tile_e_ref, n_used_ref, x_ref, wg_ref, wu_ref, wd_ref, y_ref):
    used = pl.program_id(0) < n_used_ref[0]

    @pl.when(used)
    def _():
        x = _from_slabs(x_ref, x_ref.shape[0] // SLAB).astype(BF16)
        gate = _dot(x, wg_ref[0].astype(BF16))
        hdn = gate * jax.nn.sigmoid(gate) * _dot(x, wu_ref[0].astype(BF16))
        _to_slabs(y_ref, _dot(hdn.astype(BF16), wd_ref[0].astype(BF16)))

    @pl.when(jnp.logical_not(used))
    def _():
        y_ref[...] = jnp.zeros(y_ref.shape, y_ref.dtype)


def _experts(tile_e, n_used, xs_flat, wg, wu, wd, tm):
    _, d, de = wg.shape
    by_tile = lambda i, te, nu: (te[i], 0, 0)
    return pl.pallas_call(
        _expert_kernel,
        out_shape=jax.ShapeDtypeStruct(xs_flat.shape, F32),
        grid_spec=pltpu.PrefetchScalarGridSpec(
            num_scalar_prefetch=2,
            grid=(xs_flat.shape[0] // (tm * SLAB),),
            in_specs=[pl.BlockSpec((tm * SLAB, LANES), lambda i, te, nu: (jnp.minimum(i, nu[0] - 1), 0)),
                      pl.BlockSpec((1, d, de), by_tile), pl.BlockSpec((1, d, de), by_tile),
                      pl.BlockSpec((1, de, d), by_tile)],
            out_specs=pl.BlockSpec((tm * SLAB, LANES), lambda i, te, nu: (i, 0))),
        compiler_params=_cparams("arbitrary"),
        name="moe_experts",
    )(tile_e, n_used, xs_flat, wg, wu, wd)


def _combine_kernel(pos_ref, pos_next_ref, ys_ref, z_ref, h_ref, mod_ref, g_ref, o_ref, buf_ref, sems,
                    *, final_norm):
    tm = h_ref.shape[0]
    i = pl.program_id(0)

    def fetch(p_ref, slot):
        def one(r, carry):
            for k in range(2):
                pltpu.make_async_copy(ys_ref.at[p_ref[k * tm + r]], _slab(buf_ref.at[slot, k], r),
                                      sems.at[slot]).start(priority=k)
            return carry

        lax.fori_loop(0, tm, one, 0, unroll=8)

    @pl.when(i == 0)
    def _():
        fetch(pos_ref, 0)

    @pl.when(i + 1 < pl.num_programs(0))
    def _():
        fetch(pos_next_ref, (i + 1) % 2)

    slot = i % 2
    for k in range(2):
        pltpu.make_async_copy(buf_ref.at[slot, k], buf_ref.at[slot, k], sems.at[slot]).wait()
    z = z_ref[...]
    ffn = z[:, 4:5] * _from_slabs(buf_ref.at[slot, 0], tm) + z[:, 5:6] * _from_slabs(buf_ref.at[slot, 1], tm)
    h = h_ref[...] + mod_ref[0][5:6, :] * ffn
    o_ref[...] = _rms(h, g_ref[...]) if final_norm else h


def _combine(pos, ys, z, h1, mod3, g_final, seq, tm, final_norm):
    t, d = h1.shape
    tpb = seq // tm
    row = lambda i: (i, 0)
    return pl.pallas_call(
        functools.partial(_combine_kernel, final_norm=final_norm),
        out_shape=jax.ShapeDtypeStruct((t, d), F32),
        grid=(t // tm,),
        in_specs=[pl.BlockSpec((2 * tm,), lambda i: (i,), memory_space=pltpu.SMEM),
                  pl.BlockSpec((2 * tm,), lambda i: (jnp.minimum(i + 1, t // tm - 1),), memory_space=pltpu.SMEM),
                  pl.BlockSpec(memory_space=pl.ANY),
                  pl.BlockSpec((tm, LANES), row), pl.BlockSpec((tm, d), row),
                  pl.BlockSpec((1, 6, d), lambda i: (i // tpb, 0, 0)),
                  pl.BlockSpec((1, d), lambda i: (0, 0))],
        out_specs=pl.BlockSpec((tm, d), row),
        scratch_shapes=[pltpu.VMEM((2, 2, tm * SLAB, LANES), F32), pltpu.SemaphoreType.DMA((2,))],
        compiler_params=_cparams("arbitrary"),
        name="moe_combine",
    )(pos, pos, ys, z, h1, mod3, g_final)


def _route_tables(cnt, zt, n_tiles, tm_tok, tile):
    i32 = jnp.int32
    counts = cnt[0, :N_EXPERTS].astype(i32)
    padded = (counts + tile - 1) // tile * tile
    ends = jnp.cumsum(padded)
    starts = ends - padded
    n_used = (ends[-1] // tile).reshape(1)
    tile_row = jnp.minimum(jnp.arange(n_tiles, dtype=i32), n_used - 1) * tile
    tile_e = jnp.minimum(jnp.sum(tile_row[:, None] >= ends[None, :], axis=1), N_EXPERTS - 1).astype(i32)
    pos1 = jnp.take(starts, zt[0].astype(i32)) + zt[2].astype(i32)
    pos2 = jnp.take(starts, zt[1].astype(i32)) + zt[3].astype(i32)
    pos = jnp.stack([pos1.reshape(-1, tm_tok), pos2.reshape(-1, tm_tok)], axis=1).reshape(-1)
    return starts + counts, ends, pos, tile_e, n_used.astype(i32)


def kernel(x, c, w_ada, b_ada, g_norm_mix, w_in, b_in, s5_a_re, s5_a_im, s5_b_re, s5_b_im, s5_c_re, s5_c_im, s5_d, s5_log_dt, w_glu, b_glu, lambda_q1, lambda_k1, lambda_q2, lambda_k2, g_subln, w_br_ssm, w_br_attn, w_out, g_norm_ffn, w_router_grp, b_router_grp, w_router_exp, b_router_exp, w_exp_gate, w_exp_up, w_exp_down, g_final):
    nb, seq, d = x.shape
    depth = w_ada.shape[0]
    s5w = s5_d.shape[1]
    daw = w_br_attn.shape[1]
    ng = s5w // S5_GROUP
    nk = seq // S5_STEP
    assert seq % S5_TILE == 0 and ng % 2 == 0
    widths = (daw, daw, daw, 2 * d)
    tm = 512
    tm_moe = 1024
    bq = 512
    row = lambda a: a.reshape(1, -1)
    col = lambda a: a.reshape(-1, 1)

    h = x.reshape(nb * seq, d)
    for l in range(depth):
        lambda_init = 0.8 - 0.6 * math.exp(-0.3 * l)
        mod3 = _ada_mod(c, w_ada[l], row(b_ada[l])).reshape(nb, 6, d)

        w_in_b = w_in[l].astype(BF16)
        q, k, v, gates = _in_proj(h, mod3, row(g_norm_mix[l]), w_in_b[:, s5w:], row(b_in[l][s5w:]),
                                  seq, widths, tm)

        u_t = _s5_in(h, mod3, row(g_norm_mix[l]), w_in_b[:, :s5w].T, col(b_in[l][:s5w]), seq, S5_TILE)
        ops = _s5_prep(s5_a_re[l], s5_a_im[l], s5_log_dt[l], s5_b_re[l], s5_b_im[l], s5_c_re[l], s5_c_im[l],
                       s5_d[l])
        y_t = _s5_core(u_t, ops, nb, nk)
        ys = _s5_out(y_t, w_glu[l].T.astype(BF16), col(b_glu[l]), w_br_ssm[l].astype(BF16), S5_TILE)

        as3 = lambda a: a.reshape(nb, seq, daw)
        oa = _diff_attn(as3(q), as3(k), as3(v), row(lambda_q1[l]), row(lambda_k1[l]), row(lambda_q2[l]),
                        row(lambda_k2[l]), row(g_subln[l]), lambda_init, bq).reshape(nb * seq, daw)

        w_router = jnp.zeros((d, LANES), F32)
        w_router = w_router.at[:, :N_EXPERTS].set(w_router_exp[l])
        w_router = w_router.at[:, N_EXPERTS:N_EXPERTS + N_GROUPS].set(w_router_grp[l])
        b_router = jnp.zeros((1, LANES), F32)
        b_router = b_router.at[0, :N_EXPERTS].set(b_router_exp[l])
        b_router = b_router.at[0, N_EXPERTS:N_EXPERTS + N_GROUPS].set(b_router_grp[l])
        wr_hi = lax.bitcast_convert_type(
            lax.bitcast_convert_type(w_router, jnp.uint32) & jnp.uint32(0xFFFF0000), F32)
        wr_split = jnp.stack([wr_hi.astype(BF16), (w_router - wr_hi).astype(BF16)])
        h1, u2, z, zt, cnt = _post_mix(ys, oa, gates, h, mod3, w_br_attn[l].astype(BF16),
                                       w_out[l].astype(BF16), row(g_norm_ffn[l]), wr_split, b_router, seq, tm)

        n_tiles = (2 * nb * seq) // EXPERT_TILE + N_EXPERTS
        pad_lo, pad_hi, pos, tile_e, n_used = _route_tables(cnt, zt, n_tiles, tm_moe, EXPERT_TILE)
        n_rows = n_tiles * EXPERT_TILE
        xs = _dispatch(pad_lo, pad_hi, n_used, pos, u2, n_rows, tm_moe, EXPERT_TILE)
        ys_e = _experts(tile_e, n_used, xs.reshape(n_rows * SLAB, LANES), w_exp_gate[l], w_exp_up[l],
                        w_exp_down[l], EXPERT_TILE)
        h = _combine(pos, ys_e.reshape(n_rows, SLAB, LANES), z, h1, mod3, row(g_final), seq, tm_moe,
                     final_norm=(l == depth - 1))
    return h.reshape(nb, seq, d)
```

```python
import functools
import math

import jax
import jax.numpy as jnp
from jax import lax
from jax.experimental import pallas as pl
from jax.experimental.pallas import tpu as pltpu

EPS = 1e-6
CHUNK = 64
S5_GROUP = 16
S5_STATE = 64
S5_STEP = 16
S5_TILE = 2048
DA_HEADS = 8
DA_HEAD_DIM = 64
N_GROUPS = 4
EXP_PER_GROUP = 8
N_EXPERTS = N_GROUPS * EXP_PER_GROUP
D_EXPERT = 256
EXPERT_TILE = 512
ROUTE_ROWS = 8
LANES = 128
Q_SCALE = DA_HEAD_DIM ** -0.5 * math.log2(math.e)
VMEM_LIMIT = 56 * 1024 * 1024

F32 = jnp.float32
BF16 = jnp.bfloat16
HIGHEST = lax.Precision.HIGHEST


def _cparams(*sem):
    return pltpu.CompilerParams(dimension_semantics=sem, vmem_limit_bytes=VMEM_LIMIT)


def _dot(a, b):
    return jnp.dot(a, b, preferred_element_type=F32)


def _dot_nt(a, b):
    return lax.dot_general(a, b, (((1,), (1,)), ((), ())), preferred_element_type=F32)


def _rms(x, g):
    return x * lax.rsqrt(jnp.mean(x * x, axis=-1, keepdims=True) + EPS) * g


def _ada_kernel(c_ref, w_ref, b_ref, o_ref):
    c = c_ref[...]
    cs = c * jax.nn.sigmoid(c)
    o_ref[...] = jnp.dot(cs, w_ref[...], preferred_element_type=F32, precision=HIGHEST) + b_ref[...]


def _ada_mod(c, w, b):
    nb, d = c.shape
    n = w.shape[1]
    tn = n // 4
    return pl.pallas_call(
        _ada_kernel,
        out_shape=jax.ShapeDtypeStruct((nb, n), F32),
        grid=(n // tn,),
        in_specs=[pl.BlockSpec((nb, d), lambda j: (0, 0)),
                  pl.BlockSpec((d, tn), lambda j: (0, j)),
                  pl.BlockSpec((1, tn), lambda j: (0, j))],
        out_specs=pl.BlockSpec((nb, tn), lambda j: (0, j)),
        compiler_params=_cparams("parallel"),
        name="ada_mod",
    )(c, w, b)


def _modulated_norm(x_ref, mod_ref, g_ref):
    mod = mod_ref[0]
    return _rms(x_ref[...], g_ref[...]) * (1.0 + mod[1:2, :]) + mod[0:1, :]


def _inproj_kernel(x_ref, mod_ref, g_ref, w_ref, b_ref, q_ref, k_ref, v_ref, gt_ref, *, widths):
    ub = _modulated_norm(x_ref, mod_ref, g_ref).astype(BF16)
    qw, kw, vw, gw = widths
    o = 0
    q_ref[...] = ((_dot(ub, w_ref[:, o:o + qw]) + b_ref[:, o:o + qw]) * Q_SCALE).astype(BF16)
    o += qw
    k_ref[...] = (_dot(ub, w_ref[:, o:o + kw]) + b_ref[:, o:o + kw]).astype(BF16)
    o += kw
    v_ref[...] = (_dot(ub, w_ref[:, o:o + vw]) + b_ref[:, o:o + vw]).astype(BF16)
    o += vw
    gt_ref[...] = jax.nn.sigmoid(_dot(ub, w_ref[:, o:o + gw]) + b_ref[:, o:o + gw]).astype(BF16)


def _in_proj(x2, mod3, g, w, b, seq, widths, tm):
    t, d = x2.shape
    tpb = seq // tm
    row = lambda i: (i, 0)
    full = lambda a: pl.BlockSpec(a.shape, lambda i: (0, 0), pipeline_mode=pl.Buffered(1))
    return pl.pallas_call(
        functools.partial(_inproj_kernel, widths=widths),
        out_shape=[jax.ShapeDtypeStruct((t, wd), BF16) for wd in widths],
        grid=(t // tm,),
        in_specs=[pl.BlockSpec((tm, d), row), pl.BlockSpec((1, 6, d), lambda i: (i // tpb, 0, 0)),
                  full(g), full(w), full(b)],
        out_specs=[pl.BlockSpec((tm, wd), row) for wd in widths],
        compiler_params=_cparams("parallel"),
        name="in_proj",
    )(x2, mod3, g, w, b)


def _s5_in_kernel(x_ref, mod_ref, g_ref, wst_ref, bs_ref, ut_ref, slab_ref):
    u = _modulated_norm(x_ref, mod_ref, g_ref)
    nslab = slab_ref.shape[0]
    nk = x_ref.shape[0] // S5_STEP
    for j in range(nslab):
        slab_ref[j] = u[:, j * LANES:(j + 1) * LANES]
    by_step = jnp.concatenate(
        [jnp.concatenate([slab_ref[j, pl.ds(s, nk, stride=S5_STEP), :] for j in range(nslab)], axis=1).astype(BF16)
         for s in range(S5_STEP)], axis=0)
    ut = (_dot_nt(wst_ref[...], by_step) + bs_ref[...]).astype(BF16)
    for s in range(S5_STEP):
        ut_ref[s] = ut[:, s * nk:(s + 1) * nk]


def _s5_in(x2, mod3, g, wst, bs, seq, tm):
    t, d = x2.shape
    s5w = wst.shape[0]
    nkt = tm // S5_STEP
    tpb = seq // tm
    full = lambda a: pl.BlockSpec(a.shape, lambda i: (0, 0))
    return pl.pallas_call(
        _s5_in_kernel,
        out_shape=jax.ShapeDtypeStruct((S5_STEP, s5w, t // S5_STEP), BF16),
        grid=(t // tm,),
        in_specs=[pl.BlockSpec((tm, d), lambda i: (i, 0)), pl.BlockSpec((1, 6, d), lambda i: (i // tpb, 0, 0)),
                  full(g), full(wst), full(bs)],
        out_specs=pl.BlockSpec((S5_STEP, s5w, nkt), lambda i: (0, 0, i)),
        scratch_shapes=[pltpu.VMEM((d // LANES, tm, LANES), F32)],
        compiler_params=_cparams("parallel"),
        name="s5_in",
    )(x2, mod3, g, wst, bs)


def _s5_out_kernel(yt_ref, wglut_ref, bglu_ref, wbs_ref, o_ref, slab_ref):
    nslab = slab_ref.shape[0]
    nk = yt_ref.shape[2]
    for s in range(S5_STEP):
        yt = yt_ref[s]
        zt = _dot(wglut_ref[...], yt) + bglu_ref[...]
        yg = (yt.astype(F32) * jax.nn.sigmoid(zt)).T
        for j in range(nslab):
            slab_ref[j, pl.ds(s, nk, stride=S5_STEP), :] = yg[:, j * LANES:(j + 1) * LANES]
    yg = jnp.concatenate([slab_ref[j] for j in range(nslab)], axis=1)
    o_ref[...] = _dot(yg.astype(BF16), wbs_ref[...]).astype(BF16)


def _s5_out(y_t, wglut, bglu, wbs, tm):
    n, s5w, r = y_t.shape
    d = wbs.shape[1]
    nkt = tm // n
    full = lambda a: pl.BlockSpec(a.shape, lambda i: (0, 0))
    return pl.pallas_call(
        _s5_out_kernel,
        out_shape=jax.ShapeDtypeStruct((r * n, d), BF16),
        grid=(r // nkt,),
        in_specs=[pl.BlockSpec((n, s5w, nkt), lambda i: (0, 0, i)), full(wglut), full(bglu), full(wbs)],
        out_specs=pl.BlockSpec((tm, d), lambda i: (i, 0)),
        scratch_shapes=[pltpu.VMEM((s5w // LANES, tm, LANES), F32)],
        compiler_params=_cparams("parallel"),
        name="s5_out",
    )(y_t, wglut, bglu, wbs)


def _s5_prep_kernel(lrc_ref, lic_ref, dtc_ref, lrr_ref, lir_ref, dtr_ref, br_ref, bi_ref, cr_ref, ci_ref,
                    mt_ref, wtr_ref, wti_ref, vtr_ref, vti_ref, a16r_ref, a16i_ref):
    n, gc, ns = S5_STEP, S5_GROUP, S5_STATE
    w = n * gc
    iota = lambda shape, dim: lax.broadcasted_iota(jnp.int32, shape, dim)
    hdot = lambda a, b: jnp.dot(a, b, preferred_element_type=F32, precision=HIGHEST)

    def power(lr, li, dt, k):
        mag = jnp.exp(lr * dt * k)
        return mag * jnp.cos(li * dt * k), mag * jnp.sin(li * dt * k)

    lr, li, dt = lrc_ref[0], lic_ref[0], jnp.exp(dtc_ref[0])
    abr, abi = power(lr, li, dt, 1.0)
    den = lr * lr + li * li
    nr = abr - 1.0
    f_re = (nr * lr + abi * li) / den
    f_im = (abi * lr - nr * li) / den
    bb_re = f_re * br_ref[0] - f_im * bi_ref[0]
    bb_im = f_re * bi_ref[0] + f_im * br_ref[0]
    spread = jnp.where(iota((gc, 2 * w), 1) % gc == iota((gc, 2 * w), 0), 1.0, 0.0)
    bt_re, bt_im = hdot(bb_re, spread), hdot(bb_im, spread)
    k_in = (n - 1 - (iota((1, 2 * w), 1) % w) // gc).astype(F32)
    p_re, p_im = power(lr, li, dt, k_in)
    e_re = p_re * bt_re - p_im * bt_im
    e_im = p_re * bt_im + p_im * bt_re
    own = iota((2 * ns, 2 * w), 0) // ns == iota((2 * ns, 2 * w), 1) // w
    wtr_ref[0] = jnp.where(own, e_re, 0.0).astype(BF16)
    wti_ref[0] = jnp.where(own, e_im, 0.0).astype(BF16)

    lr_r, li_r, dt_r = lrr_ref[0], lir_ref[0], jnp.exp(dtr_ref[0])
    c_r, c_i = power(lr_r, li_r, dt_r, 1.0)
    for _ in range(n.bit_length() - 1):
        c_r, c_i = c_r * c_r - c_i * c_i, 2.0 * c_r * c_i
    a16r_ref[0], a16i_ref[0] = c_r, c_i
    k_out = ((iota((2 * w, 1), 0) % w) // gc + 1).astype(F32)
    q_re, q_im = power(lr_r, li_r, dt_r, k_out)
    c_re = jnp.concatenate([cr_ref[0]] * (2 * n), axis=0)
    c_im = jnp.concatenate([ci_ref[0]] * (2 * n), axis=0)
    own = iota((2 * w, 2 * ns), 0) // w == iota((2 * w, 2 * ns), 1) // ns
    vtr_ref[0] = jnp.where(own, c_re * q_re - c_im * q_im, 0.0).astype(BF16)
    vti_ref[0] = jnp.where(own, -(c_re * q_im + c_im * q_re), 0.0).astype(BF16)

    group_of_lane = iota((gc, 2 * ns), 1) // ns
    for h in range(2):
        mine = group_of_lane == h
        krev = (hdot(jnp.where(mine, cr_ref[0], 0.0), e_re[:, :w])
                - hdot(jnp.where(mine, ci_ref[0], 0.0), e_im[:, :w]))
        strip = jnp.concatenate([krev, jnp.zeros_like(krev)], axis=1)
        rows = []
        for t in range(n):
            shift = (n - 1 - t) * gc
            rows.append((pltpu.roll(strip, 2 * w - shift, 1) if shift else strip)[:, :w])
        mt_ref[0, h * w:(h + 1) * w, h * w:(h + 1) * w] = jnp.concatenate(rows, axis=0).astype(BF16)
        mt_ref[0, h * w:(h + 1) * w, (1 - h) * w:(2 - h) * w] = jnp.zeros((w, w), BF16)


def _s5_prep(a_re, a_im, log_dt, b_re, b_im, c_re, c_im, d_skip):
    ng, ns = a_re.shape
    gc = b_re.shape[-1]
    na = ng // 2
    w2 = 2 * S5_STEP * gc
    col = lambda a: a.reshape(na, 2 * ns, 1)
    row = lambda a: a.reshape(na, 1, 2 * ns)
    per_state = jnp.repeat(log_dt, ns)
    pair_c = lambda a: a.reshape(na, 2, gc, ns).transpose(0, 2, 1, 3).reshape(na, gc, 2 * ns)
    g3 = lambda g: (g, 0, 0)
    spec = lambda *shape: pl.BlockSpec((1,) + shape, g3)
    shp = lambda *shape, dt=BF16: jax.ShapeDtypeStruct((na,) + shape, dt)
    ops = pl.pallas_call(
        _s5_prep_kernel,
        out_shape=[shp(w2, w2), shp(2 * ns, w2), shp(2 * ns, w2), shp(w2, 2 * ns), shp(w2, 2 * ns),
                   shp(1, 2 * ns, dt=F32), shp(1, 2 * ns, dt=F32)],
        grid=(na,),
        in_specs=[spec(2 * ns, 1)] * 3 + [spec(1, 2 * ns)] * 3 + [spec(2 * ns, gc)] * 2 + [spec(gc, 2 * ns)] * 2,
        out_specs=[spec(w2, w2), spec(2 * ns, w2), spec(2 * ns, w2), spec(w2, 2 * ns), spec(w2, 2 * ns),
                   spec(1, 2 * ns), spec(1, 2 * ns)],
        compiler_params=_cparams("parallel"),
        name="s5_prep",
    )(col(a_re), col(a_im), col(per_state), row(a_re), row(a_im), row(per_state),
      b_re.reshape(na, 2 * ns, gc), b_im.reshape(na, 2 * ns, gc), pair_c(c_re), pair_c(c_im))
    d_col = jnp.broadcast_to(d_skip.reshape(na, 2, 1, gc), (na, 2, S5_STEP, gc)).reshape(na, w2, 1)
    return (*ops, d_col)


def _s5_core_kernel(u_ref, m_ref, wr_ref, wi_ref, vr_ref, vi_ref, ar_ref, ai_ref, d_ref, o_ref,
                    sr_ref, si_ref, xr_ref, xi_ref, *, nb, nk):
    n, gw, r = u_ref.shape
    gc = gw // 2
    u = jnp.concatenate([u_ref[:, h * gc:(h + 1) * gc, :].reshape(n * gc, r) for h in range(2)], axis=0)
    sr_ref[...] = _dot(wr_ref[0], u).T
    si_ref[...] = _dot(wi_ref[0], u).T
    a_r = ar_ref[0]
    a_i = ai_ref[0]

    def step(k, carry):
        out = []
        for b in range(nb):
            x_r, x_i = carry[2 * b], carry[2 * b + 1]
            row = pl.ds(b * nk + k, 1)
            xr_ref[row, :] = x_r
            xi_ref[row, :] = x_i
            out.append(a_r * x_r - a_i * x_i + sr_ref[row, :])
            out.append(a_r * x_i + a_i * x_r + si_ref[row, :])
        return tuple(out)

    zero = jnp.zeros((1, xr_ref.shape[1]), F32)
    lax.fori_loop(0, nk, step, (zero,) * (2 * nb), unroll=4)
    y = (_dot(m_ref[0], u) + _dot(vr_ref[0], xr_ref[...].T.astype(BF16))
         + _dot(vi_ref[0], xi_ref[...].T.astype(BF16)))
    y = y + d_ref[0] * u.astype(F32)
    y = jax.nn.gelu(y).astype(BF16)
    for h in range(2):
        o_ref[:, h * gc:(h + 1) * gc, :] = y[h * n * gc:(h + 1) * n * gc].reshape(n, gc, r)


def _s5_core(u_t, ops, nb, nk):
    m, wr, wi, vr, vi, ar, ai, d_col = ops
    npair, w2, _ = m.shape
    p2 = wr.shape[1]
    n, _, r = u_t.shape
    gw = w2 // n
    g3 = lambda g: (g, 0, 0)
    data = pl.BlockSpec((n, gw, r), lambda g: (0, g, 0))
    return pl.pallas_call(
        functools.partial(_s5_core_kernel, nb=nb, nk=nk),
        out_shape=jax.ShapeDtypeStruct(u_t.shape, BF16),
        grid=(npair,),
        in_specs=[data, pl.BlockSpec((1, w2, w2), g3),
                  pl.BlockSpec((1, p2, w2), g3), pl.BlockSpec((1, p2, w2), g3),
                  pl.BlockSpec((1, w2, p2), g3), pl.BlockSpec((1, w2, p2), g3),
                  pl.BlockSpec((1, 1, p2), g3), pl.BlockSpec((1, 1, p2), g3),
                  pl.BlockSpec((1, w2, 1), g3)],
        out_specs=data,
        scratch_shapes=[pltpu.VMEM((r, p2), F32)] * 4,
        compiler_params=_cparams("parallel"),
        name="s5_core",
    )(u_t, m, wr, wi, vr, vi, ar, ai, d_col)


def _attn_kernel(q_ref, k_ref, v_ref, lq1_ref, lk1_ref, lq2_ref, lk2_ref, g_ref, o_ref,
                 m_ref, acc_ref, *, bq, lambda_init):
    qi = pl.program_id(2)
    q = q_ref[0]
    lane = lax.broadcasted_iota(jnp.int32, (1, 2 * DA_HEAD_DIM), 1)
    zero = jnp.zeros_like(q)
    qs = jnp.concatenate([jnp.where(lane < DA_HEAD_DIM, q, zero),
                          jnp.where(lane >= DA_HEAD_DIM, q, zero)], axis=0)
    hw = 2 * DA_HEAD_DIM
    m_ref[...] = jnp.full(m_ref.shape, -jnp.inf, F32)
    acc_ref[...] = jnp.zeros(acc_ref.shape, F32)
    def absorb(start, size, masked):
        keys = pl.ds(pl.multiple_of(start, bq), size)
        s = _dot_nt(qs, k_ref[0, keys, :])
        if masked:
            r_chunk = (lax.broadcasted_iota(jnp.int32, s.shape, 0) % bq) // CHUNK
            c_chunk = lax.broadcasted_iota(jnp.int32, s.shape, 1) // CHUNK
            s = jnp.where(c_chunk <= r_chunk, s, -jnp.inf)
        m_old = m_ref[...]
        m_new = jnp.maximum(m_old, jnp.max(s, axis=-1, keepdims=True))
        alpha = jnp.exp2(m_old - m_new)
        p = jnp.exp2((s - jnp.concatenate([m_new] * (size // LANES), axis=1)).astype(BF16))
        v_ext = jnp.concatenate([v_ref[0, keys, :], jnp.ones((size, hw), BF16)], axis=1)
        pv = jnp.concatenate([_dot(p[:bq], v_ext), _dot(p[bq:], v_ext)], axis=0)
        acc_ref[...] = jnp.concatenate([alpha, alpha], axis=1) * acc_ref[...] + pv
        m_ref[...] = m_new

    def visible_pair(jj, carry):
        absorb(2 * jj * bq, 2 * bq, False)
        return carry

    lax.fori_loop(0, qi // 2, visible_pair, 0)

    @pl.when(qi % 2 == 1)
    def _():
        absorb((qi - 1) * bq, bq, False)

    absorb(qi * bq, bq, True)

    lam = (jnp.exp(jnp.sum(lq1_ref[...] * lk1_ref[...], axis=-1, keepdims=True))
           - jnp.exp(jnp.sum(lq2_ref[...] * lk2_ref[...], axis=-1, keepdims=True)) + lambda_init)
    acc = acc_ref[...]
    o_all = acc[:, :hw] / acc[:, hw:]
    o = o_all[:bq] - lam * o_all[bq:]
    o_ref[0] = (_rms(o, g_ref[...]) * (1.0 - lambda_init)).astype(BF16)


def _diff_attn(q3, k3, v3, lq1, lk1, lq2, lk2, g_subln, lambda_init, bq):
    nb, seq, _ = q3.shape
    hw = 2 * DA_HEAD_DIM
    lam_spec = pl.BlockSpec((1, DA_HEAD_DIM), lambda b, h, i: (0, 0))
    kv_spec = pl.BlockSpec((1, seq, hw), lambda b, h, i: (b, 0, h))
    return pl.pallas_call(
        functools.partial(_attn_kernel, bq=bq, lambda_init=lambda_init),
        out_shape=jax.ShapeDtypeStruct(q3.shape, BF16),
        grid=(nb, DA_HEADS, seq // bq),
        in_specs=[pl.BlockSpec((1, bq, hw), lambda b, h, i: (b, i, h)), kv_spec, kv_spec,
                  lam_spec, lam_spec, lam_spec, lam_spec,
                  pl.BlockSpec((1, hw), lambda b, h, i: (0, 0))],
        out_specs=pl.BlockSpec((1, bq, hw), lambda b, h, i: (b, i, h)),
        scratch_shapes=[pltpu.VMEM((2 * bq, hw), F32), pltpu.VMEM((2 * bq, 2 * hw), F32)],
        compiler_params=_cparams("parallel", "parallel", "parallel"),
        name="diff_attn",
    )(q3, k3, v3, lq1, lk1, lq2, lk2, g_subln)


def _post_kernel(ys_ref, oa_ref, gt_ref, x_ref, mod_ref, wba_ref, wout_ref, g_ref, wr_ref, br_ref,
                 h_ref, u2_ref, z_ref, zt_ref, cnt_ref, base_ref):
    d = x_ref.shape[1]

    @pl.when(pl.program_id(0) == 0)
    def _():
        base_ref[...] = jnp.zeros(base_ref.shape, F32)

    mod = mod_ref[0]
    y_att = _dot(oa_ref[...], wba_ref[...])
    gates = gt_ref[...].astype(F32)
    mix_in = gates[:, :d] * ys_ref[...].astype(F32) + gates[:, d:] * y_att
    mix = _dot(mix_in.astype(BF16), wout_ref[...])
    h = x_ref[...] + mod[2:3, :] * mix
    h_ref[...] = h
    u2 = _rms(h, g_ref[...]) * (1.0 + mod[4:5, :]) + mod[3:4, :]
    _to_slabs(u2_ref, u2)

    u2_hi = pltpu.bitcast(pltpu.bitcast(u2, jnp.uint32) & jnp.uint32(0xFFFF0000), F32)
    u2_lo = (u2 - u2_hi).astype(BF16)
    u2_hi = u2_hi.astype(BF16)
    logits = (_dot(u2_hi, wr_ref[0]) + _dot(u2_hi, wr_ref[1]) + _dot(u2_lo, wr_ref[0])) + br_ref[...]
    lane = lax.broadcasted_iota(jnp.int32, logits.shape, 1)
    neg = jnp.full_like(logits, -jnp.inf)
    big = jnp.full_like(lane, LANES)
    is_grp = (lane >= N_EXPERTS) & (lane < N_EXPERTS + N_GROUPS)
    glog = jnp.where(is_grp, logits, neg)
    gmax = jnp.max(glog, axis=-1, keepdims=True)
    gp = 1.0 / jnp.sum(jnp.exp(glog - gmax), axis=-1, keepdims=True)
    gi = jnp.min(jnp.where(glog == gmax, lane, big), axis=-1, keepdims=True) - N_EXPERTS
    in_grp = (lane < N_EXPERTS) & ((lane // EXP_PER_GROUP) == gi)
    elog = jnp.where(in_grp, logits, neg)
    v1 = jnp.max(elog, axis=-1, keepdims=True)
    i1 = jnp.min(jnp.where(elog == v1, lane, big), axis=-1, keepdims=True)
    elog2 = jnp.where(lane == i1, neg, elog)
    v2 = jnp.max(elog2, axis=-1, keepdims=True)
    i2 = jnp.min(jnp.where(elog2 == v2, lane, big), axis=-1, keepdims=True)
    e2 = jnp.exp(v2 - v1)
    w1 = gp / (1.0 + e2)
    w2 = gp * e2 / (1.0 + e2)
    tm = logits.shape[0]
    hot = jnp.where((lane == i1) | (lane == i2), 1.0, 0.0)
    earlier = (lax.broadcasted_iota(jnp.int32, (tm, tm), 1) < lax.broadcasted_iota(jnp.int32, (tm, tm), 0))
    before = _dot(jnp.where(earlier, 1.0, 0.0).astype(BF16), hot.astype(BF16)) + base_ref[...]
    rank1 = jnp.sum(jnp.where(lane == i1, before, 0.0), axis=-1, keepdims=True)
    rank2 = jnp.sum(jnp.where(lane == i2, before, 0.0), axis=-1, keepdims=True)
    base_ref[...] += jnp.sum(hot, axis=0, keepdims=True)
    cnt_ref[...] = base_ref[...]
    z = jnp.zeros_like(logits)
    for k, val in enumerate((i1.astype(F32), i2.astype(F32), rank1, rank2, w1, w2)):
        z = jnp.where(lane == k, val, z)
    z_ref[...] = z
    zt_ref[...] = z.T[:ROUTE_ROWS, :]


def _post_mix(ys, oa, gates, x2, mod3, wba, wout, g, wr, br, seq, tm):
    t, d = x2.shape
    tpb = seq // tm
    row = lambda i: (i, 0)
    full = lambda a: pl.BlockSpec(a.shape, lambda i: (0,) * a.ndim)
    return pl.pallas_call(
        _post_kernel,
        out_shape=[jax.ShapeDtypeStruct((t, d), F32), jax.ShapeDtypeStruct((t * SLAB, LANES), F32),
                   jax.ShapeDtypeStruct((t, LANES), F32), jax.ShapeDtypeStruct((ROUTE_ROWS, t), F32),
                   jax.ShapeDtypeStruct((1, LANES), F32)],
        grid=(t // tm,),
        in_specs=[pl.BlockSpec((tm, d), row), pl.BlockSpec((tm, oa.shape[1]), row),
                  pl.BlockSpec((tm, 2 * d), row), pl.BlockSpec((tm, d), row),
                  pl.BlockSpec((1, 6, d), lambda i: (i // tpb, 0, 0)),
                  full(wba), full(wout), full(g), full(wr), full(br)],
        out_specs=[pl.BlockSpec((tm, d), row), pl.BlockSpec((tm * SLAB, LANES), row),
                   pl.BlockSpec((tm, LANES), row),
                   pl.BlockSpec((ROUTE_ROWS, tm), lambda i: (0, i)), pl.BlockSpec((1, LANES), lambda i: (0, 0))],
        scratch_shapes=[pltpu.VMEM((1, LANES), F32)],
        compiler_params=_cparams("arbitrary"),
        name="post_mix",
    )(ys, oa, gates, x2, mod3, wba, wout, g, wr, br)


SLAB = 8


def _to_slabs(ref, val):
    n = val.shape[0]
    for j in range(SLAB):
        ref[pl.ds(j, n, stride=SLAB), :] = val[:, j * LANES:(j + 1) * LANES]


def _from_slabs(ref, n):
    return jnp.concatenate([ref[pl.ds(j, n, stride=SLAB), :] for j in range(SLAB)], axis=1)


def _slab(flat_ref, r):
    return flat_ref.at[pl.ds(pl.multiple_of(r * SLAB, SLAB), SLAB), :]


def _dispatch_kernel(pad_lo_ref, pad_hi_ref, n_used_ref, pos_ref, u_ref, xs_ref, zero_ref, sem, pad_sem):
    tm = u_ref.shape[0] // SLAB
    tile = zero_ref.shape[0]
    n_tiles = xs_ref.shape[0] // tile

    def send(r, carry):
        for k in range(2):
            pltpu.make_async_copy(_slab(u_ref, r), xs_ref.at[pos_ref[k * tm + r]], sem).start(priority=k)
        return carry

    lax.fori_loop(0, tm, send, 0, unroll=8)

    @pl.when(pl.program_id(0) == pl.num_programs(0) - 1)
    def _():
        zero_ref[...] = jnp.zeros(zero_ref.shape, zero_ref.dtype)

        def pad_copies(e, wait):
            lo = pad_lo_ref[e]
            n = pad_hi_ref[e] - lo
            for b in range(tile.bit_length() - 1):
                size = 1 << b

                @pl.when((n >> b) & 1 == 1)
                def _():
                    cp = pltpu.make_async_copy(zero_ref.at[pl.ds(0, size)],
                                               xs_ref.at[pl.ds(lo + (n & (size - 1)), size)], pad_sem)
                    cp.wait() if wait else cp.start()

        def tile_copy(i):
            return pltpu.make_async_copy(zero_ref, xs_ref.at[pl.ds(i * tile, tile)], pad_sem)

        for wait in (False, True):
            @pl.loop(0, N_EXPERTS)
            def _(e):
                pad_copies(e, wait)

            @pl.loop(n_used_ref[0], n_tiles)
            def _(j):
                tile_copy(j).wait() if wait else tile_copy(j).start()

    for _ in range(2 * tm // tile):
        pltpu.make_async_copy(zero_ref, xs_ref.at[pl.ds(0, tile)], sem).wait()


def _dispatch(pad_lo, pad_hi, n_used, pos, u2_flat, n_rows, tm, tile):
    t = u2_flat.shape[0] // SLAB
    return pl.pallas_call(
        _dispatch_kernel,
        out_shape=jax.ShapeDtypeStruct((n_rows, SLAB, LANES), F32),
        grid_spec=pltpu.PrefetchScalarGridSpec(
            num_scalar_prefetch=3,
            grid=(t // tm,),
            in_specs=[pl.BlockSpec((2 * tm,), lambda i, *_: (i,), memory_space=pltpu.SMEM),
                      pl.BlockSpec((tm * SLAB, LANES), lambda i, *_: (i, 0))],
            out_specs=pl.BlockSpec(memory_space=pl.ANY),
            scratch_shapes=[pltpu.VMEM((tile, SLAB, LANES), F32), pltpu.SemaphoreType.DMA,
                            pltpu.SemaphoreType.DMA]),
        compiler_params=_cparams("arbitrary"),
        name="moe_dispatch",
    )(pad_lo, pad_hi, n_used, pos, u2_flat)


def _expert_kernel(tile_e_ref, n_used_ref, x_ref, wg_ref, wu_ref, wd_ref, y_ref):
    used = pl.program_id(0) < n_used_ref[0]

    @pl.when(used)
    def _():
        x = _from_slabs(x_ref, x_ref.shape[0] // SLAB).astype(BF16)
        gate = _dot(x, wg_ref[0].astype(BF16))
        hdn = gate * jax.nn.sigmoid(gate) * _dot(x, wu_ref[0].astype(BF16))
        _to_slabs(y_ref, _dot(hdn.astype(BF16), wd_ref[0].astype(BF16)))

    @pl.when(jnp.logical_not(used))
    def _():
        y_ref[...] = jnp.zeros(y_ref.shape, y_ref.dtype)


def _experts(tile_e, n_used, xs_flat, wg, wu, wd, tm):
    _, d, de = wg.shape
    by_tile = lambda i, te, nu: (te[i], 0, 0)
    return pl.pallas_call(
        _expert_kernel,
        out_shape=jax.ShapeDtypeStruct(xs_flat.shape, F32),
        grid_spec=pltpu.PrefetchScalarGridSpec(
            num_scalar_prefetch=2,
            grid=(xs_flat.shape[0] // (tm * SLAB),),
            in_specs=[pl.BlockSpec((tm * SLAB, LANES), lambda i, te, nu: (jnp.minimum(i, nu[0] - 1), 0)),
                      pl.BlockSpec((1, d, de), by_tile), pl.BlockSpec((1, d, de), by_tile),
                      pl.BlockSpec((1, de, d), by_tile)],
            out_specs=pl.BlockSpec((tm * SLAB, LANES), lambda i, te, nu: (i, 0))),
        compiler_params=_cparams("arbitrary"),
        name="moe_experts",
    )(tile_e, n_used, xs_flat, wg, wu, wd)


def _combine_kernel(pos_ref, pos_next_ref, ys_ref, z_ref, h_ref, mod_ref, g_ref, o_ref, buf_ref, sems,
                    *, final_norm):
    tm = h_ref.shape[0]
    i = pl.program_id(0)

    def fetch(p_ref, slot):
        def one(r, carry):
            for k in range(2):
                pltpu.make_async_copy(ys_ref.at[p_ref[k * tm + r]], _slab(buf_ref.at[slot, k], r),
                                      sems.at[slot]).start(priority=k)
            return carry

        lax.fori_loop(0, tm, one, 0, unroll=8)

    @pl.when(i == 0)
    def _():
        fetch(pos_ref, 0)

    @pl.when(i + 1 < pl.num_programs(0))
    def _():
        fetch(pos_next_ref, (i + 1) % 2)

    slot = i % 2
    for k in range(2):
        pltpu.make_async_copy(buf_ref.at[slot, k], buf_ref.at[slot, k], sems.at[slot]).wait()
    z = z_ref[...]
    ffn = z[:, 4:5] * _from_slabs(buf_ref.at[slot, 0], tm) + z[:, 5:6] * _from_slabs(buf_ref.at[slot, 1], tm)
    h = h_ref[...] + mod_ref[0][5:6, :] * ffn
    o_ref[...] = _rms(h, g_ref[...]) if final_norm else h


def _combine(pos, ys, z, h1, mod3, g_final, seq, tm, final_norm):
    t, d = h1.shape
    tpb = seq // tm
    row = lambda i: (i, 0)
    return pl.pallas_call(
        functools.partial(_combine_kernel, final_norm=final_norm),
        out_shape=jax.ShapeDtypeStruct((t, d), F32),
        grid=(t // tm,),
        in_specs=[pl.BlockSpec((2 * tm,), lambda i: (i,), memory_space=pltpu.SMEM),
                  pl.BlockSpec((2 * tm,), lambda i: (jnp.minimum(i + 1, t // tm - 1),), memory_space=pltpu.SMEM),
                  pl.BlockSpec(memory_space=pl.ANY),
                  pl.BlockSpec((tm, LANES), row), pl.BlockSpec((tm, d), row),
                  pl.BlockSpec((1, 6, d), lambda i: (i // tpb, 0, 0)),
                  pl.BlockSpec((1, d), lambda i: (0, 0))],
        out_specs=pl.BlockSpec((tm, d), row),
        scratch_shapes=[pltpu.VMEM((2, 2, tm * SLAB, LANES), F32), pltpu.SemaphoreType.DMA((2,))],
        compiler_params=_cparams("arbitrary"),
        name="moe_combine",
    )(pos, pos, ys, z, h1, mod3, g_final)


def _route_tables(cnt, zt, n_tiles, tms, tile):
    i32 = jnp.int32
    counts = cnt[0, :N_EXPERTS].astype(i32)
    padded = (counts + tile - 1) // tile * tile
    ends = jnp.cumsum(padded)
    starts = ends - padded
    n_used = (ends[-1] // tile).reshape(1)
    tile_row = jnp.minimum(jnp.arange(n_tiles, dtype=i32), n_used - 1) * tile
    tile_e = jnp.minimum(jnp.sum(tile_row[:, None] >= ends[None, :], axis=1), N_EXPERTS - 1).astype(i32)
    pos1 = jnp.take(starts, zt[0].astype(i32)) + zt[2].astype(i32)
    pos2 = jnp.take(starts, zt[1].astype(i32)) + zt[3].astype(i32)
    pos = [jnp.stack([pos1.reshape(-1, tm), pos2.reshape(-1, tm)], axis=1).reshape(-1) for tm in tms]
    return starts + counts, ends, pos, tile_e, n_used.astype(i32)


def kernel(x, c, w_ada, b_ada, g_norm_mix, w_in, b_in, s5_a_re, s5_a_im, s5_b_re, s5_b_im, s5_c_re, s5_c_im, s5_d, s5_log_dt, w_glu, b_glu, lambda_q1, lambda_k1, lambda_q2, lambda_k2, g_subln, w_br_ssm, w_br_attn, w_out, g_norm_ffn, w_router_grp, b_router_grp, w_router_exp, b_router_exp, w_exp_gate, w_exp_up, w_exp_down, g_final):
    nb, seq, d = x.shape
    depth = w_ada.shape[0]
    s5w = s5_d.shape[1]
    daw = w_br_attn.shape[1]
    ng = s5w // S5_GROUP
    nk = seq // S5_STEP
    assert seq % S5_TILE == 0 and ng % 2 == 0
    widths = (daw, daw, daw, 2 * d)
    tm = 512
    tm_moe = 1024
    bq = 512
    row = lambda a: a.reshape(1, -1)
    col = lambda a: a.reshape(-1, 1)

    h = x.reshape(nb * seq, d)
    for l in range(depth):
        lambda_init = 0.8 - 0.6 * math.exp(-0.3 * l)
        mod3 = _ada_mod(c, w_ada[l], row(b_ada[l])).reshape(nb, 6, d)

        w_in_b = w_in[l].astype(BF16)
        q, k, v, gates = _in_proj(h, mod3, row(g_norm_mix[l]), w_in_b[:, s5w:], row(b_in[l][s5w:]),
                                  seq, widths, 2 * tm)

        u_t = _s5_in(h, mod3, row(g_norm_mix[l]), w_in_b[:, :s5w].T, col(b_in[l][:s5w]), seq, S5_TILE)
        ops = _s5_prep(s5_a_re[l], s5_a_im[l], s5_log_dt[l], s5_b_re[l], s5_b_im[l], s5_c_re[l], s5_c_im[l],
                       s5_d[l])
        y_t = _s5_core(u_t, ops, nb, nk)
        ys = _s5_out(y_t, w_glu[l].T.astype(BF16), col(b_glu[l]), w_br_ssm[l].astype(BF16), S5_TILE)

        as3 = lambda a: a.reshape(nb, seq, daw)
        oa = _diff_attn(as3(q), as3(k), as3(v), row(lambda_q1[l]), row(lambda_k1[l]), row(lambda_q2[l]),
                        row(lambda_k2[l]), row(g_subln[l]), lambda_init, bq).reshape(nb * seq, daw)

        w_router = jnp.zeros((d, LANES), F32)
        w_router = w_router.at[:, :N_EXPERTS].set(w_router_exp[l])
        w_router = w_router.at[:, N_EXPERTS:N_EXPERTS + N_GROUPS].set(w_router_grp[l])
        b_router = jnp.zeros((1, LANES), F32)
        b_router = b_router.at[0, :N_EXPERTS].set(b_router_exp[l])
        b_router = b_router.at[0, N_EXPERTS:N_EXPERTS + N_GROUPS].set(b_router_grp[l])
        wr_hi = lax.bitcast_convert_type(
            lax.bitcast_convert_type(w_router, jnp.uint32) & jnp.uint32(0xFFFF0000), F32)
        wr_split = jnp.stack([wr_hi.astype(BF16), (w_router - wr_hi).astype(BF16)])
        h1, u2, z, zt, cnt = _post_mix(ys, oa, gates, h, mod3, w_br_attn[l].astype(BF16),
                                       w_out[l].astype(BF16), row(g_norm_ffn[l]), wr_split, b_router, seq, tm)

        n_tiles = (2 * nb * seq) // EXPERT_TILE + N_EXPERTS
        pad_lo, pad_hi, (pos_out, pos_in), tile_e, n_used = _route_tables(cnt, zt, n_tiles, (tm_moe, tm),
                                                                          EXPERT_TILE)
        n_rows = n_tiles * EXPERT_TILE
        xs = _dispatch(pad_lo, pad_hi, n_used, pos_out, u2, n_rows, tm_moe, EXPERT_TILE)
        ys_e = _experts(tile_e, n_used, xs.reshape(n_rows * SLAB, LANES), w_exp_gate[l], w_exp_up[l],
                        w_exp_down[l], EXPERT_TILE)
        h = _combine(pos_in, ys_e.reshape(n_rows, SLAB, LANES), z, h1, mod3, row(g_final), seq, tm,
                     final_norm=(l == depth - 1))
    return h.reshape(nb, seq, d)
```

```python
import functools
import math

import jax
import jax.numpy as jnp
from jax import lax
from jax.experimental import pallas as pl
from jax.experimental.pallas import tpu as pltpu

EPS = 1e-6
CHUNK = 64
S5_GROUP = 16
S5_STATE = 64
S5_STEP = 16
S5_TILE = 2048
DA_HEADS = 8
DA_HEAD_DIM = 64
N_GROUPS = 4
EXP_PER_GROUP = 8
N_EXPERTS = N_GROUPS * EXP_PER_GROUP
D_EXPERT = 256
EXPERT_TILE = 512
ROUTE_ROWS = 8
LANES = 128
Q_SCALE = DA_HEAD_DIM ** -0.5 * math.log2(math.e)
VMEM_LIMIT = 56 * 1024 * 1024

F32 = jnp.float32
BF16 = jnp.bfloat16
HIGHEST = lax.Precision.HIGHEST


def _cparams(*sem):
    return pltpu.CompilerParams(dimension_semantics=sem, vmem_limit_bytes=VMEM_LIMIT)


def _dot(a, b):
    return jnp.dot(a, b, preferred_element_type=F32)


def _dot_nt(a, b):
    return lax.dot_general(a, b, (((1,), (1,)), ((), ())), preferred_element_type=F32)


def _rms(x, g):
    return x * lax.rsqrt(jnp.mean(x * x, axis=-1, keepdims=True) + EPS) * g


def _ada_kernel(c_ref, w_ref, b_ref, o_ref):
    c = c_ref[...]
    cs = c * jax.nn.sigmoid(c)
    o_ref[...] = jnp.dot(cs, w_ref[...], preferred_element_type=F32, precision=HIGHEST) + b_ref[...]


def _ada_mod(c, w, b):
    nb, d = c.shape
    n = w.shape[1]
    tn = n // 4
    return pl.pallas_call(
        _ada_kernel,
        out_shape=jax.ShapeDtypeStruct((nb, n), F32),
        grid=(n // tn,),
        in_specs=[pl.BlockSpec((nb, d), lambda j: (0, 0)),
                  pl.BlockSpec((d, tn), lambda j: (0, j)),
                  pl.BlockSpec((1, tn), lambda j: (0, j))],
        out_specs=pl.BlockSpec((nb, tn), lambda j: (0, j)),
        compiler_params=_cparams("parallel"),
        name="ada_mod",
    )(c, w, b)


def _modulated_norm(x_ref, mod_ref, g_ref):
    mod = mod_ref[0]
    return _rms(x_ref[...], g_ref[...]) * (1.0 + mod[1:2, :]) + mod[0:1, :]


def _inproj_kernel(x_ref, mod_ref, g_ref, w_ref, b_ref, q_ref, k_ref, v_ref, gt_ref, *, widths):
    ub = _modulated_norm(x_ref, mod_ref, g_ref).astype(BF16)
    qw, kw, vw, gw = widths
    o = 0
    q_ref[...] = ((_dot(ub, w_ref[:, o:o + qw]) + b_ref[:, o:o + qw]) * Q_SCALE).astype(BF16)
    o += qw
    k_ref[...] = (_dot(ub, w_ref[:, o:o + kw]) + b_ref[:, o:o + kw]).astype(BF16)
    o += kw
    v_ref[...] = (_dot(ub, w_ref[:, o:o + vw]) + b_ref[:, o:o + vw]).astype(BF16)
    o += vw
    gt_ref[...] = jax.nn.sigmoid(_dot(ub, w_ref[:, o:o + gw]) + b_ref[:, o:o + gw]).astype(BF16)


def _in_proj(x2, mod3, g, w, b, seq, widths, tm):
    t, d = x2.shape
    tpb = seq // tm
    row = lambda i: (i, 0)
    full = lambda a: pl.BlockSpec(a.shape, lambda i: (0, 0), pipeline_mode=pl.Buffered(1))
    return pl.pallas_call(
        functools.partial(_inproj_kernel, widths=widths),
        out_shape=[jax.ShapeDtypeStruct((t, wd), BF16) for wd in widths],
        grid=(t // tm,),
        in_specs=[pl.BlockSpec((tm, d), row), pl.BlockSpec((1, 6, d), lambda i: (i // tpb, 0, 0)),
                  full(g), full(w), full(b)],
        out_specs=[pl.BlockSpec((tm, wd), row) for wd in widths],
        compiler_params=_cparams("parallel"),
        name="in_proj",
    )(x2, mod3, g, w, b)


def _s5_in_kernel(x_ref, mod_ref, g_ref, wst_ref, bs_ref, ut_ref, slab_ref):
    u = _modulated_norm(x_ref, mod_ref, g_ref)
    nslab = slab_ref.shape[0]
    nk = x_ref.shape[0] // S5_STEP
    for j in range(nslab):
        slab_ref[j] = u[:, j * LANES:(j + 1) * LANES]
    by_step = jnp.concatenate(
        [jnp.concatenate([slab_ref[j, pl.ds(s, nk, stride=S5_STEP), :] for j in range(nslab)], axis=1).astype(BF16)
         for s in range(S5_STEP)], axis=0)
    ut = (_dot_nt(wst_ref[...], by_step) + bs_ref[...]).astype(BF16)
    for s in range(S5_STEP):
        ut_ref[s] = ut[:, s * nk:(s + 1) * nk]


def _s5_in(x2, mod3, g, wst, bs, seq, tm):
    t, d = x2.shape
    s5w = wst.shape[0]
    nkt = tm // S5_STEP
    tpb = seq // tm
    full = lambda a: pl.BlockSpec(a.shape, lambda i: (0, 0))
    return pl.pallas_call(
        _s5_in_kernel,
        out_shape=jax.ShapeDtypeStruct((S5_STEP, s5w, t // S5_STEP), BF16),
        grid=(t // tm,),
        in_specs=[pl.BlockSpec((tm, d), lambda i: (i, 0)), pl.BlockSpec((1, 6, d), lambda i: (i // tpb, 0, 0)),
                  full(g), full(wst), full(bs)],
        out_specs=pl.BlockSpec((S5_STEP, s5w, nkt), lambda i: (0, 0, i)),
        scratch_shapes=[pltpu.VMEM((d // LANES, tm, LANES), F32)],
        compiler_params=_cparams("parallel"),
        name="s5_in",
    )(x2, mod3, g, wst, bs)


def _s5_out_kernel(yt_ref, wglut_ref, bglu_ref, wbs_ref, o_ref, slab_ref):
    nslab = slab_ref.shape[0]
    nk = yt_ref.shape[2]
    yt = jnp.concatenate([yt_ref[s] for s in range(S5_STEP)], axis=1)
    gated = yt.astype(F32) * jax.nn.sigmoid(_dot(wglut_ref[...], yt) + bglu_ref[...])
    for s in range(S5_STEP):
        yg = gated[:, s * nk:(s + 1) * nk].T
        for j in range(nslab):
            slab_ref[j, pl.ds(s, nk, stride=S5_STEP), :] = yg[:, j * LANES:(j + 1) * LANES]
    yg = jnp.concatenate([slab_ref[j] for j in range(nslab)], axis=1)
    o_ref[...] = _dot(yg.astype(BF16), wbs_ref[...]).astype(BF16)


def _s5_out(y_t, wglut, bglu, wbs, tm):
    n, s5w, r = y_t.shape
    d = wbs.shape[1]
    nkt = tm // n
    full = lambda a: pl.BlockSpec(a.shape, lambda i: (0, 0))
    return pl.pallas_call(
        _s5_out_kernel,
        out_shape=jax.ShapeDtypeStruct((r * n, d), BF16),
        grid=(r // nkt,),
        in_specs=[pl.BlockSpec((n, s5w, nkt), lambda i: (0, 0, i)), full(wglut), full(bglu), full(wbs)],
        out_specs=pl.BlockSpec((tm, d), lambda i: (i, 0)),
        scratch_shapes=[pltpu.VMEM((s5w // LANES, tm, LANES), F32)],
        compiler_params=_cparams("parallel"),
        name="s5_out",
    )(y_t, wglut, bglu, wbs)


def _s5_prep_kernel(lrc_ref, lic_ref, dtc_ref, lrr_ref, lir_ref, dtr_ref, br_ref, bi_ref, cr_ref, ci_ref,
                    mt_ref, wtr_ref, wti_ref, vtr_ref, vti_ref, a16r_ref, a16i_ref):
    n, gc, ns = S5_STEP, S5_GROUP, S5_STATE
    w = n * gc
    iota = lambda shape, dim: lax.broadcasted_iota(jnp.int32, shape, dim)
    hdot = lambda a, b: jnp.dot(a, b, preferred_element_type=F32, precision=HIGHEST)

    def power(lr, li, dt, k):
        mag = jnp.exp(lr * dt * k)
        return mag * jnp.cos(li * dt * k), mag * jnp.sin(li * dt * k)

    lr, li, dt = lrc_ref[0], lic_ref[0], jnp.exp(dtc_ref[0])
    abr, abi = power(lr, li, dt, 1.0)
    den = lr * lr + li * li
    nr = abr - 1.0
    f_re = (nr * lr + abi * li) / den
    f_im = (abi * lr - nr * li) / den
    bb_re = f_re * br_ref[0] - f_im * bi_ref[0]
    bb_im = f_re * bi_ref[0] + f_im * br_ref[0]
    spread = jnp.where(iota((gc, 2 * w), 1) % gc == iota((gc, 2 * w), 0), 1.0, 0.0)
    bt_re, bt_im = hdot(bb_re, spread), hdot(bb_im, spread)
    k_in = (n - 1 - (iota((1, 2 * w), 1) % w) // gc).astype(F32)
    p_re, p_im = power(lr, li, dt, k_in)
    e_re = p_re * bt_re - p_im * bt_im
    e_im = p_re * bt_im + p_im * bt_re
    own = iota((2 * ns, 2 * w), 0) // ns == iota((2 * ns, 2 * w), 1) // w
    wtr_ref[0] = jnp.where(own, e_re, 0.0).astype(BF16)
    wti_ref[0] = jnp.where(own, e_im, 0.0).astype(BF16)

    lr_r, li_r, dt_r = lrr_ref[0], lir_ref[0], jnp.exp(dtr_ref[0])
    c_r, c_i = power(lr_r, li_r, dt_r, 1.0)
    for _ in range(n.bit_length() - 1):
        c_r, c_i = c_r * c_r - c_i * c_i, 2.0 * c_r * c_i
    a16r_ref[0], a16i_ref[0] = c_r, c_i
    k_out = ((iota((2 * w, 1), 0) % w) // gc + 1).astype(F32)
    q_re, q_im = power(lr_r, li_r, dt_r, k_out)
    c_re = jnp.concatenate([cr_ref[0]] * (2 * n), axis=0)
    c_im = jnp.concatenate([ci_ref[0]] * (2 * n), axis=0)
    own = iota((2 * w, 2 * ns), 0) // w == iota((2 * w, 2 * ns), 1) // ns
    vtr_ref[0] = jnp.where(own, c_re * q_re - c_im * q_im, 0.0).astype(BF16)
    vti_ref[0] = jnp.where(own, -(c_re * q_im + c_im * q_re), 0.0).astype(BF16)

    group_of_lane = iota((gc, 2 * ns), 1) // ns
    for h in range(2):
        mine = group_of_lane == h
        krev = (hdot(jnp.where(mine, cr_ref[0], 0.0), e_re[:, :w])
                - hdot(jnp.where(mine, ci_ref[0], 0.0), e_im[:, :w]))
        strip = jnp.concatenate([krev, jnp.zeros_like(krev)], axis=1)
        rows = []
        for t in range(n):
            shift = (n - 1 - t) * gc
            rows.append((pltpu.roll(strip, 2 * w - shift, 1) if shift else strip)[:, :w])
        mt_ref[0, h * w:(h + 1) * w, h * w:(h + 1) * w] = jnp.concatenate(rows, axis=0).astype(BF16)
        mt_ref[0, h * w:(h + 1) * w, (1 - h) * w:(2 - h) * w] = jnp.zeros((w, w), BF16)


def _s5_prep(a_re, a_im, log_dt, b_re, b_im, c_re, c_im, d_skip):
    ng, ns = a_re.shape
    gc = b_re.shape[-1]
    na = ng // 2
    w2 = 2 * S5_STEP * gc
    col = lambda a: a.reshape(na, 2 * ns, 1)
    row = lambda a: a.reshape(na, 1, 2 * ns)
    per_state = jnp.repeat(log_dt, ns)
    pair_c = lambda a: a.reshape(na, 2, gc, ns).transpose(0, 2, 1, 3).reshape(na, gc, 2 * ns)
    g3 = lambda g: (g, 0, 0)
    spec = lambda *shape: pl.BlockSpec((1,) + shape, g3)
    shp = lambda *shape, dt=BF16: jax.ShapeDtypeStruct((na,) + shape, dt)
    ops = pl.pallas_call(
        _s5_prep_kernel,
        out_shape=[shp(w2, w2), shp(2 * ns, w2), shp(2 * ns, w2), shp(w2, 2 * ns), shp(w2, 2 * ns),
                   shp(1, 2 * ns, dt=F32), shp(1, 2 * ns, dt=F32)],
        grid=(na,),
        in_specs=[spec(2 * ns, 1)] * 3 + [spec(1, 2 * ns)] * 3 + [spec(2 * ns, gc)] * 2 + [spec(gc, 2 * ns)] * 2,
        out_specs=[spec(w2, w2), spec(2 * ns, w2), spec(2 * ns, w2), spec(w2, 2 * ns), spec(w2, 2 * ns),
                   spec(1, 2 * ns), spec(1, 2 * ns)],
        compiler_params=_cparams("parallel"),
        name="s5_prep",
    )(col(a_re), col(a_im), col(per_state), row(a_re), row(a_im), row(per_state),
      b_re.reshape(na, 2 * ns, gc), b_im.reshape(na, 2 * ns, gc), pair_c(c_re), pair_c(c_im))
    d_col = jnp.broadcast_to(d_skip.reshape(na, 2, 1, gc), (na, 2, S5_STEP, gc)).reshape(na, w2, 1)
    return (*ops, d_col)


def _s5_core_kernel(u_ref, m_ref, wr_ref, wi_ref, vr_ref, vi_ref, ar_ref, ai_ref, d_ref, o_ref,
                    sr_ref, si_ref, xr_ref, xi_ref, *, nb, nk):
    n, gw, r = u_ref.shape
    gc = gw // 2
    u = jnp.concatenate([u_ref[:, h * gc:(h + 1) * gc, :].reshape(n * gc, r) for h in range(2)], axis=0)
    sr_ref[...] = _dot(wr_ref[0], u).T
    si_ref[...] = _dot(wi_ref[0], u).T
    a_r = ar_ref[0]
    a_i = ai_ref[0]

    def step(k, carry):
        out = []
        for b in range(nb):
            x_r, x_i = carry[2 * b], carry[2 * b + 1]
            row = pl.ds(b * nk + k, 1)
            xr_ref[row, :] = x_r
            xi_ref[row, :] = x_i
            out.append(a_r * x_r - a_i * x_i + sr_ref[row, :])
            out.append(a_r * x_i + a_i * x_r + si_ref[row, :])
        return tuple(out)

    zero = jnp.zeros((1, xr_ref.shape[1]), F32)
    lax.fori_loop(0, nk, step, (zero,) * (2 * nb), unroll=4)
    y = (_dot(m_ref[0], u) + _dot(vr_ref[0], xr_ref[...].T.astype(BF16))
         + _dot(vi_ref[0], xi_ref[...].T.astype(BF16)))
    y = y + d_ref[0] * u.astype(F32)
    y = jax.nn.gelu(y).astype(BF16)
    for h in range(2):
        o_ref[:, h * gc:(h + 1) * gc, :] = y[h * n * gc:(h + 1) * n * gc].reshape(n, gc, r)


def _s5_core(u_t, ops, nb, nk):
    m, wr, wi, vr, vi, ar, ai, d_col = ops
    npair, w2, _ = m.shape
    p2 = wr.shape[1]
    n, _, r = u_t.shape
    gw = w2 // n
    g3 = lambda g: (g, 0, 0)
    data = pl.BlockSpec((n, gw, r), lambda g: (0, g, 0))
    return pl.pallas_call(
        functools.partial(_s5_core_kernel, nb=nb, nk=nk),
        out_shape=jax.ShapeDtypeStruct(u_t.shape, BF16),
        grid=(npair,),
        in_specs=[data, pl.BlockSpec((1, w2, w2), g3),
                  pl.BlockSpec((1, p2, w2), g3), pl.BlockSpec((1, p2, w2), g3),
                  pl.BlockSpec((1, w2, p2), g3), pl.BlockSpec((1, w2, p2), g3),
                  pl.BlockSpec((1, 1, p2), g3), pl.BlockSpec((1, 1, p2), g3),
                  pl.BlockSpec((1, w2, 1), g3)],
        out_specs=data,
        scratch_shapes=[pltpu.VMEM((r, p2), F32)] * 4,
        compiler_params=_cparams("parallel"),
        name="s5_core",
    )(u_t, m, wr, wi, vr, vi, ar, ai, d_col)


def _attn_kernel(q_ref, k_ref, v_ref, lq1_ref, lk1_ref, lq2_ref, lk2_ref, g_ref, o_ref,
                 m_ref, acc_ref, *, bq, lambda_init):
    qi = pl.program_id(2)
    q = q_ref[0]
    lane = lax.broadcasted_iota(jnp.int32, (1, 2 * DA_HEAD_DIM), 1)
    zero = jnp.zeros_like(q)
    qs = jnp.concatenate([jnp.where(lane < DA_HEAD_DIM, q, zero),
                          jnp.where(lane >= DA_HEAD_DIM, q, zero)], axis=0)
    hw = 2 * DA_HEAD_DIM
    m_ref[...] = jnp.full(m_ref.shape, -jnp.inf, F32)
    acc_ref[...] = jnp.zeros(acc_ref.shape, F32)
    def absorb(start, size, masked):
        keys = pl.ds(pl.multiple_of(start, bq), size)
        s = _dot_nt(qs, k_ref[0, keys, :])
        if masked:
            r_chunk = (lax.broadcasted_iota(jnp.int32, s.shape, 0) % bq) // CHUNK
            c_chunk = lax.broadcasted_iota(jnp.int32, s.shape, 1) // CHUNK
            s = jnp.where(c_chunk <= r_chunk, s, -jnp.inf)
        m_old = m_ref[...]
        m_new = jnp.maximum(m_old, jnp.max(s, axis=-1, keepdims=True))
        alpha = jnp.exp2(m_old - m_new)
        p = jnp.exp2((s - jnp.concatenate([m_new] * (size // LANES), axis=1)).astype(BF16))
        v_ext = jnp.concatenate([v_ref[0, keys, :], jnp.ones((size, hw), BF16)], axis=1)
        pv = jnp.concatenate([_dot(p[:bq], v_ext), _dot(p[bq:], v_ext)], axis=0)
        acc_ref[...] = jnp.concatenate([alpha, alpha], axis=1) * acc_ref[...] + pv
        m_ref[...] = m_new

    def visible_pair(jj, carry):
        absorb(2 * jj * bq, 2 * bq, False)
        return carry

    lax.fori_loop(0, qi // 2, visible_pair, 0)

    @pl.when(qi % 2 == 1)
    def _():
        absorb((qi - 1) * bq, bq, False)

    absorb(qi * bq, bq, True)

    lam = (jnp.exp(jnp.sum(lq1_ref[...] * lk1_ref[...], axis=-1, keepdims=True))
           - jnp.exp(jnp.sum(lq2_ref[...] * lk2_ref[...], axis=-1, keepdims=True)) + lambda_init)
    acc = acc_ref[...]
    o_all = acc[:, :hw] / acc[:, hw:]
    o = o_all[:bq] - lam * o_all[bq:]
    o_ref[0] = (_rms(o, g_ref[...]) * (1.0 - lambda_init)).astype(BF16)


def _diff_attn(q3, k3, v3, lq1, lk1, lq2, lk2, g_subln, lambda_init, bq):
    nb, seq, _ = q3.shape
    hw = 2 * DA_HEAD_DIM
    lam_spec = pl.BlockSpec((1, DA_HEAD_DIM), lambda b, h, i: (0, 0))
    kv_spec = pl.BlockSpec((1, seq, hw), lambda b, h, i: (b, 0, h))
    return pl.pallas_call(
        functools.partial(_attn_kernel, bq=bq, lambda_init=lambda_init),
        out_shape=jax.ShapeDtypeStruct(q3.shape, BF16),
        grid=(nb, DA_HEADS, seq // bq),
        in_specs=[pl.BlockSpec((1, bq, hw), lambda b, h, i: (b, i, h)), kv_spec, kv_spec,
                  lam_spec, lam_spec, lam_spec, lam_spec,
                  pl.BlockSpec((1, hw), lambda b, h, i: (0, 0))],
        out_specs=pl.BlockSpec((1, bq, hw), lambda b, h, i: (b, i, h)),
        scratch_shapes=[pltpu.VMEM((2 * bq, hw), F32), pltpu.VMEM((2 * bq, 2 * hw), F32)],
        compiler_params=_cparams("parallel", "parallel", "parallel"),
        name="diff_attn",
    )(q3, k3, v3, lq1, lk1, lq2, lk2, g_subln)


def _post_kernel(ys_ref, oa_ref, gt_ref, x_ref, mod_ref, wba_ref, wout_ref, g_ref, wr_ref, br_ref,
                 h_ref, u2_ref, z_ref, zt_ref, cnt_ref, base_ref):
    d = x_ref.shape[1]

    @pl.when(pl.program_id(0) == 0)
    def _():
        base_ref[...] = jnp.zeros(base_ref.shape, F32)

    mod = mod_ref[0]
    y_att = _dot(oa_ref[...], wba_ref[...])
    gates = gt_ref[...].astype(F32)
    mix_in = gates[:, :d] * ys_ref[...].astype(F32) + gates[:, d:] * y_att
    mix = _dot(mix_in.astype(BF16), wout_ref[...])
    h = x_ref[...] + mod[2:3, :] * mix
    h_ref[...] = h
    u2 = _rms(h, g_ref[...]) * (1.0 + mod[4:5, :]) + mod[3:4, :]
    _to_slabs(u2_ref, u2)

    u2_hi = pltpu.bitcast(pltpu.bitcast(u2, jnp.uint32) & jnp.uint32(0xFFFF0000), F32)
    u2_lo = (u2 - u2_hi).astype(BF16)
    u2_hi = u2_hi.astype(BF16)
    hi_w = _dot(u2_hi, wr_ref[...])
    logits = hi_w[:, :LANES] + hi_w[:, LANES:] + _dot(u2_lo, wr_ref[:, :LANES]) + br_ref[...]
    lane = lax.broadcasted_iota(jnp.int32, logits.shape, 1)
    neg = jnp.full_like(logits, -jnp.inf)
    big = jnp.full_like(lane, LANES)
    is_grp = (lane >= N_EXPERTS) & (lane < N_EXPERTS + N_GROUPS)
    glog = jnp.where(is_grp, logits, neg)
    gmax = jnp.max(glog, axis=-1, keepdims=True)
    gp = 1.0 / jnp.sum(jnp.exp(glog - gmax), axis=-1, keepdims=True)
    gi = jnp.min(jnp.where(glog == gmax, lane, big), axis=-1, keepdims=True) - N_EXPERTS
    in_grp = (lane < N_EXPERTS) & ((lane // EXP_PER_GROUP) == gi)
    elog = jnp.where(in_grp, logits, neg)
    v1 = jnp.max(elog, axis=-1, keepdims=True)
    i1 = jnp.min(jnp.where(elog == v1, lane, big), axis=-1, keepdims=True)
    elog2 = jnp.where(lane == i1, neg, elog)
    v2 = jnp.max(elog2, axis=-1, keepdims=True)
    i2 = jnp.min(jnp.where(elog2 == v2, lane, big), axis=-1, keepdims=True)
    e2 = jnp.exp(v2 - v1)
    w1 = gp / (1.0 + e2)
    w2 = gp * e2 / (1.0 + e2)
    tm = logits.shape[0]
    hot = jnp.where((lane == i1) | (lane == i2), 1.0, 0.0)
    earlier = (lax.broadcasted_iota(jnp.int32, (tm, tm), 1) < lax.broadcasted_iota(jnp.int32, (tm, tm), 0))
    before = _dot(jnp.where(earlier, 1.0, 0.0).astype(BF16), hot.astype(BF16)) + base_ref[...]
    rank1 = jnp.sum(jnp.where(lane == i1, before, 0.0), axis=-1, keepdims=True)
    rank2 = jnp.sum(jnp.where(lane == i2, before, 0.0), axis=-1, keepdims=True)
    base_ref[...] += jnp.sum(hot, axis=0, keepdims=True)
    cnt_ref[...] = base_ref[...]
    z = jnp.zeros_like(logits)
    for k, val in enumerate((i1.astype(F32), i2.astype(F32), rank1, rank2, w1, w2)):
        z = jnp.where(lane == k, val, z)
    z_ref[...] = z
    zt_ref[...] = z.T[:ROUTE_ROWS, :]


def _post_mix(ys, oa, gates, x2, mod3, wba, wout, g, wr, br, seq, tm):
    t, d = x2.shape
    tpb = seq // tm
    row = lambda i: (i, 0)
    full = lambda a: pl.BlockSpec(a.shape, lambda i: (0,) * a.ndim)
    return pl.pallas_call(
        _post_kernel,
        out_shape=[jax.ShapeDtypeStruct((t, d), F32), jax.ShapeDtypeStruct((t * SLAB, LANES), F32),
                   jax.ShapeDtypeStruct((t, LANES), F32), jax.ShapeDtypeStruct((ROUTE_ROWS, t), F32),
                   jax.ShapeDtypeStruct((1, LANES), F32)],
        grid=(t // tm,),
        in_specs=[pl.BlockSpec((tm, d), row), pl.BlockSpec((tm, oa.shape[1]), row),
                  pl.BlockSpec((tm, 2 * d), row), pl.BlockSpec((tm, d), row),
                  pl.BlockSpec((1, 6, d), lambda i: (i // tpb, 0, 0)),
                  full(wba), full(wout), full(g), full(wr), full(br)],
        out_specs=[pl.BlockSpec((tm, d), row), pl.BlockSpec((tm * SLAB, LANES), row),
                   pl.BlockSpec((tm, LANES), row),
                   pl.BlockSpec((ROUTE_ROWS, tm), lambda i: (0, i)), pl.BlockSpec((1, LANES), lambda i: (0, 0))],
        scratch_shapes=[pltpu.VMEM((1, LANES), F32)],
        compiler_params=_cparams("arbitrary"),
        name="post_mix",
    )(ys, oa, gates, x2, mod3, wba, wout, g, wr, br)


SLAB = 8


def _to_slabs(ref, val):
    n = val.shape[0]
    for j in range(SLAB):
        ref[pl.ds(j, n, stride=SLAB), :] = val[:, j * LANES:(j + 1) * LANES]


def _from_slabs(ref, n):
    return jnp.concatenate([ref[pl.ds(j, n, stride=SLAB), :] for j in range(SLAB)], axis=1)


def _slab(flat_ref, r):
    return flat_ref.at[pl.ds(pl.multiple_of(r * SLAB, SLAB), SLAB), :]


def _dispatch_kernel(pad_lo_ref, pad_hi_ref, n_used_ref, pos_ref, u_ref, xs_ref, zero_ref, sem, pad_sem):
    tm = u_ref.shape[0] // SLAB
    tile = zero_ref.shape[0]
    n_tiles = xs_ref.shape[0] // tile

    def send(r, carry):
        for k in range(2):
            pltpu.make_async_copy(_slab(u_ref, r), xs_ref.at[pos_ref[k * tm + r]], sem).start(priority=k)
        return carry

    lax.fori_loop(0, tm, send, 0, unroll=8)

    @pl.when(pl.program_id(0) == pl.num_programs(0) - 1)
    def _():
        zero_ref[...] = jnp.zeros(zero_ref.shape, zero_ref.dtype)

        def pad_copies(e, wait):
            lo = pad_lo_ref[e]
            n = pad_hi_ref[e] - lo
            for b in range(tile.bit_length() - 1):
                size = 1 << b

                @pl.when((n >> b) & 1 == 1)
                def _():
                    cp = pltpu.make_async_copy(zero_ref.at[pl.ds(0, size)],
                                               xs_ref.at[pl.ds(lo + (n & (size - 1)), size)], pad_sem)
                    cp.wait() if wait else cp.start()

        def tile_copy(i):
            return pltpu.make_async_copy(zero_ref, xs_ref.at[pl.ds(i * tile, tile)], pad_sem)

        for wait in (False, True):
            @pl.loop(0, N_EXPERTS)
            def _(e):
                pad_copies(e, wait)

            @pl.loop(n_used_ref[0], n_tiles)
            def _(j):
                tile_copy(j).wait() if wait else tile_copy(j).start()

    for _ in range(2 * tm // tile):
        pltpu.make_async_copy(zero_ref, xs_ref.at[pl.ds(0, tile)], sem).wait()


def _dispatch(pad_lo, pad_hi, n_used, pos, u2_flat, n_rows, tm, tile):
    t = u2_flat.shape[0] // SLAB
    return pl.pallas_call(
        _dispatch_kernel,
        out_shape=jax.ShapeDtypeStruct((n_rows, SLAB, LANES), F32),
        grid_spec=pltpu.PrefetchScalarGridSpec(
            num_scalar_prefetch=3,
            grid=(t // tm,),
            in_specs=[pl.BlockSpec((2 * tm,), lambda i, *_: (i,), memory_space=pltpu.SMEM),
                      pl.BlockSpec((tm * SLAB, LANES), lambda i, *_: (i, 0))],
            out_specs=pl.BlockSpec(memory_space=pl.ANY),
            scratch_shapes=[pltpu.VMEM((tile, SLAB, LANES), F32), pltpu.SemaphoreType.DMA,
                            pltpu.SemaphoreType.DMA]),
        compiler_params=_cparams("arbitrary"),
        name="moe_dispatch",
    )(pad_lo, pad_hi, n_used, pos, u2_flat)


def _expert_kernel(tile_e_ref, n_used_ref, x_ref, wg_ref, wu_ref, wd_ref, y_ref):
    used = pl.program_id(0) < n_used_ref[0]

    @pl.when(used)
    def _():
        x = _from_slabs(x_ref, x_ref.shape[0] // SLAB).astype(BF16)
        gate = _dot(x, wg_ref[0].astype(BF16))
        hdn = gate * jax.nn.sigmoid(gate) * _dot(x, wu_ref[0].astype(BF16))
        _to_slabs(y_ref, _dot(hdn.astype(BF16), wd_ref[0].astype(BF16)))

    @pl.when(jnp.logical_not(used))
    def _():
        y_ref[...] = jnp.zeros(y_ref.shape, y_ref.dtype)


def _experts(tile_e, n_used, xs_flat, wg, wu, wd, tm):
    _, d, de = wg.shape
    by_tile = lambda i, te, nu: (te[i], 0, 0)
    return pl.pallas_call(
        _expert_kernel,
        out_shape=jax.ShapeDtypeStruct(xs_flat.shape, F32),
        grid_spec=pltpu.PrefetchScalarGridSpec(
            num_scalar_prefetch=2,
            grid=(xs_flat.shape[0] // (tm * SLAB),),
            in_specs=[pl.BlockSpec((tm * SLAB, LANES), lambda i, te, nu: (jnp.minimum(i, nu[0] - 1), 0)),
                      pl.BlockSpec((1, d, de), by_tile), pl.BlockSpec((1, d, de), by_tile),
                      pl.BlockSpec((1, de, d), by_tile)],
            out_specs=pl.BlockSpec((tm * SLAB, LANES), lambda i, te, nu: (i, 0))),
        compiler_params=_cparams("arbitrary"),
        name="moe_experts",
    )(tile_e, n_used, xs_flat, wg, wu, wd)


def _combine_kernel(pos_ref, pos_next_ref, ys_ref, z_ref, h_ref, mod_ref, g_ref, o_ref, buf_ref, sems,
                    *, final_norm):
    tm = h_ref.shape[0]
    i = pl.program_id(0)

    def fetch(p_ref, slot):
        def one(r, carry):
            for k in range(2):
                pltpu.make_async_copy(ys_ref.at[p_ref[k * tm + r]], _slab(buf_ref.at[slot, k], r),
                                      sems.at[slot]).start(priority=k)
            return carry

        lax.fori_loop(0, tm, one, 0, unroll=8)

    @pl.when(i == 0)
    def _():
        fetch(pos_ref, 0)

    @pl.when(i + 1 < pl.num_programs(0))
    def _():
        fetch(pos_next_ref, (i + 1) % 2)

    slot = i % 2
    for k in range(2):
        pltpu.make_async_copy(buf_ref.at[slot, k], buf_ref.at[slot, k], sems.at[slot]).wait()
    z = z_ref[...]
    ffn = z[:, 4:5] * _from_slabs(buf_ref.at[slot, 0], tm) + z[:, 5:6] * _from_slabs(buf_ref.at[slot, 1], tm)
    h = h_ref[...] + mod_ref[0][5:6, :] * ffn
    o_ref[...] = _rms(h, g_ref[...]) if final_norm else h


def _combine(pos, ys, z, h1, mod3, g_final, seq, tm, final_norm):
    t, d = h1.shape
    tpb = seq // tm
    row = lambda i: (i, 0)
    return pl.pallas_call(
        functools.partial(_combine_kernel, final_norm=final_norm),
        out_shape=jax.ShapeDtypeStruct((t, d), F32),
        grid=(t // tm,),
        in_specs=[pl.BlockSpec((2 * tm,), lambda i: (i,), memory_space=pltpu.SMEM),
                  pl.BlockSpec((2 * tm,), lambda i: (jnp.minimum(i + 1, t // tm - 1),), memory_space=pltpu.SMEM),
                  pl.BlockSpec(memory_space=pl.ANY),
                  pl.BlockSpec((tm, LANES), row), pl.BlockSpec((tm, d), row),
                  pl.BlockSpec((1, 6, d), lambda i: (i // tpb, 0, 0)),
                  pl.BlockSpec((1, d), lambda i: (0, 0))],
        out_specs=pl.BlockSpec((tm, d), row),
        scratch_shapes=[pltpu.VMEM((2, 2, tm * SLAB, LANES), F32), pltpu.SemaphoreType.DMA((2,))],
        compiler_params=_cparams("arbitrary"),
        name="moe_combine",
    )(pos, pos, ys, z, h1, mod3, g_final)


def _route_tables(cnt, zt, n_tiles, tms, tile):
    i32 = jnp.int32
    counts = cnt[0, :N_EXPERTS].astype(i32)
    padded = (counts + tile - 1) // tile * tile
    ends = jnp.cumsum(padded)
    starts = ends - padded
    n_used = (ends[-1] // tile).reshape(1)
    tile_row = jnp.minimum(jnp.arange(n_tiles, dtype=i32), n_used - 1) * tile
    tile_e = jnp.minimum(jnp.sum(tile_row[:, None] >= ends[None, :], axis=1), N_EXPERTS - 1).astype(i32)
    pos1 = jnp.take(starts, zt[0].astype(i32)) + zt[2].astype(i32)
    pos2 = jnp.take(starts, zt[1].astype(i32)) + zt[3].astype(i32)
    pos = [jnp.stack([pos1.reshape(-1, tm), pos2.reshape(-1, tm)], axis=1).reshape(-1) for tm in tms]
    return starts + counts, ends, pos, tile_e, n_used.astype(i32)


def kernel(x, c, w_ada, b_ada, g_norm_mix, w_in, b_in, s5_a_re, s5_a_im, s5_b_re, s5_b_im, s5_c_re, s5_c_im, s5_d, s5_log_dt, w_glu, b_glu, lambda_q1, lambda_k1, lambda_q2, lambda_k2, g_subln, w_br_ssm, w_br_attn, w_out, g_norm_ffn, w_router_grp, b_router_grp, w_router_exp, b_router_exp, w_exp_gate, w_exp_up, w_exp_down, g_final):
    nb, seq, d = x.shape
    depth = w_ada.shape[0]
    s5w = s5_d.shape[1]
    daw = w_br_attn.shape[1]
    ng = s5w // S5_GROUP
    nk = seq // S5_STEP
    assert seq % S5_TILE == 0 and ng % 2 == 0
    widths = (daw, daw, daw, 2 * d)
    tm = 512
    tm_moe = 1024
    bq = 512
    row = lambda a: a.reshape(1, -1)
    col = lambda a: a.reshape(-1, 1)

    h = x.reshape(nb * seq, d)
    for l in range(depth):
        lambda_init = 0.8 - 0.6 * math.exp(-0.3 * l)
        mod3 = _ada_mod(c, w_ada[l], row(b_ada[l])).reshape(nb, 6, d)

        w_in_b = w_in[l].astype(BF16)
        q, k, v, gates = _in_proj(h, mod3, row(g_norm_mix[l]), w_in_b[:, s5w:], row(b_in[l][s5w:]),
                                  seq, widths, 2 * tm)

        u_t = _s5_in(h, mod3, row(g_norm_mix[l]), w_in_b[:, :s5w].T, col(b_in[l][:s5w]), seq, S5_TILE)
        ops = _s5_prep(s5_a_re[l], s5_a_im[l], s5_log_dt[l], s5_b_re[l], s5_b_im[l], s5_c_re[l], s5_c_im[l],
                       s5_d[l])
        y_t = _s5_core(u_t, ops, nb, nk)
        ys = _s5_out(y_t, w_glu[l].T.astype(BF16), col(b_glu[l]), w_br_ssm[l].astype(BF16), S5_TILE)

        as3 = lambda a: a.reshape(nb, seq, daw)
        oa = _diff_attn(as3(q), as3(k), as3(v), row(lambda_q1[l]), row(lambda_k1[l]), row(lambda_q2[l]),
                        row(lambda_k2[l]), row(g_subln[l]), lambda_init, bq).reshape(nb * seq, daw)

        w_router = jnp.zeros((d, LANES), F32)
        w_router = w_router.at[:, :N_EXPERTS].set(w_router_exp[l])
        w_router = w_router.at[:, N_EXPERTS:N_EXPERTS + N_GROUPS].set(w_router_grp[l])
        b_router = jnp.zeros((1, LANES), F32)
        b_router = b_router.at[0, :N_EXPERTS].set(b_router_exp[l])
        b_router = b_router.at[0, N_EXPERTS:N_EXPERTS + N_GROUPS].set(b_router_grp[l])
        wr_hi = lax.bitcast_convert_type(
            lax.bitcast_convert_type(w_router, jnp.uint32) & jnp.uint32(0xFFFF0000), F32)
        wr_split = jnp.concatenate([wr_hi.astype(BF16), (w_router - wr_hi).astype(BF16)], axis=1)
        h1, u2, z, zt, cnt = _post_mix(ys, oa, gates, h, mod3, w_br_attn[l].astype(BF16),
                                       w_out[l].astype(BF16), row(g_norm_ffn[l]), wr_split, b_router, seq, tm)

        n_tiles = (2 * nb * seq) // EXPERT_TILE + N_EXPERTS
        pad_lo, pad_hi, (pos_out, pos_in), tile_e, n_used = _route_tables(cnt, zt, n_tiles, (tm_moe, tm),
                                                                          EXPERT_TILE)
        n_rows = n_tiles * EXPERT_TILE
        xs = _dispatch(pad_lo, pad_hi, n_used, pos_out, u2, n_rows, tm_moe, EXPERT_TILE)
        ys_e = _experts(tile_e, n_used, xs.reshape(n_rows * SLAB, LANES), w_exp_gate[l], w_exp_up[l],
                        w_exp_down[l], EXPERT_TILE)
        h = _combine(pos_in, ys_e.reshape(n_rows, SLAB, LANES), z, h1, mod3, row(g_final), seq, tm,
                     final_norm=(l == depth - 1))
    return h.reshape(nb, seq, d)
```

```python
import functools
import math

import jax
import jax.numpy as jnp
from jax import lax
from jax.experimental import pallas as pl
from jax.experimental.pallas import tpu as pltpu

EPS = 1e-6
CHUNK = 64
S5_GROUP = 16
S5_STATE = 64
S5_STEP = 16
S5_TILE = 2048
DA_HEADS = 8
DA_HEAD_DIM = 64
N_GROUPS = 4
EXP_PER_GROUP = 8
N_EXPERTS = N_GROUPS * EXP_PER_GROUP
D_EXPERT = 256
EXPERT_TILE = 512
ROUTE_ROWS = 8
LANES = 128
Q_SCALE = DA_HEAD_DIM ** -0.5 * math.log2(math.e)
VMEM_LIMIT = 56 * 1024 * 1024

F32 = jnp.float32
BF16 = jnp.bfloat16
HIGHEST = lax.Precision.HIGHEST


def _cparams(*sem):
    return pltpu.CompilerParams(dimension_semantics=sem, vmem_limit_bytes=VMEM_LIMIT)


def _dot(a, b):
    return jnp.dot(a, b, preferred_element_type=F32)


def _dot_nt(a, b):
    return lax.dot_general(a, b, (((1,), (1,)), ((), ())), preferred_element_type=F32)


def _rms(x, g):
    return x * lax.rsqrt(jnp.mean(x * x, axis=-1, keepdims=True) + EPS) * g


def _ada_kernel(c_ref, w_ref, b_ref, o_ref):
    c = c_ref[...]
    cs = c * jax.nn.sigmoid(c)
    o_ref[...] = jnp.dot(cs, w_ref[...], preferred_element_type=F32, precision=HIGHEST) + b_ref[...]


def _ada_mod(c, w, b):
    nb, d = c.shape
    n = w.shape[1]
    tn = n // 4
    return pl.pallas_call(
        _ada_kernel,
        out_shape=jax.ShapeDtypeStruct((nb, n), F32),
        grid=(n // tn,),
        in_specs=[pl.BlockSpec((nb, d), lambda j: (0, 0)),
                  pl.BlockSpec((d, tn), lambda j: (0, j)),
                  pl.BlockSpec((1, tn), lambda j: (0, j))],
        out_specs=pl.BlockSpec((nb, tn), lambda j: (0, j)),
        compiler_params=_cparams("parallel"),
        name="ada_mod",
    )(c, w, b)


def _modulated_norm(x_ref, mod_ref, g_ref):
    mod = mod_ref[0]
    return _rms(x_ref[...], g_ref[...]) * (1.0 + mod[1:2, :]) + mod[0:1, :]


def _inproj_kernel(x_ref, mod_ref, g_ref, w_ref, b_ref, q_ref, k_ref, v_ref, gt_ref, *, widths):
    ub = _modulated_norm(x_ref, mod_ref, g_ref).astype(BF16)
    qw, kw, vw, gw = widths
    o = 0
    q_ref[...] = ((_dot(ub, w_ref[:, o:o + qw]) + b_ref[:, o:o + qw]) * Q_SCALE).astype(BF16)
    o += qw
    k_ref[...] = (_dot(ub, w_ref[:, o:o + kw]) + b_ref[:, o:o + kw]).astype(BF16)
    o += kw
    v_ref[...] = (_dot(ub, w_ref[:, o:o + vw]) + b_ref[:, o:o + vw]).astype(BF16)
    o += vw
    gt_ref[...] = jax.nn.sigmoid(_dot(ub, w_ref[:, o:o + gw]) + b_ref[:, o:o + gw]).astype(BF16)


def _in_proj(x2, mod3, g, w, b, seq, widths, tm):
    t, d = x2.shape
    tpb = seq // tm
    row = lambda i: (i, 0)
    full = lambda a: pl.BlockSpec(a.shape, lambda i: (0, 0), pipeline_mode=pl.Buffered(1))
    return pl.pallas_call(
        functools.partial(_inproj_kernel, widths=widths),
        out_shape=[jax.ShapeDtypeStruct((t, wd), BF16) for wd in widths],
        grid=(t // tm,),
        in_specs=[pl.BlockSpec((tm, d), row), pl.BlockSpec((1, 6, d), lambda i: (i // tpb, 0, 0)),
                  full(g), full(w), full(b)],
        out_specs=[pl.BlockSpec((tm, wd), row) for wd in widths],
        compiler_params=_cparams("parallel"),
        name="in_proj",
    )(x2, mod3, g, w, b)


def _s5_in_kernel(x_ref, mod_ref, g_ref, wst_ref, bs_ref, ut_ref, slab_ref):
    u = _modulated_norm(x_ref, mod_ref, g_ref)
    nslab = slab_ref.shape[0]
    nk = x_ref.shape[0] // S5_STEP
    for j in range(nslab):
        slab_ref[j] = u[:, j * LANES:(j + 1) * LANES]
    by_step = jnp.concatenate(
        [jnp.concatenate([slab_ref[j, pl.ds(s, nk, stride=S5_STEP), :] for j in range(nslab)], axis=1).astype(BF16)
         for s in range(S5_STEP)], axis=0)
    ut = (_dot_nt(wst_ref[...], by_step) + bs_ref[...]).astype(BF16)
    for s in range(S5_STEP):
        ut_ref[s] = ut[:, s * nk:(s + 1) * nk]


def _s5_in(x2, mod3, g, wst, bs, seq, tm):
    t, d = x2.shape
    s5w = wst.shape[0]
    nkt = tm // S5_STEP
    tpb = seq // tm
    full = lambda a: pl.BlockSpec(a.shape, lambda i: (0, 0))
    return pl.pallas_call(
        _s5_in_kernel,
        out_shape=jax.ShapeDtypeStruct((S5_STEP, s5w, t // S5_STEP), BF16),
        grid=(t // tm,),
        in_specs=[pl.BlockSpec((tm, d), lambda i: (i, 0)), pl.BlockSpec((1, 6, d), lambda i: (i // tpb, 0, 0)),
                  full(g), full(wst), full(bs)],
        out_specs=pl.BlockSpec((S5_STEP, s5w, nkt), lambda i: (0, 0, i)),
        scratch_shapes=[pltpu.VMEM((d // LANES, tm, LANES), F32)],
        compiler_params=_cparams("parallel"),
        name="s5_in",
    )(x2, mod3, g, wst, bs)


def _s5_out_kernel(yt_ref, wglut_ref, bglu_ref, wbs_ref, o_ref, slab_ref):
    nslab = slab_ref.shape[0]
    nk = yt_ref.shape[2]
    yt = jnp.concatenate([yt_ref[s] for s in range(S5_STEP)], axis=1)
    gated = yt.astype(F32) * jax.nn.sigmoid(_dot(wglut_ref[...], yt) + bglu_ref[...])
    for s in range(S5_STEP):
        yg = gated[:, s * nk:(s + 1) * nk].T
        for j in range(nslab):
            slab_ref[j, pl.ds(s, nk, stride=S5_STEP), :] = yg[:, j * LANES:(j + 1) * LANES]
    yg = jnp.concatenate([slab_ref[j] for j in range(nslab)], axis=1)
    o_ref[...] = _dot(yg.astype(BF16), wbs_ref[...]).astype(BF16)


def _s5_out(y_t, wglut, bglu, wbs, tm):
    n, s5w, r = y_t.shape
    d = wbs.shape[1]
    nkt = tm // n
    full = lambda a: pl.BlockSpec(a.shape, lambda i: (0, 0))
    return pl.pallas_call(
        _s5_out_kernel,
        out_shape=jax.ShapeDtypeStruct((r * n, d), BF16),
        grid=(r // nkt,),
        in_specs=[pl.BlockSpec((n, s5w, nkt), lambda i: (0, 0, i)), full(wglut), full(bglu), full(wbs)],
        out_specs=pl.BlockSpec((tm, d), lambda i: (i, 0)),
        scratch_shapes=[pltpu.VMEM((s5w // LANES, tm, LANES), F32)],
        compiler_params=_cparams("parallel"),
        name="s5_out",
    )(y_t, wglut, bglu, wbs)


def _s5_prep_kernel(lrc_ref, lic_ref, dtc_ref, lrr_ref, lir_ref, dtr_ref, br_ref, bi_ref, cr_ref, ci_ref,
                    mt_ref, wtr_ref, wti_ref, vtr_ref, vti_ref, a16r_ref, a16i_ref):
    n, gc, ns = S5_STEP, S5_GROUP, S5_STATE
    w = n * gc
    iota = lambda shape, dim: lax.broadcasted_iota(jnp.int32, shape, dim)
    hdot = lambda a, b: jnp.dot(a, b, preferred_element_type=F32, precision=HIGHEST)

    def power(lr, li, dt, k):
        mag = jnp.exp(lr * dt * k)
        return mag * jnp.cos(li * dt * k), mag * jnp.sin(li * dt * k)

    lr, li, dt = lrc_ref[0], lic_ref[0], jnp.exp(dtc_ref[0])
    abr, abi = power(lr, li, dt, 1.0)
    den = lr * lr + li * li
    nr = abr - 1.0
    f_re = (nr * lr + abi * li) / den
    f_im = (abi * lr - nr * li) / den
    bb_re = f_re * br_ref[0] - f_im * bi_ref[0]
    bb_im = f_re * bi_ref[0] + f_im * br_ref[0]
    spread = jnp.where(iota((gc, 2 * w), 1) % gc == iota((gc, 2 * w), 0), 1.0, 0.0)
    bt_re, bt_im = hdot(bb_re, spread), hdot(bb_im, spread)
    k_in = (n - 1 - (iota((1, 2 * w), 1) % w) // gc).astype(F32)
    p_re, p_im = power(lr, li, dt, k_in)
    e_re = p_re * bt_re - p_im * bt_im
    e_im = p_re * bt_im + p_im * bt_re
    own = iota((2 * ns, 2 * w), 0) // ns == iota((2 * ns, 2 * w), 1) // w
    wtr_ref[0] = jnp.where(own, e_re, 0.0).astype(BF16)
    wti_ref[0] = jnp.where(own, e_im, 0.0).astype(BF16)

    lr_r, li_r, dt_r = lrr_ref[0], lir_ref[0], jnp.exp(dtr_ref[0])
    c_r, c_i = power(lr_r, li_r, dt_r, 1.0)
    for _ in range(n.bit_length() - 1):
        c_r, c_i = c_r * c_r - c_i * c_i, 2.0 * c_r * c_i
    a16r_ref[0], a16i_ref[0] = c_r, c_i
    k_out = ((iota((2 * w, 1), 0) % w) // gc + 1).astype(F32)
    q_re, q_im = power(lr_r, li_r, dt_r, k_out)
    c_re = jnp.concatenate([cr_ref[0]] * (2 * n), axis=0)
    c_im = jnp.concatenate([ci_ref[0]] * (2 * n), axis=0)
    own = iota((2 * w, 2 * ns), 0) // w == iota((2 * w, 2 * ns), 1) // ns
    vtr_ref[0] = jnp.where(own, c_re * q_re - c_im * q_im, 0.0).astype(BF16)
    vti_ref[0] = jnp.where(own, -(c_re * q_im + c_im * q_re), 0.0).astype(BF16)

    group_of_lane = iota((gc, 2 * ns), 1) // ns
    for h in range(2):
        mine = group_of_lane == h
        krev = (hdot(jnp.where(mine, cr_ref[0], 0.0), e_re[:, :w])
                - hdot(jnp.where(mine, ci_ref[0], 0.0), e_im[:, :w]))
        strip = jnp.concatenate([krev, jnp.zeros_like(krev)], axis=1)
        rows = []
        for t in range(n):
            shift = (n - 1 - t) * gc
            rows.append((pltpu.roll(strip, 2 * w - shift, 1) if shift else strip)[:, :w])
        mt_ref[0, h * w:(h + 1) * w, h * w:(h + 1) * w] = jnp.concatenate(rows, axis=0).astype(BF16)
        mt_ref[0, h * w:(h + 1) * w, (1 - h) * w:(2 - h) * w] = jnp.zeros((w, w), BF16)


def _s5_prep(a_re, a_im, log_dt, b_re, b_im, c_re, c_im, d_skip):
    ng, ns = a_re.shape
    gc = b_re.shape[-1]
    na = ng // 2
    w2 = 2 * S5_STEP * gc
    col = lambda a: a.reshape(na, 2 * ns, 1)
    row = lambda a: a.reshape(na, 1, 2 * ns)
    per_state = jnp.repeat(log_dt, ns)
    pair_c = lambda a: a.reshape(na, 2, gc, ns).transpose(0, 2, 1, 3).reshape(na, gc, 2 * ns)
    g3 = lambda g: (g, 0, 0)
    spec = lambda *shape: pl.BlockSpec((1,) + shape, g3)
    shp = lambda *shape, dt=BF16: jax.ShapeDtypeStruct((na,) + shape, dt)
    ops = pl.pallas_call(
        _s5_prep_kernel,
        out_shape=[shp(w2, w2), shp(2 * ns, w2), shp(2 * ns, w2), shp(w2, 2 * ns), shp(w2, 2 * ns),
                   shp(1, 2 * ns, dt=F32), shp(1, 2 * ns, dt=F32)],
        grid=(na,),
        in_specs=[spec(2 * ns, 1)] * 3 + [spec(1, 2 * ns)] * 3 + [spec(2 * ns, gc)] * 2 + [spec(gc, 2 * ns)] * 2,
        out_specs=[spec(w2, w2), spec(2 * ns, w2), spec(2 * ns, w2), spec(w2, 2 * ns), spec(w2, 2 * ns),
                   spec(1, 2 * ns), spec(1, 2 * ns)],
        compiler_params=_cparams("parallel"),
        name="s5_prep",
    )(col(a_re), col(a_im), col(per_state), row(a_re), row(a_im), row(per_state),
      b_re.reshape(na, 2 * ns, gc), b_im.reshape(na, 2 * ns, gc), pair_c(c_re), pair_c(c_im))
    d_col = jnp.broadcast_to(d_skip.reshape(na, 2, 1, gc), (na, 2, S5_STEP, gc)).reshape(na, w2, 1)
    return (*ops, d_col)


def _s5_core_kernel(u_ref, m_ref, wr_ref, wi_ref, vr_ref, vi_ref, ar_ref, ai_ref, d_ref, o_ref,
                    sr_ref, si_ref, xr_ref, xi_ref, *, nb, nk):
    n, gw, r = u_ref.shape
    gc = gw // 2
    u = jnp.concatenate([u_ref[:, h * gc:(h + 1) * gc, :].reshape(n * gc, r) for h in range(2)], axis=0)
    sr_ref[...] = _dot(wr_ref[0], u).T
    si_ref[...] = _dot(wi_ref[0], u).T
    a_r = ar_ref[0]
    a_i = ai_ref[0]

    def step(k, carry):
        out = []
        for b in range(nb):
            x_r, x_i = carry[2 * b], carry[2 * b + 1]
            row = pl.ds(b * nk + k, 1)
            xr_ref[row, :] = x_r
            xi_ref[row, :] = x_i
            out.append(a_r * x_r - a_i * x_i + sr_ref[row, :])
            out.append(a_r * x_i + a_i * x_r + si_ref[row, :])
        return tuple(out)

    zero = jnp.zeros((1, xr_ref.shape[1]), F32)
    lax.fori_loop(0, nk, step, (zero,) * (2 * nb), unroll=4)
    y = (_dot(m_ref[0], u) + _dot(vr_ref[0], xr_ref[...].T.astype(BF16))
         + _dot(vi_ref[0], xi_ref[...].T.astype(BF16)))
    y = y + d_ref[0] * u.astype(F32)
    y = jax.nn.gelu(y).astype(BF16)
    for h in range(2):
        o_ref[:, h * gc:(h + 1) * gc, :] = y[h * n * gc:(h + 1) * n * gc].reshape(n, gc, r)


def _s5_core(u_t, ops, nb, nk):
    m, wr, wi, vr, vi, ar, ai, d_col = ops
    npair, w2, _ = m.shape
    p2 = wr.shape[1]
    n, _, r = u_t.shape
    gw = w2 // n
    g3 = lambda g: (g, 0, 0)
    data = pl.BlockSpec((n, gw, r), lambda g: (0, g, 0))
    return pl.pallas_call(
        functools.partial(_s5_core_kernel, nb=nb, nk=nk),
        out_shape=jax.ShapeDtypeStruct(u_t.shape, BF16),
        grid=(npair,),
        in_specs=[data, pl.BlockSpec((1, w2, w2), g3),
                  pl.BlockSpec((1, p2, w2), g3), pl.BlockSpec((1, p2, w2), g3),
                  pl.BlockSpec((1, w2, p2), g3), pl.BlockSpec((1, w2, p2), g3),
                  pl.BlockSpec((1, 1, p2), g3), pl.BlockSpec((1, 1, p2), g3),
                  pl.BlockSpec((1, w2, 1), g3)],
        out_specs=data,
        scratch_shapes=[pltpu.VMEM((r, p2), F32)] * 4,
        compiler_params=_cparams("parallel"),
        name="s5_core",
    )(u_t, m, wr, wi, vr, vi, ar, ai, d_col)


def _attn_kernel(q_ref, k_ref, v_ref, lq1_ref, lk1_ref, lq2_ref, lk2_ref, g_ref, o_ref,
                 m_ref, acc_ref, *, bq, lambda_init):
    qi = pl.program_id(2)
    q = q_ref[0]
    lane = lax.broadcasted_iota(jnp.int32, (1, 2 * DA_HEAD_DIM), 1)
    zero = jnp.zeros_like(q)
    qs = jnp.concatenate([jnp.where(lane < DA_HEAD_DIM, q, zero),
                          jnp.where(lane >= DA_HEAD_DIM, q, zero)], axis=0)
    hw = 2 * DA_HEAD_DIM
    m_ref[...] = jnp.full(m_ref.shape, -jnp.inf, F32)
    acc_ref[...] = jnp.zeros(acc_ref.shape, F32)
    def absorb(start, size, masked):
        keys = pl.ds(pl.multiple_of(start, bq), size)
        s = _dot_nt(qs, k_ref[0, keys, :])
        if masked:
            r_chunk = (lax.broadcasted_iota(jnp.int32, s.shape, 0) % bq) // CHUNK
            c_chunk = lax.broadcasted_iota(jnp.int32, s.shape, 1) // CHUNK
            s = jnp.where(c_chunk <= r_chunk, s, -jnp.inf)
        m_old = m_ref[...]
        m_new = jnp.maximum(m_old, jnp.max(s, axis=-1, keepdims=True))
        alpha = jnp.exp2(m_old - m_new)
        p = jnp.exp2((s - jnp.concatenate([m_new] * (size // LANES), axis=1)).astype(BF16))
        v_ext = jnp.concatenate([v_ref[0, keys, :], jnp.ones((size, hw), BF16)], axis=1)
        pv = jnp.concatenate([_dot(p[:bq], v_ext), _dot(p[bq:], v_ext)], axis=0)
        acc_ref[...] = jnp.concatenate([alpha, alpha], axis=1) * acc_ref[...] + pv
        m_ref[...] = m_new

    def visible_pair(jj, carry):
        absorb(2 * jj * bq, 2 * bq, False)
        return carry

    lax.fori_loop(0, qi // 2, visible_pair, 0)

    @pl.when(qi % 2 == 1)
    def _():
        absorb((qi - 1) * bq, bq, False)

    absorb(qi * bq, bq, True)

    lam = (jnp.exp(jnp.sum(lq1_ref[...] * lk1_ref[...], axis=-1, keepdims=True))
           - jnp.exp(jnp.sum(lq2_ref[...] * lk2_ref[...], axis=-1, keepdims=True)) + lambda_init)
    acc = acc_ref[...]
    o_all = acc[:, :hw] / acc[:, hw:]
    o = o_all[:bq] - lam * o_all[bq:]
    o_ref[0] = (_rms(o, g_ref[...]) * (1.0 - lambda_init)).astype(BF16)


def _diff_attn(q3, k3, v3, lq1, lk1, lq2, lk2, g_subln, lambda_init, bq):
    nb, seq, _ = q3.shape
    hw = 2 * DA_HEAD_DIM
    lam_spec = pl.BlockSpec((1, DA_HEAD_DIM), lambda b, h, i: (0, 0))
    kv_spec = pl.BlockSpec((1, seq, hw), lambda b, h, i: (b, 0, h))
    return pl.pallas_call(
        functools.partial(_attn_kernel, bq=bq, lambda_init=lambda_init),
        out_shape=jax.ShapeDtypeStruct(q3.shape, BF16),
        grid=(nb, DA_HEADS, seq // bq),
        in_specs=[pl.BlockSpec((1, bq, hw), lambda b, h, i: (b, i, h)), kv_spec, kv_spec,
                  lam_spec, lam_spec, lam_spec, lam_spec,
                  pl.BlockSpec((1, hw), lambda b, h, i: (0, 0))],
        out_specs=pl.BlockSpec((1, bq, hw), lambda b, h, i: (b, i, h)),
        scratch_shapes=[pltpu.VMEM((2 * bq, hw), F32), pltpu.VMEM((2 * bq, 2 * hw), F32)],
        compiler_params=_cparams("parallel", "parallel", "parallel"),
        name="diff_attn",
    )(q3, k3, v3, lq1, lk1, lq2, lk2, g_subln)


def _post_kernel(ys_ref, oa_ref, gt_ref, x_ref, mod_ref, wba_ref, wout_ref, g_ref, wr_ref, br_ref,
                 h_ref, u2_ref, z_ref, zt_ref, cnt_ref, base_ref):
    d = x_ref.shape[1]

    @pl.when(pl.program_id(0) == 0)
    def _():
        base_ref[...] = jnp.zeros(base_ref.shape, F32)

    mod = mod_ref[0]
    y_att = _dot(oa_ref[...], wba_ref[...])
    gates = gt_ref[...].astype(F32)
    mix_in = gates[:, :d] * ys_ref[...].astype(F32) + gates[:, d:] * y_att
    mix = _dot(mix_in.astype(BF16), wout_ref[...])
    h = x_ref[...] + mod[2:3, :] * mix
    h_ref[...] = h
    u2 = _rms(h, g_ref[...]) * (1.0 + mod[4:5, :]) + mod[3:4, :]
    _to_slabs(u2_ref, u2)

    u2_hi = pltpu.bitcast(pltpu.bitcast(u2, jnp.uint32) & jnp.uint32(0xFFFF0000), F32)
    u2_lo = (u2 - u2_hi).astype(BF16)
    u2_hi = u2_hi.astype(BF16)
    hi_w = _dot(u2_hi, wr_ref[...])
    logits = hi_w[:, :LANES] + hi_w[:, LANES:] + _dot(u2_lo, wr_ref[:, :LANES]) + br_ref[...]
    lane = lax.broadcasted_iota(jnp.int32, logits.shape, 1)
    neg = jnp.full_like(logits, -jnp.inf)
    big = jnp.full_like(lane, LANES)
    is_grp = (lane >= N_EXPERTS) & (lane < N_EXPERTS + N_GROUPS)
    glog = jnp.where(is_grp, logits, neg)
    gmax = jnp.max(glog, axis=-1, keepdims=True)
    gp = 1.0 / jnp.sum(jnp.exp(glog - gmax), axis=-1, keepdims=True)
    gi = jnp.min(jnp.where(glog == gmax, lane, big), axis=-1, keepdims=True) - N_EXPERTS
    in_grp = (lane < N_EXPERTS) & ((lane // EXP_PER_GROUP) == gi)
    elog = jnp.where(in_grp, logits, neg)
    v1 = jnp.max(elog, axis=-1, keepdims=True)
    i1 = jnp.min(jnp.where(elog == v1, lane, big), axis=-1, keepdims=True)
    elog2 = jnp.where(lane == i1, neg, elog)
    v2 = jnp.max(elog2, axis=-1, keepdims=True)
    i2 = jnp.min(jnp.where(elog2 == v2, lane, big), axis=-1, keepdims=True)
    e2 = jnp.exp(v2 - v1)
    w1 = gp / (1.0 + e2)
    w2 = gp * e2 / (1.0 + e2)
    tm = logits.shape[0]
    hot = jnp.where((lane == i1) | (lane == i2), 1.0, 0.0)
    earlier = (lax.broadcasted_iota(jnp.int32, (tm, tm), 1) < lax.broadcasted_iota(jnp.int32, (tm, tm), 0))
    before = _dot(jnp.where(earlier, 1.0, 0.0).astype(BF16), hot.astype(BF16)) + base_ref[...]
    rank1 = jnp.sum(jnp.where(lane == i1, before, 0.0), axis=-1, keepdims=True)
    rank2 = jnp.sum(jnp.where(lane == i2, before, 0.0), axis=-1, keepdims=True)
    base_ref[...] += jnp.sum(hot, axis=0, keepdims=True)
    cnt_ref[...] = base_ref[...]
    z = jnp.zeros_like(logits)
    for k, val in enumerate((i1.astype(F32), i2.astype(F32), rank1, rank2, w1, w2)):
        z = jnp.where(lane == k, val, z)
    z_ref[...] = z
    zt_ref[...] = z.T[:ROUTE_ROWS, :]


def _post_mix(ys, oa, gates, x2, mod3, wba, wout, g, wr, br, seq, tm):
    t, d = x2.shape
    tpb = seq // tm
    row = lambda i: (i, 0)
    full = lambda a: pl.BlockSpec(a.shape, lambda i: (0,) * a.ndim)
    return pl.pallas_call(
        _post_kernel,
        out_shape=[jax.ShapeDtypeStruct((t, d), F32), jax.ShapeDtypeStruct((t * SLAB, LANES), F32),
                   jax.ShapeDtypeStruct((t, LANES), F32), jax.ShapeDtypeStruct((ROUTE_ROWS, t), F32),
                   jax.ShapeDtypeStruct((1, LANES), F32)],
        grid=(t // tm,),
        in_specs=[pl.BlockSpec((tm, d), row), pl.BlockSpec((tm, oa.shape[1]), row),
                  pl.BlockSpec((tm, 2 * d), row), pl.BlockSpec((tm, d), row),
                  pl.BlockSpec((1, 6, d), lambda i: (i // tpb, 0, 0)),
                  full(wba), full(wout), full(g), full(wr), full(br)],
        out_specs=[pl.BlockSpec((tm, d), row), pl.BlockSpec((tm * SLAB, LANES), row),
                   pl.BlockSpec((tm, LANES), row),
                   pl.BlockSpec((ROUTE_ROWS, tm), lambda i: (0, i)), pl.BlockSpec((1, LANES), lambda i: (0, 0))],
        scratch_shapes=[pltpu.VMEM((1, LANES), F32)],
        compiler_params=_cparams("arbitrary"),
        name="post_mix",
    )(ys, oa, gates, x2, mod3, wba, wout, g, wr, br)


SLAB = 8


def _to_slabs(ref, val):
    n = val.shape[0]
    for j in range(SLAB):
        ref[pl.ds(j, n, stride=SLAB), :] = val[:, j * LANES:(j + 1) * LANES]


def _from_slabs(ref, n):
    return jnp.concatenate([ref[pl.ds(j, n, stride=SLAB), :] for j in range(SLAB)], axis=1)


def _slab(flat_ref, r):
    return flat_ref.at[pl.ds(pl.multiple_of(r * SLAB, SLAB), SLAB), :]


def _dispatch_kernel(pad_lo_ref, pad_hi_ref, n_used_ref, pos_ref, u_ref, xs_ref, zero_ref, sem, pad_sem):
    tm = u_ref.shape[0] // SLAB
    tile = zero_ref.shape[0]
    n_tiles = xs_ref.shape[0] // tile

    def send(r, carry):
        for k in range(2):
            pltpu.make_async_copy(_slab(u_ref, r), xs_ref.at[pos_ref[k * tm + r]], sem).start(priority=k)
        return carry

    lax.fori_loop(0, tm, send, 0, unroll=8)

    @pl.when(pl.program_id(0) == pl.num_programs(0) - 1)
    def _():
        zero_ref[...] = jnp.zeros(zero_ref.shape, zero_ref.dtype)

        def pad_copies(e, wait):
            lo = pad_lo_ref[e]
            n = pad_hi_ref[e] - lo
            for b in range(tile.bit_length() - 1):
                size = 1 << b

                @pl.when((n >> b) & 1 == 1)
                def _():
                    cp = pltpu.make_async_copy(zero_ref.at[pl.ds(0, size)],
                                               xs_ref.at[pl.ds(lo + (n & (size - 1)), size)], pad_sem)
                    cp.wait() if wait else cp.start()

        def tile_copy(i):
            return pltpu.make_async_copy(zero_ref, xs_ref.at[pl.ds(i * tile, tile)], pad_sem)

        for wait in (False, True):
            @pl.loop(0, N_EXPERTS)
            def _(e):
                pad_copies(e, wait)

            @pl.loop(n_used_ref[0], n_tiles)
            def _(j):
                tile_copy(j).wait() if wait else tile_copy(j).start()

    for _ in range(2 * tm // tile):
        pltpu.make_async_copy(zero_ref, xs_ref.at[pl.ds(0, tile)], sem).wait()


def _dispatch(pad_lo, pad_hi, n_used, pos, u2_flat, n_rows, tm, tile):
    t = u2_flat.shape[0] // SLAB
    return pl.pallas_call(
        _dispatch_kernel,
        out_shape=jax.ShapeDtypeStruct((n_rows, SLAB, LANES), F32),
        grid_spec=pltpu.PrefetchScalarGridSpec(
            num_scalar_prefetch=3,
            grid=(t // tm,),
            in_specs=[pl.BlockSpec((2 * tm,), lambda i, *_: (i,), memory_space=pltpu.SMEM),
                      pl.BlockSpec((tm * SLAB, LANES), lambda i, *_: (i, 0))],
            out_specs=pl.BlockSpec(memory_space=pl.ANY),
            scratch_shapes=[pltpu.VMEM((tile, SLAB, LANES), F32), pltpu.SemaphoreType.DMA,
                            pltpu.SemaphoreType.DMA]),
        compiler_params=_cparams("arbitrary"),
        name="moe_dispatch",
    )(pad_lo, pad_hi, n_used, pos, u2_flat)


def _expert_kernel(tile_e_ref, n_used_ref, x_ref, wg_ref, wu_ref, wd_ref, y_ref):
    used = pl.program_id(0) < n_used_ref[0]

    @pl.when(used)
    def _():
        x = _from_slabs(x_ref, x_ref.shape[0] // SLAB).astype(BF16)
        gate = _dot(x, wg_ref[0].astype(BF16))
        hdn = gate * jax.nn.sigmoid(gate) * _dot(x, wu_ref[0].astype(BF16))
        _to_slabs(y_ref, _dot(hdn.astype(BF16), wd_ref[0].astype(BF16)))

    @pl.when(jnp.logical_not(used))
    def _():
        y_ref[...] = jnp.zeros(y_ref.shape, y_ref.dtype)


def _experts(tile_e, n_used, xs_flat, wg, wu, wd, tm):
    _, d, de = wg.shape
    by_tile = lambda i, te, nu: (te[i], 0, 0)
    return pl.pallas_call(
        _expert_kernel,
        out_shape=jax.ShapeDtypeStruct(xs_flat.shape, F32),
        grid_spec=pltpu.PrefetchScalarGridSpec(
            num_scalar_prefetch=2,
            grid=(xs_flat.shape[0] // (tm * SLAB),),
            in_specs=[pl.BlockSpec((tm * SLAB, LANES), lambda i, te, nu: (jnp.minimum(i, nu[0] - 1), 0)),
                      pl.BlockSpec((1, d, de), by_tile), pl.BlockSpec((1, d, de), by_tile),
                      pl.BlockSpec((1, de, d), by_tile)],
            out_specs=pl.BlockSpec((tm * SLAB, LANES), lambda i, te, nu: (i, 0))),
        compiler_params=_cparams("arbitrary"),
        name="moe_experts",
    )(tile_e, n_used, xs_flat, wg, wu, wd)


def _combine_kernel(pos_ref, pos_next_ref, ys_ref, z_ref, h_ref, mod_ref, g_ref, o_ref, buf_ref, sems,
                    *, final_norm):
    tm = h_ref.shape[0]
    i = pl.program_id(0)

    def fetch(p_ref, slot):
        def one(r, carry):
            for k in range(2):
                pltpu.make_async_copy(ys_ref.at[p_ref[k * tm + r]], _slab(buf_ref.at[slot, k], r),
                                      sems.at[slot]).start(priority=k)
            return carry

        lax.fori_loop(0, tm, one, 0, unroll=8)

    @pl.when(i == 0)
    def _():
        fetch(pos_ref, 0)

    @pl.when(i + 1 < pl.num_programs(0))
    def _():
        fetch(pos_next_ref, (i + 1) % 2)

    slot = i % 2
    for k in range(2):
        pltpu.make_async_copy(buf_ref.at[slot, k], buf_ref.at[slot, k], sems.at[slot]).wait()
    z = z_ref[...]
    ffn = z[:, 4:5] * _from_slabs(buf_ref.at[slot, 0], tm) + z[:, 5:6] * _from_slabs(buf_ref.at[slot, 1], tm)
    h = h_ref[...] + mod_ref[0][5:6, :] * ffn
    o_ref[...] = _rms(h, g_ref[...]) if final_norm else h


def _combine(pos, ys, z, h1, mod3, g_final, seq, tm, final_norm):
    t, d = h1.shape
    tpb = seq // tm
    row = lambda i: (i, 0)
    return pl.pallas_call(
        functools.partial(_combine_kernel, final_norm=final_norm),
        out_shape=jax.ShapeDtypeStruct((t, d), F32),
        grid=(t // tm,),
        in_specs=[pl.BlockSpec((2 * tm,), lambda i: (i,), memory_space=pltpu.SMEM),
                  pl.BlockSpec((2 * tm,), lambda i: (jnp.minimum(i + 1, t // tm - 1),), memory_space=pltpu.SMEM),
                  pl.BlockSpec(memory_space=pl.ANY),
                  pl.BlockSpec((tm, LANES), row), pl.BlockSpec((tm, d), row),
                  pl.BlockSpec((1, 6, d), lambda i: (i // tpb, 0, 0)),
                  pl.BlockSpec((1, d), lambda i: (0, 0))],
        out_specs=pl.BlockSpec((tm, d), row),
        scratch_shapes=[pltpu.VMEM((2, 2, tm * SLAB, LANES), F32), pltpu.SemaphoreType.DMA((2,))],
        compiler_params=_cparams("arbitrary"),
        name="moe_combine",
    )(pos, pos, ys, z, h1, mod3, g_final)


def _route_tables(cnt, zt, n_tiles, tms, tile):
    i32 = jnp.int32
    counts = cnt[0, :N_EXPERTS].astype(i32)
    padded = (counts + tile - 1) // tile * tile
    ends = jnp.cumsum(padded)
    starts = ends - padded
    n_used = (ends[-1] // tile).reshape(1)
    tile_row = jnp.minimum(jnp.arange(n_tiles, dtype=i32), n_used - 1) * tile
    tile_e = jnp.minimum(jnp.sum(tile_row[:, None] >= ends[None, :], axis=1), N_EXPERTS - 1).astype(i32)
    pos1 = jnp.take(starts, zt[0].astype(i32)) + zt[2].astype(i32)
    pos2 = jnp.take(starts, zt[1].astype(i32)) + zt[3].astype(i32)
    pos = [jnp.stack([pos1.reshape(-1, tm), pos2.reshape(-1, tm)], axis=1).reshape(-1) for tm in tms]
    return starts + counts, ends, pos, tile_e, n_used.astype(i32)


def kernel(x, c, w_ada, b_ada, g_norm_mix, w_in, b_in, s5_a_re, s5_a_im, s5_b_re, s5_b_im, s5_c_re, s5_c_im, s5_d, s5_log_dt, w_glu, b_glu, lambda_q1, lambda_k1, lambda_q2, lambda_k2, g_subln, w_br_ssm, w_br_attn, w_out, g_norm_ffn, w_router_grp, b_router_grp, w_router_exp, b_router_exp, w_exp_gate, w_exp_up, w_exp_down, g_final):
    nb, seq, d = x.shape
    depth = w_ada.shape[0]
    s5w = s5_d.shape[1]
    daw = w_br_attn.shape[1]
    ng = s5w // S5_GROUP
    nk = seq // S5_STEP
    assert seq % S5_TILE == 0 and ng % 2 == 0
    widths = (daw, daw, daw, 2 * d)
    tm = 512
    tm_moe = 2048
    bq = 512
    row = lambda a: a.reshape(1, -1)
    col = lambda a: a.reshape(-1, 1)

    h = x.reshape(nb * seq, d)
    for l in range(depth):
        lambda_init = 0.8 - 0.6 * math.exp(-0.3 * l)
        mod3 = _ada_mod(c, w_ada[l], row(b_ada[l])).reshape(nb, 6, d)

        w_in_b = w_in[l].astype(BF16)
        q, k, v, gates = _in_proj(h, mod3, row(g_norm_mix[l]), w_in_b[:, s5w:], row(b_in[l][s5w:]),
                                  seq, widths, 2 * tm)

        u_t = _s5_in(h, mod3, row(g_norm_mix[l]), w_in_b[:, :s5w].T, col(b_in[l][:s5w]), seq, S5_TILE)
        ops = _s5_prep(s5_a_re[l], s5_a_im[l], s5_log_dt[l], s5_b_re[l], s5_b_im[l], s5_c_re[l], s5_c_im[l],
                       s5_d[l])
        y_t = _s5_core(u_t, ops, nb, nk)
        ys = _s5_out(y_t, w_glu[l].T.astype(BF16), col(b_glu[l]), w_br_ssm[l].astype(BF16), S5_TILE)

        as3 = lambda a: a.reshape(nb, seq, daw)
        oa = _diff_attn(as3(q), as3(k), as3(v), row(lambda_q1[l]), row(lambda_k1[l]), row(lambda_q2[l]),
                        row(lambda_k2[l]), row(g_subln[l]), lambda_init, bq).reshape(nb * seq, daw)

        w_router = jnp.zeros((d, LANES), F32)
        w_router = w_router.at[:, :N_EXPERTS].set(w_router_exp[l])
        w_router = w_router.at[:, N_EXPERTS:N_EXPERTS + N_GROUPS].set(w_router_grp[l])
        b_router = jnp.zeros((1, LANES), F32)
        b_router = b_router.at[0, :N_EXPERTS].set(b_router_exp[l])
        b_router = b_router.at[0, N_EXPERTS:N_EXPERTS + N_GROUPS].set(b_router_grp[l])
        wr_hi = lax.bitcast_convert_type(
            lax.bitcast_convert_type(w_router, jnp.uint32) & jnp.uint32(0xFFFF0000), F32)
        wr_split = jnp.concatenate([wr_hi.astype(BF16), (w_router - wr_hi).astype(BF16)], axis=1)
        h1, u2, z, zt, cnt = _post_mix(ys, oa, gates, h, mod3, w_br_attn[l].astype(BF16),
                                       w_out[l].astype(BF16), row(g_norm_ffn[l]), wr_split, b_router, seq, tm)

        n_tiles = (2 * nb * seq) // EXPERT_TILE + N_EXPERTS
        pad_lo, pad_hi, (pos_out, pos_in), tile_e, n_used = _route_tables(cnt, zt, n_tiles, (tm_moe, tm),
                                                                          EXPERT_TILE)
        n_rows = n_tiles * EXPERT_TILE
        xs = _dispatch(pad_lo, pad_hi, n_used, pos_out, u2, n_rows, tm_moe, EXPERT_TILE)
        ys_e = _experts(tile_e, n_used, xs.reshape(n_rows * SLAB, LANES), w_exp_gate[l], w_exp_up[l],
                        w_exp_down[l], EXPERT_TILE)
        h = _combine(pos_in, ys_e.reshape(n_rows, SLAB, LANES), z, h1, mod3, row(g_final), seq, tm,
                     final_norm=(l == depth - 1))
    return h.reshape(nb, seq, d)
```
